```python
import jax
import jax.numpy as jnp
from jax import lax
import numpy as np

D_MODEL = 1024
BATCH = 8
SEQ = 4096
DEPTH = 1

CHUNK = 64
RET_HEADS = 4
RET_DK = 128
RET_DV = 256
GLA_HEADS = 4
GLA_DK = 128
GLA_DV = 256
GLA_GATE_RANK = 16
GLA_GATE_TAU = 16.0
ROPE_THETA = 10000.0
N_EXPERTS = 256
TOP_K = 8
N_GROUPS = 8
TOPK_GROUPS = 4
D_EXPERT = 256
D_SHARED = 256
ROUTED_SCALE = 2.5
MOE_BLOCK = 128
LN_EPS = 1e-5
NORM_EPS = 1e-6
DN_ALPHA = (2.0 * DEPTH) ** 0.25
DN_BETA = (8.0 * DEPTH) ** -0.25

RET_QK = RET_HEADS * RET_DK
RET_V = RET_HEADS * RET_DV
GLA_QK = GLA_HEADS * GLA_DK
GLA_V = GLA_HEADS * GLA_DV
IN_SPLITS = (RET_QK, RET_QK, RET_V, RET_V, GLA_QK, GLA_QK, GLA_V, GLA_V, GLA_GATE_RANK, 2 * D_MODEL)
D_IN = sum(IN_SPLITS)

kernel_name = 'hybrid_retention_gla_moe_deepnorm'


def _layernorm(x, g, b, out_dtype):
    xf = x.astype(jnp.float32)
    mu = jnp.mean(xf, -1, keepdims=True)
    xc = xf - mu
    var = jnp.mean(xc * xc, -1, keepdims=True)
    return (xc * lax.rsqrt(var + LN_EPS) * g.astype(jnp.float32) + b.astype(jnp.float32)).astype(out_dtype)


def _head_layernorm(y, g, n_heads):
    B, S, W = y.shape
    yh = y.reshape(B, S, n_heads, W // n_heads)
    yc = yh - jnp.mean(yh, -1, keepdims=True)
    var = jnp.mean(yc * yc, -1, keepdims=True)
    return (yc * lax.rsqrt(var + LN_EPS)).reshape(B, S, W) * g.astype(jnp.float32)


def _head_rmsnorm(y, g, n_heads):
    B, S, W = y.shape
    yh = y.reshape(B, S, n_heads, W // n_heads)
    ms = jnp.mean(yh * yh, -1, keepdims=True)
    return (yh * lax.rsqrt(ms + NORM_EPS)).reshape(B, S, W) * g.astype(jnp.float32)


def _rope(t, pos):
    half = t.shape[-1] // 2
    inv = ROPE_THETA ** (-jnp.arange(half, dtype=jnp.float32) / half)
    ang = pos[:, None] * inv[None, :]
    cos = jnp.cos(ang)[None, :, None, :]
    sin = jnp.sin(ang)[None, :, None, :]
    t1, t2 = t[..., :half], t[..., half:]
    return jnp.concatenate([t1 * cos - t2 * sin, t1 * sin + t2 * cos], axis=-1)


def _to_chunks(t):
    B, S, H, d = t.shape
    return t.reshape(B, S // CHUNK, CHUNK, H, d).transpose(1, 0, 3, 2, 4)


def _from_chunks(t):
    N, B, H, C, d = t.shape
    return t.transpose(1, 0, 3, 2, 4).reshape(B, N * C, H * d)


def _retention(q, k, v):
    B, S, H, dk = q.shape
    dv = v.shape[-1]
    h = jnp.arange(H, dtype=jnp.float32)
    log_g = jnp.log1p(-jnp.exp2(-5.0 - h))
    j = jnp.arange(CHUNK, dtype=jnp.float32)
    dist = jnp.abs(j[:, None] - j[None, :])
    intra_decay = jnp.exp(log_g[:, None, None] * dist)
    q_decay = jnp.exp(log_g[:, None] * (j[None, :] + 1.0))
    k_decay = jnp.exp(log_g[:, None] * (CHUNK - 1.0 - j[None, :]))
    chunk_decay = jnp.exp(log_g * CHUNK)
    k = k * (dk ** -0.5)

    def step(state, inp):
        qc, kc, vc = inp
        scores = jnp.einsum('bhjd,bhmd->bhjm', qc, kc) * intra_decay
        o = jnp.einsum('bhjm,bhme->bhje', scores, vc)
        o = o + jnp.einsum('bhjd,bhde->bhje', qc, state) * q_decay[:, :, None]
        state = chunk_decay[:, None, None] * state + jnp.einsum('bhmd,bhme->bhde', kc * k_decay[:, :, None], vc)
        return state, o

    s0 = jnp.zeros((B, H, dk, dv), jnp.float32)
    _, o = lax.scan(step, s0, (_to_chunks(q), _to_chunks(k), _to_chunks(v)))
    return _from_chunks(o)


def _gla(q, k, v, log_a):
    B, S, H, dk = q.shape
    dv = v.shape[-1]
    q = q * (dk ** -0.5)

    def step(state, inp):
        qc, kc, vc, ac = inp
        b = jnp.cumsum(ac, axis=2)
        b_end = b[:, :, -1:, :]
        kt = kc * jnp.exp(b_end - b)
        state = jnp.exp(b_end[:, :, 0, :])[..., None] * state + jnp.einsum('bhmd,bhme->bhde', kt, vc)
        o = jnp.einsum('bhjd,bhde->bhje', qc, state)
        return state, o

    s0 = jnp.zeros((B, H, dk, dv), jnp.float32)
    _, o = lax.scan(step, s0, (_to_chunks(q), _to_chunks(k), _to_chunks(v), _to_chunks(log_a)))
    return _from_chunks(o)


def _moe(x, router_w, router_bias, w_gate, w_up, w_down, sw_gate, sw_up, sw_down):
    B, S, D = x.shape
    T = B * S
    E = N_EXPERTS
    xt = x.reshape(T, D)
    scores = jax.nn.sigmoid(jnp.einsum('td,de->te', xt, router_w).astype(jnp.float32))
    biased = scores + router_bias.astype(jnp.float32)
    grp = biased.reshape(T, N_GROUPS, E // N_GROUPS)
    grp_score = lax.top_k(grp, 2)[0].sum(-1)
    _, grp_idx = lax.top_k(grp_score, TOPK_GROUPS)
    grp_mask = jax.nn.one_hot(grp_idx, N_GROUPS, dtype=jnp.float32).sum(1) > 0
    exp_mask = jnp.repeat(grp_mask, E // N_GROUPS, axis=1)
    _, idx = lax.top_k(jnp.where(exp_mask, biased, -jnp.inf), TOP_K)
    w = jnp.take_along_axis(scores, idx, axis=1)
    w = w / jnp.sum(w, -1, keepdims=True) * ROUTED_SCALE

    TK = T * TOP_K
    flat_e = idx.reshape(TK).astype(jnp.int32)
    flat_tok = jnp.repeat(jnp.arange(T, dtype=jnp.int32), TOP_K)
    flat_w = w.reshape(TK)
    order = jnp.argsort(flat_e)
    sorted_e = flat_e[order]
    counts = jnp.zeros((E,), jnp.int32).at[flat_e].add(1)
    starts = jnp.cumsum(counts) - counts
    padded = (counts + MOE_BLOCK - 1) // MOE_BLOCK * MOE_BLOCK
    pad_ends = jnp.cumsum(padded)
    pad_starts = pad_ends - padded
    dest = pad_starts[sorted_e] + jnp.arange(TK, dtype=jnp.int32) - starts[sorted_e]
    n_blocks = -(-(TK + E * (MOE_BLOCK - 1)) // MOE_BLOCK)
    R = n_blocks * MOE_BLOCK
    row_tok = jnp.full((R,), T, jnp.int32).at[dest].set(flat_tok[order])
    row_w = jnp.zeros((R,), jnp.float32).at[dest].set(flat_w[order])
    block_start = jnp.arange(n_blocks, dtype=jnp.int32) * MOE_BLOCK
    block_e = jnp.minimum(jnp.searchsorted(pad_ends, block_start, side='right'), E - 1).astype(jnp.int32)
    x_pad = jnp.concatenate([xt, jnp.zeros((1, D), xt.dtype)], axis=0)

    def block_fn(args):
        e, tok, wt = args
        xb = x_pad[tok]
        hb = jax.nn.silu(xb @ w_gate[e]) * (xb @ w_up[e])
        return (hb @ w_down[e]) * wt[:, None]

    ys = lax.map(block_fn, (block_e, row_tok.reshape(n_blocks, MOE_BLOCK), row_w.reshape(n_blocks, MOE_BLOCK)))
    routed = jnp.zeros((T + 1, D), ys.dtype).at[row_tok].add(ys.reshape(R, D))[:T]
    shared = (jax.nn.silu(xt @ sw_gate) * (xt @ sw_up)) @ sw_down
    return (routed.astype(x.dtype) + shared).reshape(B, S, D)


def setup_inputs(seed: int = 0) -> dict:
    key = jax.random.key(seed)
    ks = jax.random.split(key, 40)
    f32 = jnp.float32

    def nrm(k, shape, scale):
        return jax.random.normal(k, shape, f32) * scale

    sD = D_MODEL ** -0.5
    in_scales = (sD, sD, sD * DN_BETA, sD, sD, sD, sD * DN_BETA, sD, sD, sD)
    w_in = jnp.concatenate([nrm(ks[1 + i], (DEPTH, D_MODEL, wd), sc) for i, (wd, sc) in enumerate(zip(IN_SPLITS, in_scales))], axis=-1)
    return {
        'x': nrm(ks[0], (BATCH, SEQ, D_MODEL), 1.0),
        'w_in': w_in,
        'ret_norm_g': 1.0 + nrm(ks[11], (DEPTH, RET_V), 0.1),
        'gla_gate_w2': nrm(ks[12], (DEPTH, GLA_GATE_RANK, GLA_QK), GLA_GATE_RANK ** -0.5),
        'gla_gate_b': nrm(ks[13], (DEPTH, GLA_QK), 0.1),
        'gla_norm_g': 1.0 + nrm(ks[14], (DEPTH, GLA_V), 0.1),
        'w_ret_out': nrm(ks[15], (DEPTH, RET_V, D_MODEL), RET_V ** -0.5 * DN_BETA),
        'w_gla_out': nrm(ks[16], (DEPTH, GLA_V, D_MODEL), GLA_V ** -0.5 * DN_BETA),
        'w_o': nrm(ks[17], (DEPTH, D_MODEL, D_MODEL), sD * DN_BETA),
        'ln1_g': 1.0 + nrm(ks[18], (DEPTH, D_MODEL), 0.1),
        'ln1_b': nrm(ks[19], (DEPTH, D_MODEL), 0.02),
        'router_w': nrm(ks[20], (DEPTH, D_MODEL, N_EXPERTS), sD),
        'router_bias': nrm(ks[21], (DEPTH, N_EXPERTS), 0.01),
        'exp_w_gate': nrm(ks[22], (DEPTH, N_EXPERTS, D_MODEL, D_EXPERT), sD),
        'exp_w_up': nrm(ks[23], (DEPTH, N_EXPERTS, D_MODEL, D_EXPERT), sD),
        'exp_w_down': nrm(ks[24], (DEPTH, N_EXPERTS, D_EXPERT, D_MODEL), D_EXPERT ** -0.5 * DN_BETA),
        'shared_w_gate': nrm(ks[25], (DEPTH, D_MODEL, D_SHARED), sD),
        'shared_w_up': nrm(ks[26], (DEPTH, D_MODEL, D_SHARED), sD),
        'shared_w_down': nrm(ks[27], (DEPTH, D_SHARED, D_MODEL), D_SHARED ** -0.5 * DN_BETA),
        'ln2_g': 1.0 + nrm(ks[28], (DEPTH, D_MODEL), 0.1),
        'ln2_b': nrm(ks[29], (DEPTH, D_MODEL), 0.02),
    }


def reference(x, w_in, ret_norm_g, gla_gate_w2, gla_gate_b, gla_norm_g, w_ret_out, w_gla_out, w_o, ln1_g, ln1_b, router_w, router_bias, exp_w_gate, exp_w_up, exp_w_down, shared_w_gate, shared_w_up, shared_w_down, ln2_g, ln2_b):
    dt = x.dtype
    B, S, _ = x.shape
    f32 = jnp.float32
    pos = jnp.arange(S, dtype=f32)
    offsets = np.cumsum(IN_SPLITS)[:-1].tolist()
    for l in range(DEPTH):
        proj = jnp.einsum('bsd,de->bse', x, w_in[l]).astype(f32)
        rq, rk, rv, rg, gq, gk, gv, gg, ga, mg = jnp.split(proj, offsets, axis=-1)
        rq = _rope(rq.reshape(B, S, RET_HEADS, RET_DK), pos)
        rk = _rope(rk.reshape(B, S, RET_HEADS, RET_DK), pos)
        y_ret = _retention(rq, rk, rv.reshape(B, S, RET_HEADS, RET_DV))
        y_ret = _head_layernorm(y_ret, ret_norm_g[l], RET_HEADS) * jax.nn.silu(rg)
        log_a = jax.nn.log_sigmoid(ga @ gla_gate_w2[l].astype(f32) + gla_gate_b[l].astype(f32)) / GLA_GATE_TAU
        y_gla = _gla(gq.reshape(B, S, GLA_HEADS, GLA_DK), gk.reshape(B, S, GLA_HEADS, GLA_DK),
                     gv.reshape(B, S, GLA_HEADS, GLA_DV), log_a.reshape(B, S, GLA_HEADS, GLA_DK))
        y_gla = _head_rmsnorm(y_gla, gla_norm_g[l], GLA_HEADS) * jax.nn.silu(gg)
        u_ret = jnp.einsum('bsv,vd->bsd', y_ret.astype(dt), w_ret_out[l])
        u_gla = jnp.einsum('bsv,vd->bsd', y_gla.astype(dt), w_gla_out[l])
        gate = jax.nn.sigmoid(mg)
        merged = gate[..., :D_MODEL] * u_ret + gate[..., D_MODEL:] * u_gla
        mix = jnp.einsum('bsd,de->bse', merged.astype(dt), w_o[l])
        x = _layernorm(DN_ALPHA * x + mix, ln1_g[l], ln1_b[l], dt)
        ffn = _moe(x, router_w[l], router_bias[l], exp_w_gate[l], exp_w_up[l], exp_w_down[l],
                   shared_w_gate[l], shared_w_up[l], shared_w_down[l])
        x = _layernorm(DN_ALPHA * x + ffn, ln2_g[l], ln2_b[l], dt)
    return x
```

```python
import functools

import jax
import jax.numpy as jnp
import numpy as np
from jax import lax
from jax.experimental import pallas as pl
from jax.experimental.pallas import tpu as pltpu

CHUNK = 64
RET_HEADS = 4
RET_DK = 128
RET_DV = 256
GLA_HEADS = 4
GLA_DK = 128
GLA_DV = 256
GLA_GATE_RANK = 16
GLA_GATE_TAU = 16.0
ROPE_THETA = 10000.0
N_EXPERTS = 256
TOP_K = 8
N_GROUPS = 8
TOPK_GROUPS = 4
ROUTED_SCALE = 2.5
LN_EPS = 1e-5
NORM_EPS = 1e-6

V7X_LANES = 128
V7X_VMEM_LIMIT = 60 * 1024 * 1024

MIX_ROWS = 256
ROUTE_COLS = 512
ROW_BLOCK = 128
COMBINE_ROWS = 256

F32 = jnp.float32
BF16 = jnp.bfloat16


def _dot(a, b):
    return jnp.dot(a, b, preferred_element_type=F32)


def _dot_nt(a, b):
    return lax.dot_general(a, b, (((1,), (1,)), ((), ())), preferred_element_type=F32)


def _dot_tn(a, b):
    return lax.dot_general(a, b, (((0,), (0,)), ((), ())), preferred_element_type=F32)


def _sigmoid(v):
    return 1.0 / (1.0 + jnp.exp(-v))


def _silu(v):
    return v * _sigmoid(v)


def _layernorm_rows(v, g, b):
    mu = jnp.mean(v, axis=-1, keepdims=True)
    vc = v - mu
    var = jnp.mean(vc * vc, axis=-1, keepdims=True)
    return vc * lax.rsqrt(var + LN_EPS) * g + b


def _mix_kernel(x_ref, wrq_ref, wrk_ref, wrv_ref, wrg_ref, wgq_ref, wgk_ref, wgv_ref, wgg_ref, wga_ref, wmg_ref,
                cos_ref, sin_ref, dmask_ref, qdec_ref, kdec_ref, tri_ref, w2_ref, gb_ref, retg_ref, glag_ref,
                wro_ref, wgo_ref, wo_ref, ln1g_ref, ln1b_ref,
                out_ref, rstate_ref, gstate_ref, yret_ref, ygla_ref, *, block_decay, alpha):
    rows = x_ref.shape[0]

    @pl.when(pl.program_id(1) == 0)
    def _():
        rstate_ref[...] = jnp.zeros_like(rstate_ref)
        gstate_ref[...] = jnp.zeros_like(gstate_ref)

    x = x_ref[...]
    xb = x.astype(BF16)

    def proj(w_ref):
        return _dot(xb, w_ref[...])

    cos = cos_ref[...]
    sin = sin_ref[...]

    def rope(t):
        return t * cos + pltpu.roll(t, RET_DK // 2, 1) * sin

    rq = proj(wrq_ref)
    rk = proj(wrk_ref)
    rv = proj(wrv_ref)
    rg = proj(wrg_ref)
    for h in range(RET_HEADS):
        qk = slice(h * RET_DK, (h + 1) * RET_DK)
        vv = slice(h * RET_DV, (h + 1) * RET_DV)
        q = rope(rq[:, qk])
        k = rope(rk[:, qk])
        v = rv[:, vv].astype(BF16)
        scores = _dot_nt(q.astype(BF16), k.astype(BF16)) * dmask_ref[h]
        o = _dot(scores.astype(BF16), v)
        state = rstate_ref[h]
        o = o + _dot((q * qdec_ref[h]).astype(BF16), state.astype(BF16))
        rstate_ref[h] = state * block_decay[h] + _dot_tn((k * kdec_ref[h]).astype(BF16), v)
        mu = jnp.mean(o, axis=-1, keepdims=True)
        oc = o - mu
        var = jnp.mean(oc * oc, axis=-1, keepdims=True)
        y = oc * lax.rsqrt(var + LN_EPS) * retg_ref[:, vv] * _silu(rg[:, vv])
        yret_ref[:, vv] = y.astype(BF16)

    gq = proj(wgq_ref) * (GLA_DK ** -0.5)
    gk = proj(wgk_ref)
    gv = proj(wgv_ref)
    gg = proj(wgg_ref)
    ga = proj(wga_ref)
    z = _dot(ga.astype(BF16), w2_ref[...]) + gb_ref[...]
    log_a = (jnp.minimum(z, 0.0) - jnp.log1p(jnp.exp(-jnp.abs(z)))) * (1.0 / GLA_GATE_TAU)
    la_hi = log_a.astype(BF16)
    la_lo = (log_a - la_hi.astype(F32)).astype(BF16)
    tri = tri_ref[...]
    bcum = _dot(tri, la_hi) + _dot(tri, la_lo)
    for c in range(rows // CHUNK):
        rs = slice(c * CHUNK, (c + 1) * CHUNK)
        b_end = bcum[(c + 1) * CHUNK - 1:(c + 1) * CHUNK, :]
        kt = gk[rs, :] * jnp.exp(b_end - bcum[rs, :])
        dec = jnp.exp(b_end)
        for h in range(GLA_HEADS):
            qk = slice(h * GLA_DK, (h + 1) * GLA_DK)
            vv = slice(h * GLA_DV, (h + 1) * GLA_DV)
            state_t = gstate_ref[h] * dec[:, qk] + _dot_tn(gv[rs, vv].astype(BF16), kt[:, qk].astype(BF16))
            gstate_ref[h] = state_t
            o = _dot_nt(gq[rs, qk].astype(BF16), state_t.astype(BF16))
            ms = jnp.mean(o * o, axis=-1, keepdims=True)
            y = o * lax.rsqrt(ms + NORM_EPS) * glag_ref[:, vv] * _silu(gg[rs, vv])
            ygla_ref[rs, vv] = y.astype(BF16)

    d_model = x.shape[1]
    u_ret = _dot(yret_ref[...], wro_ref[...])
    u_gla = _dot(ygla_ref[...], wgo_ref[...])
    gate = _sigmoid(proj(wmg_ref))
    merged = gate[:, :d_model] * u_ret + gate[:, d_model:] * u_gla
    mix = _dot(merged.astype(BF16), wo_ref[...])
    out_ref[...] = _layernorm_rows(alpha * x + mix, ln1g_ref[...], ln1b_ref[...])


def _mix_tables(seq, rows):
    half = RET_DK // 2
    inv = ROPE_THETA ** (-np.arange(half, dtype=np.float64) / half)
    ang = np.arange(seq, dtype=np.float64)[:, None] * inv[None, :]
    cos2 = np.concatenate([np.cos(ang), np.cos(ang)], axis=1)
    sin2 = np.concatenate([-np.sin(ang), np.sin(ang)], axis=1)
    log_g = np.log1p(-np.exp2(-5.0 - np.arange(RET_HEADS, dtype=np.float64)))
    j = np.arange(rows, dtype=np.float64)
    same_or_earlier_chunk = (j[None, :] // CHUNK) <= (j[:, None] // CHUNK)
    k_scale = RET_DK ** -0.5
    dmask = np.exp(log_g[:, None, None] * np.abs(j[:, None] - j[None, :])) * same_or_earlier_chunk[None] * k_scale
    qdec = np.exp(log_g[:, None] * (j[None, :] + 1.0))
    kdec = np.exp(log_g[:, None] * (rows - 1.0 - j[None, :])) * k_scale
    qdec = np.broadcast_to(qdec[:, :, None], (RET_HEADS, rows, RET_DK))
    kdec = np.broadcast_to(kdec[:, :, None], (RET_HEADS, rows, RET_DK))
    block_decay = tuple(float(v) for v in np.exp(log_g * rows))
    tri = ((j[None, :] <= j[:, None]) & ((j[None, :] // CHUNK) == (j[:, None] // CHUNK)))
    to = lambda a, dt: jnp.asarray(np.ascontiguousarray(a), dtype=dt)
    return (to(cos2, F32), to(sin2, F32), to(dmask, F32), to(qdec, F32), to(kdec, F32), to(tri, BF16)), block_decay


def _const_spec(shape):
    nd = len(shape)
    return pl.BlockSpec(shape, lambda *_: (0,) * nd, pipeline_mode=pl.Buffered(1))


def _mix(x, w_in, ret_norm_g, gla_gate_w2, gla_gate_b, gla_norm_g, w_ret_out, w_gla_out, w_o, ln1_g, ln1_b, alpha):
    batch, seq, d_model = x.shape
    rows = MIX_ROWS
    assert seq % rows == 0 and rows % CHUNK == 0
    ret_qk, ret_v = RET_HEADS * RET_DK, RET_HEADS * RET_DV
    gla_qk, gla_v = GLA_HEADS * GLA_DK, GLA_HEADS * GLA_DV
    splits = (ret_qk, ret_qk, ret_v, ret_v, gla_qk, gla_qk, gla_v, gla_v, GLA_GATE_RANK, 2 * d_model)
    assert w_in.shape == (d_model, sum(splits))
    offs = np.cumsum((0,) + splits)
    parts = [w_in[:, offs[i]:offs[i + 1]].astype(BF16) for i in range(len(splits))]
    parts[8] = jnp.pad(parts[8], ((0, 0), (0, V7X_LANES - GLA_GATE_RANK)))
    w2 = jnp.pad(gla_gate_w2.astype(BF16), ((0, V7X_LANES - GLA_GATE_RANK), (0, 0)))
    (cos2, sin2, dmask, qdec, kdec, tri), block_decay = _mix_tables(seq, rows)
    row2 = lambda a: a.reshape(1, -1).astype(F32)
    consts = [dmask, qdec, kdec, tri, w2, row2(gla_gate_b), row2(ret_norm_g), row2(gla_norm_g),
              w_ret_out.astype(BF16), w_gla_out.astype(BF16), w_o.astype(BF16), row2(ln1_g), row2(ln1_b)]
    pos_spec = pl.BlockSpec((rows, RET_DK), lambda b, s: (s, 0))
    in_specs = ([pl.BlockSpec((None, rows, d_model), lambda b, s: (b, s, 0))]
                + [_const_spec(p.shape) for p in parts]
                + [pos_spec, pos_spec]
                + [_const_spec(c.shape) for c in consts])
    return pl.pallas_call(
        functools.partial(_mix_kernel, block_decay=block_decay, alpha=alpha),
        grid=(batch, seq // rows),
        in_specs=in_specs,
        out_specs=pl.BlockSpec((None, rows, d_model), lambda b, s: (b, s, 0)),
        out_shape=jax.ShapeDtypeStruct((batch, seq, d_model), F32),
        scratch_shapes=[pltpu.VMEM((RET_HEADS, RET_DK, RET_DV), F32),
                        pltpu.VMEM((GLA_HEADS, GLA_DV, GLA_DK), F32),
                        pltpu.VMEM((rows, ret_v), BF16),
                        pltpu.VMEM((rows, gla_v), BF16)],
        compiler_params=pltpu.CompilerParams(dimension_semantics=("arbitrary", "arbitrary"),
                                             vmem_limit_bytes=V7X_VMEM_LIMIT),
        name="mix",
    )(x, *parts, cos2, sin2, *consts)


def _route_kernel(x_ref, rwt_ref, bias_ref, triu_ref, ones_ref,
                  idx_ref, w_ref, rank_ref, counts_ref, carry_ref):
    cols = x_ref.shape[0]
    n_exp = rwt_ref.shape[0]
    per_group = n_exp // N_GROUPS
    neg_inf = -jnp.inf

    @pl.when(pl.program_id(0) == 0)
    def _():
        carry_ref[...] = jnp.zeros_like(carry_ref)

    logits = _dot_nt(rwt_ref[...], x_ref[...].astype(BF16))
    scores = _sigmoid(logits)
    biased = scores + bias_ref[...]

    sub = lax.broadcasted_iota(jnp.int32, (per_group, cols), 0)
    gscore = []
    for g in range(N_GROUPS):
        blk = biased[g * per_group:(g + 1) * per_group, :]
        m1 = jnp.max(blk, axis=0, keepdims=True)
        i1 = jnp.min(jnp.where(blk == m1, sub, per_group), axis=0, keepdims=True)
        m2 = jnp.max(jnp.where(sub == i1, neg_inf, blk), axis=0, keepdims=True)
        gscore.append(m1 + m2)
    masked = []
    for g in range(N_GROUPS):
        ahead = jnp.zeros((1, cols), jnp.int32)
        for o in range(N_GROUPS):
            if o == g:
                continue
            before = (gscore[o] >= gscore[g]) if o < g else (gscore[o] > gscore[g])
            ahead = ahead + before.astype(jnp.int32)
        keep = ahead < TOPK_GROUPS
        blk = biased[g * per_group:(g + 1) * per_group, :]
        masked.append(jnp.where(keep, blk, neg_inf))
    candidates = jnp.concatenate(masked, axis=0)
    cur = candidates

    rowid = lax.broadcasted_iota(jnp.int32, (n_exp, cols), 0)
    picked = []
    weights = []
    for _ in range(TOP_K):
        m = jnp.max(cur, axis=0, keepdims=True)
        ik = jnp.min(jnp.where(cur == m, rowid, n_exp), axis=0, keepdims=True)
        sel = rowid == ik
        weights.append(jnp.sum(jnp.where(sel, scores, 0.0), axis=0, keepdims=True))
        cur = jnp.where(sel, neg_inf, cur)
        picked.append(ik)
    wsum = weights[0]
    for wk in weights[1:]:
        wsum = wsum + wk

    chosen = (cur == neg_inf) & (candidates != neg_inf)
    chosen_b = jnp.where(chosen, 1.0, 0.0).astype(BF16)
    carry = carry_ref[...]
    before = _dot(chosen_b, triu_ref[...]) + jnp.concatenate([carry] * (cols // V7X_LANES), axis=1)
    for k in range(TOP_K):
        sel = rowid == picked[k]
        rank_ref[k:k + 1, :] = jnp.sum(jnp.where(sel, before, 0.0), axis=0, keepdims=True).astype(jnp.int32)
        idx_ref[k:k + 1, :] = picked[k]
        w_ref[k:k + 1, :] = weights[k] / wsum * ROUTED_SCALE
    carry = carry + _dot(chosen_b, ones_ref[...])
    carry_ref[...] = carry
    counts_ref[...] = carry


def _route(x1, router_w, router_bias):
    tokens, d_model = x1.shape
    n_exp = router_w.shape[1]
    cols = ROUTE_COLS
    assert tokens % cols == 0 and n_exp % N_GROUPS == 0
    j = np.arange(cols)
    triu = jnp.asarray((j[:, None] < j[None, :]), dtype=BF16)
    ones = jnp.ones((cols, V7X_LANES), BF16)
    out_row = lambda dt: jax.ShapeDtypeStruct((TOP_K, tokens), dt)
    row_spec = pl.BlockSpec((TOP_K, cols), lambda i: (0, i))
    return pl.pallas_call(
        _route_kernel,
        grid=(tokens // cols,),
        in_specs=[pl.BlockSpec((cols, d_model), lambda i: (i, 0)),
                  _const_spec((n_exp, d_model)),
                  _const_spec((n_exp, 1)),
                  _const_spec((cols, cols)),
                  _const_spec((cols, V7X_LANES))],
        out_specs=[row_spec, row_spec, row_spec, pl.BlockSpec((n_exp, V7X_LANES), lambda i: (0, 0))],
        out_shape=[out_row(jnp.int32), out_row(F32), out_row(jnp.int32),
                   jax.ShapeDtypeStruct((n_exp, V7X_LANES), F32)],
        scratch_shapes=[pltpu.VMEM((n_exp, V7X_LANES), F32)],
        compiler_params=pltpu.CompilerParams(dimension_semantics=("arbitrary",), vmem_limit_bytes=V7X_VMEM_LIMIT),
        name="route",
    )(x1, router_w.T.astype(BF16), router_bias.reshape(n_exp, 1).astype(F32), triu, ones)


def _experts_kernel(block_e_ref, n_used_ref, tok_ref, x_hbm, wg_ref, wu_ref, wd_ref,
                    y_ref, xbuf_ref, wg_b, wu_b, wd_b, sem):
    i = pl.program_id(0)
    rows = xbuf_ref.shape[0]

    def row_copy(r):
        return pltpu.make_async_copy(x_hbm.at[pl.ds(tok_ref[0, 0, r], 1), :], xbuf_ref.at[pl.ds(r, 1), :], sem)

    @pl.when(i < n_used_ref[0])
    def _():
        def issue(r, c):
            row_copy(r).start()
            return c
        lax.fori_loop(0, rows, issue, 0)

        @pl.when((i == 0) | (block_e_ref[i] != block_e_ref[jnp.maximum(i - 1, 0)]))
        def _():
            wg_b[...] = wg_ref[...].astype(BF16)
            wu_b[...] = wu_ref[...].astype(BF16)
            wd_b[...] = wd_ref[...].astype(BF16)

        def drain(r, c):
            row_copy(r).wait()
            return c
        lax.fori_loop(0, rows, drain, 0)

        xb = xbuf_ref[...].astype(BF16)
        hidden = _silu(_dot(xb, wg_b[...])) * _dot(xb, wu_b[...])
        y_ref[...] = _dot(hidden.astype(BF16), wd_b[...])

    @pl.when(i >= n_used_ref[0])
    def _():
        y_ref[...] = jnp.zeros_like(y_ref)


def _experts(x1, row_tok, block_e, n_used, w_gate, w_up, w_down):
    tokens, d_model = x1.shape
    n_exp, _, d_exp = w_gate.shape
    n_blocks = block_e.shape[0]
    rows = ROW_BLOCK
    grid_spec = pltpu.PrefetchScalarGridSpec(
        num_scalar_prefetch=2,
        grid=(n_blocks,),
        in_specs=[pl.BlockSpec((1, 1, rows), lambda i, be, nu: (i, 0, 0), memory_space=pltpu.SMEM),
                  pl.BlockSpec(memory_space=pl.ANY),
                  pl.BlockSpec((None, d_model, d_exp), lambda i, be, nu: (be[i], 0, 0)),
                  pl.BlockSpec((None, d_model, d_exp), lambda i, be, nu: (be[i], 0, 0)),
                  pl.BlockSpec((None, d_exp, d_model), lambda i, be, nu: (be[i], 0, 0))],
        out_specs=pl.BlockSpec((rows, d_model), lambda i, be, nu: (i, 0)),
        scratch_shapes=[pltpu.VMEM((rows, d_model), F32),
                        pltpu.VMEM((d_model, d_exp), BF16),
                        pltpu.VMEM((d_model, d_exp), BF16),
                        pltpu.VMEM((d_exp, d_model), BF16),
                        pltpu.SemaphoreType.DMA(())],
    )
    return pl.pallas_call(
        _experts_kernel,
        grid_spec=grid_spec,
        out_shape=jax.ShapeDtypeStruct((n_blocks * rows, d_model), F32),
        compiler_params=pltpu.CompilerParams(dimension_semantics=("arbitrary",), vmem_limit_bytes=V7X_VMEM_LIMIT),
        name="experts",
    )(block_e, n_used, row_tok.reshape(n_blocks, 1, rows), x1, w_gate, w_up, w_down)


def _combine_kernel(dest_ref, wt_ref, x_ref, y_hbm, sg_ref, su_ref, sd_ref, g_ref, b_ref,
                    out_ref, ybuf_ref, sem, *, alpha):
    rows = x_ref.shape[0]

    def row_copy(k, r):
        return pltpu.make_async_copy(y_hbm.at[pl.ds(dest_ref[k, r], 1), :], ybuf_ref.at[k, pl.ds(r, 1), :], sem)

    def issue(r, c):
        for k in range(TOP_K):
            row_copy(k, r).start()
        return c
    lax.fori_loop(0, rows, issue, 0)

    x = x_ref[...]
    xb = x.astype(BF16)
    hidden = _silu(_dot(xb, sg_ref[...])) * _dot(xb, su_ref[...])
    acc = alpha * x + _dot(hidden.astype(BF16), sd_ref[...])

    def drain(r, c):
        for k in range(TOP_K):
            row_copy(k, r).wait()
        return c
    lax.fori_loop(0, rows, drain, 0)

    wt = wt_ref[...]
    for k in range(TOP_K):
        acc = acc + ybuf_ref[k] * wt[:, k:k + 1]
    out_ref[...] = _layernorm_rows(acc, g_ref[...], b_ref[...])


def _combine(x1, y, dest, w_tok, sw_gate, sw_up, sw_down, ln2_g, ln2_b, alpha):
    tokens, d_model = x1.shape
    d_shared = sw_gate.shape[1]
    rows = COMBINE_ROWS
    assert tokens % rows == 0
    row2 = lambda a: a.reshape(1, -1).astype(F32)
    return pl.pallas_call(
        functools.partial(_combine_kernel, alpha=alpha),
        grid=(tokens // rows,),
        in_specs=[pl.BlockSpec((TOP_K, rows), lambda i: (0, i), memory_space=pltpu.SMEM),
                  pl.BlockSpec((rows, TOP_K), lambda i: (i, 0)),
                  pl.BlockSpec((rows, d_model), lambda i: (i, 0)),
                  pl.BlockSpec(memory_space=pl.ANY),
                  _const_spec((d_model, d_shared)),
                  _const_spec((d_model, d_shared)),
                  _const_spec((d_shared, d_model)),
                  _const_spec((1, d_model)),
                  _const_spec((1, d_model))],
        out_specs=pl.BlockSpec((rows, d_model), lambda i: (i, 0)),
        out_shape=jax.ShapeDtypeStruct((tokens, d_model), F32),
        scratch_shapes=[pltpu.VMEM((TOP_K, rows, d_model), F32), pltpu.SemaphoreType.DMA(())],
        compiler_params=pltpu.CompilerParams(dimension_semantics=("arbitrary",), vmem_limit_bytes=V7X_VMEM_LIMIT),
        name="combine",
    )(dest, w_tok, x1, y, sw_gate.astype(BF16), sw_up.astype(BF16), sw_down.astype(BF16), row2(ln2_g), row2(ln2_b))


def _dispatch_plan(idx, rank, counts):
    tokens = idx.shape[1]
    n_exp = counts.shape[0]
    padded = (counts + ROW_BLOCK - 1) // ROW_BLOCK * ROW_BLOCK
    pad_ends = jnp.cumsum(padded)
    pad_starts = pad_ends - padded
    dest = pad_starts[idx] + rank
    n_blocks = -(-(tokens * TOP_K + n_exp * (ROW_BLOCK - 1)) // ROW_BLOCK)
    tok = jnp.broadcast_to(jnp.arange(tokens, dtype=jnp.int32)[None, :], dest.shape)
    row_tok = jnp.zeros((n_blocks * ROW_BLOCK,), jnp.int32).at[dest.reshape(-1)].set(tok.reshape(-1))
    block_start = jnp.arange(n_blocks, dtype=jnp.int32) * ROW_BLOCK
    block_e = jnp.minimum(jnp.searchsorted(pad_ends, block_start, side="right"), n_exp - 1).astype(jnp.int32)
    n_used = (pad_ends[-1:] // ROW_BLOCK).astype(jnp.int32)
    return dest, row_tok, block_e, n_used


def kernel(x, w_in, ret_norm_g, gla_gate_w2, gla_gate_b, gla_norm_g, w_ret_out, w_gla_out, w_o, ln1_g, ln1_b, router_w, router_bias, exp_w_gate, exp_w_up, exp_w_down, shared_w_gate, shared_w_up, shared_w_down, ln2_g, ln2_b):
    batch, seq, d_model = x.shape
    depth = w_in.shape[0]
    alpha = (2.0 * depth) ** 0.25
    for l in range(depth):
        x1 = _mix(x, w_in[l], ret_norm_g[l], gla_gate_w2[l], gla_gate_b[l], gla_norm_g[l],
                  w_ret_out[l], w_gla_out[l], w_o[l], ln1_g[l], ln1_b[l], alpha)
        x1 = x1.reshape(batch * seq, d_model)
        idx, w_sel, rank, counts = _route(x1, router_w[l], router_bias[l])
        dest, row_tok, block_e, n_used = _dispatch_plan(idx, rank, counts[:, 0].astype(jnp.int32))
        y = _experts(x1, row_tok, block_e, n_used, exp_w_gate[l], exp_w_up[l], exp_w_down[l])
        out = _combine(x1, y, dest, w_sel.T, shared_w_gate[l], shared_w_up[l], shared_w_down[l],
                       ln2_g[l], ln2_b[l], alpha)
        x = out.reshape(batch, seq, d_model)
    return x
```

```python
import functools

import jax
import jax.numpy as jnp
import numpy as np
from jax import lax
from jax.experimental import pallas as pl
from jax.experimental.pallas import tpu as pltpu

CHUNK = 64
RET_HEADS = 4
RET_DK = 128
RET_DV = 256
GLA_HEADS = 4
GLA_DK = 128
GLA_DV = 256
GLA_GATE_RANK = 16
GLA_GATE_TAU = 16.0
ROPE_THETA = 10000.0
N_EXPERTS = 256
TOP_K = 8
N_GROUPS = 8
TOPK_GROUPS = 4
ROUTED_SCALE = 2.5
LN_EPS = 1e-5
NORM_EPS = 1e-6

V7X_LANES = 128
V7X_VMEM_LIMIT = 60 * 1024 * 1024

MIX_ROWS = 256
ROUTE_COLS = 512
ROW_BLOCK = 256
DISPATCH_ROWS = 512
COMBINE_ROWS = 256

F32 = jnp.float32
BF16 = jnp.bfloat16


def _dot(a, b):
    return jnp.dot(a, b, preferred_element_type=F32)


def _dot_nt(a, b):
    return lax.dot_general(a, b, (((1,), (1,)), ((), ())), preferred_element_type=F32)


def _dot_tn(a, b):
    return lax.dot_general(a, b, (((0,), (0,)), ((), ())), preferred_element_type=F32)


def _sigmoid(v):
    return 1.0 / (1.0 + jnp.exp(-v))


def _silu(v):
    return v * _sigmoid(v)


def _layernorm_rows(v, g, b):
    mu = jnp.mean(v, axis=-1, keepdims=True)
    vc = v - mu
    var = jnp.mean(vc * vc, axis=-1, keepdims=True)
    return vc * lax.rsqrt(var + LN_EPS) * g + b


def _mix_kernel(x_ref, wrq_ref, wrk_ref, wrv_ref, wrg_ref, wgq_ref, wgk_ref, wgv_ref, wgg_ref, wga_ref, wmg_ref,
                cos_ref, sin_ref, dmask_ref, qdec_ref, kdec_ref, tri_ref, w2_ref, gb_ref, retg_ref, glag_ref,
                wro_ref, wgo_ref, wo_ref, ln1g_ref, ln1b_ref,
                out_ref, rstate_ref, gstate_ref, yret_ref, ygla_ref, *, block_decay, alpha):
    rows = x_ref.shape[0]

    @pl.when(pl.program_id(1) == 0)
    def _():
        rstate_ref[...] = jnp.zeros_like(rstate_ref)
        gstate_ref[...] = jnp.zeros_like(gstate_ref)

    x = x_ref[...]
    xb = x.astype(BF16)

    def proj(w_ref):
        return _dot(xb, w_ref[...])

    cos = cos_ref[...]
    sin = sin_ref[...]

    def rope(t):
        return t * cos + pltpu.roll(t, RET_DK // 2, 1) * sin

    rq = proj(wrq_ref)
    rk = proj(wrk_ref)
    rv = proj(wrv_ref)
    rg = proj(wrg_ref)
    for h in range(RET_HEADS):
        qk = slice(h * RET_DK, (h + 1) * RET_DK)
        vv = slice(h * RET_DV, (h + 1) * RET_DV)
        q = rope(rq[:, qk])
        k = rope(rk[:, qk])
        v = rv[:, vv].astype(BF16)
        scores = _dot_nt(q.astype(BF16), k.astype(BF16)) * dmask_ref[h]
        o = _dot(scores.astype(BF16), v)
        state = rstate_ref[h]
        o = o + _dot((q * qdec_ref[h]).astype(BF16), state.astype(BF16))
        rstate_ref[h] = state * block_decay[h] + _dot_tn((k * kdec_ref[h]).astype(BF16), v)
        mu = jnp.mean(o, axis=-1, keepdims=True)
        oc = o - mu
        var = jnp.mean(oc * oc, axis=-1, keepdims=True)
        y = oc * lax.rsqrt(var + LN_EPS) * retg_ref[:, vv] * _silu(rg[:, vv])
        yret_ref[:, vv] = y.astype(BF16)

    gq = proj(wgq_ref) * (GLA_DK ** -0.5)
    gk = proj(wgk_ref)
    gv = proj(wgv_ref)
    gg = proj(wgg_ref)
    ga = proj(wga_ref)
    z = _dot(ga.astype(BF16), w2_ref[...]) + gb_ref[...]
    log_a = (jnp.minimum(z, 0.0) - jnp.log1p(jnp.exp(-jnp.abs(z)))) * (1.0 / GLA_GATE_TAU)
    la_hi = log_a.astype(BF16)
    la_lo = (log_a - la_hi.astype(F32)).astype(BF16)
    tri = tri_ref[...]
    bcum = _dot(tri, la_hi) + _dot(tri, la_lo)
    for c in range(rows // CHUNK):
        rs = slice(c * CHUNK, (c + 1) * CHUNK)
        b_end = bcum[(c + 1) * CHUNK - 1:(c + 1) * CHUNK, :]
        kt = gk[rs, :] * jnp.exp(b_end - bcum[rs, :])
        dec = jnp.exp(b_end)
        for h in range(GLA_HEADS):
            qk = slice(h * GLA_DK, (h + 1) * GLA_DK)
            vv = slice(h * GLA_DV, (h + 1) * GLA_DV)
            state_t = gstate_ref[h] * dec[:, qk] + _dot_tn(gv[rs, vv].astype(BF16), kt[:, qk].astype(BF16))
            gstate_ref[h] = state_t
            o = _dot_nt(gq[rs, qk].astype(BF16), state_t.astype(BF16))
            ms = jnp.mean(o * o, axis=-1, keepdims=True)
            y = o * lax.rsqrt(ms + NORM_EPS) * glag_ref[:, vv] * _silu(gg[rs, vv])
            ygla_ref[rs, vv] = y.astype(BF16)

    d_model = x.shape[1]
    u_ret = _dot(yret_ref[...], wro_ref[...])
    u_gla = _dot(ygla_ref[...], wgo_ref[...])
    gate = _sigmoid(proj(wmg_ref))
    merged = gate[:, :d_model] * u_ret + gate[:, d_model:] * u_gla
    mix = _dot(merged.astype(BF16), wo_ref[...])
    out_ref[...] = _layernorm_rows(alpha * x + mix, ln1g_ref[...], ln1b_ref[...])


def _mix_tables(seq, rows):
    half = RET_DK // 2
    inv = ROPE_THETA ** (-np.arange(half, dtype=np.float64) / half)
    ang = np.arange(seq, dtype=np.float64)[:, None] * inv[None, :]
    cos2 = np.concatenate([np.cos(ang), np.cos(ang)], axis=1)
    sin2 = np.concatenate([-np.sin(ang), np.sin(ang)], axis=1)
    log_g = np.log1p(-np.exp2(-5.0 - np.arange(RET_HEADS, dtype=np.float64)))
    j = np.arange(rows, dtype=np.float64)
    same_or_earlier_chunk = (j[None, :] // CHUNK) <= (j[:, None] // CHUNK)
    k_scale = RET_DK ** -0.5
    dmask = np.exp(log_g[:, None, None] * np.abs(j[:, None] - j[None, :])) * same_or_earlier_chunk[None] * k_scale
    qdec = np.exp(log_g[:, None] * (j[None, :] + 1.0))
    kdec = np.exp(log_g[:, None] * (rows - 1.0 - j[None, :])) * k_scale
    qdec = np.broadcast_to(qdec[:, :, None], (RET_HEADS, rows, RET_DK))
    kdec = np.broadcast_to(kdec[:, :, None], (RET_HEADS, rows, RET_DK))
    block_decay = tuple(float(v) for v in np.exp(log_g * rows))
    tri = ((j[None, :] <= j[:, None]) & ((j[None, :] // CHUNK) == (j[:, None] // CHUNK)))
    to = lambda a, dt: jnp.asarray(np.ascontiguousarray(a), dtype=dt)
    return (to(cos2, F32), to(sin2, F32), to(dmask, F32), to(qdec, F32), to(kdec, F32), to(tri, BF16)), block_decay


def _const_spec(shape):
    nd = len(shape)
    return pl.BlockSpec(shape, lambda *_: (0,) * nd, pipeline_mode=pl.Buffered(1))


def _mix(x, w_in, ret_norm_g, gla_gate_w2, gla_gate_b, gla_norm_g, w_ret_out, w_gla_out, w_o, ln1_g, ln1_b, alpha):
    batch, seq, d_model = x.shape
    rows = MIX_ROWS
    assert seq % rows == 0 and rows % CHUNK == 0
    ret_qk, ret_v = RET_HEADS * RET_DK, RET_HEADS * RET_DV
    gla_qk, gla_v = GLA_HEADS * GLA_DK, GLA_HEADS * GLA_DV
    splits = (ret_qk, ret_qk, ret_v, ret_v, gla_qk, gla_qk, gla_v, gla_v, GLA_GATE_RANK, 2 * d_model)
    assert w_in.shape == (d_model, sum(splits))
    offs = np.cumsum((0,) + splits)
    parts = [w_in[:, offs[i]:offs[i + 1]].astype(BF16) for i in range(len(splits))]
    parts[8] = jnp.pad(parts[8], ((0, 0), (0, V7X_LANES - GLA_GATE_RANK)))
    w2 = jnp.pad(gla_gate_w2.astype(BF16), ((0, V7X_LANES - GLA_GATE_RANK), (0, 0)))
    (cos2, sin2, dmask, qdec, kdec, tri), block_decay = _mix_tables(seq, rows)
    row2 = lambda a: a.reshape(1, -1).astype(F32)
    consts = [dmask, qdec, kdec, tri, w2, row2(gla_gate_b), row2(ret_norm_g), row2(gla_norm_g),
              w_ret_out.astype(BF16), w_gla_out.astype(BF16), w_o.astype(BF16), row2(ln1_g), row2(ln1_b)]
    pos_spec = pl.BlockSpec((rows, RET_DK), lambda b, s: (s, 0))
    in_specs = ([pl.BlockSpec((None, rows, d_model), lambda b, s: (b, s, 0))]
                + [_const_spec(p.shape) for p in parts]
                + [pos_spec, pos_spec]
                + [_const_spec(c.shape) for c in consts])
    return pl.pallas_call(
        functools.partial(_mix_kernel, block_decay=block_decay, alpha=alpha),
        grid=(batch, seq // rows),
        in_specs=in_specs,
        out_specs=pl.BlockSpec((None, rows, d_model), lambda b, s: (b, s, 0)),
        out_shape=jax.ShapeDtypeStruct((batch, seq, d_model), F32),
        scratch_shapes=[pltpu.VMEM((RET_HEADS, RET_DK, RET_DV), F32),
                        pltpu.VMEM((GLA_HEADS, GLA_DV, GLA_DK), F32),
                        pltpu.VMEM((rows, ret_v), BF16),
                        pltpu.VMEM((rows, gla_v), BF16)],
        compiler_params=pltpu.CompilerParams(dimension_semantics=("arbitrary", "arbitrary"),
                                             vmem_limit_bytes=V7X_VMEM_LIMIT),
        name="mix",
    )(x, *parts, cos2, sin2, *consts)


def _route_kernel(x_ref, rwt_ref, bias_ref, triu_ref, ones_ref,
                  idx_ref, w_ref, rank_ref, counts_ref, carry_ref):
    cols = x_ref.shape[0]
    n_exp = rwt_ref.shape[0]
    per_group = n_exp // N_GROUPS
    neg_inf = -jnp.inf

    @pl.when(pl.program_id(0) == 0)
    def _():
        carry_ref[...] = jnp.zeros_like(carry_ref)

    logits = _dot_nt(rwt_ref[...], x_ref[...].astype(BF16))
    scores = _sigmoid(logits)
    biased = scores + bias_ref[...]

    sub = lax.broadcasted_iota(jnp.int32, (per_group, cols), 0)
    gscore = []
    for g in range(N_GROUPS):
        blk = biased[g * per_group:(g + 1) * per_group, :]
        m1 = jnp.max(blk, axis=0, keepdims=True)
        i1 = jnp.min(jnp.where(blk == m1, sub, per_group), axis=0, keepdims=True)
        m2 = jnp.max(jnp.where(sub == i1, neg_inf, blk), axis=0, keepdims=True)
        gscore.append(m1 + m2)
    masked = []
    for g in range(N_GROUPS):
        ahead = jnp.zeros((1, cols), jnp.int32)
        for o in range(N_GROUPS):
            if o == g:
                continue
            before = (gscore[o] >= gscore[g]) if o < g else (gscore[o] > gscore[g])
            ahead = ahead + before.astype(jnp.int32)
        keep = ahead < TOPK_GROUPS
        blk = biased[g * per_group:(g + 1) * per_group, :]
        masked.append(jnp.where(keep, blk, neg_inf))
    candidates = jnp.concatenate(masked, axis=0)
    cur = candidates

    rowid = lax.broadcasted_iota(jnp.int32, (n_exp, cols), 0)
    picked = []
    weights = []
    for _ in range(TOP_K):
        m = jnp.max(cur, axis=0, keepdims=True)
        ik = jnp.min(jnp.where(cur == m, rowid, n_exp), axis=0, keepdims=True)
        sel = rowid == ik
        weights.append(jnp.sum(jnp.where(sel, scores, 0.0), axis=0, keepdims=True))
        cur = jnp.where(sel, neg_inf, cur)
        picked.append(ik)
    wsum = weights[0]
    for wk in weights[1:]:
        wsum = wsum + wk

    chosen = (cur == neg_inf) & (candidates != neg_inf)
    chosen_b = jnp.where(chosen, 1.0, 0.0).astype(BF16)
    carry = carry_ref[...]
    before = _dot(chosen_b, triu_ref[...]) + jnp.concatenate([carry] * (cols // V7X_LANES), axis=1)
    for k in range(TOP_K):
        sel = rowid == picked[k]
        rank_ref[k:k + 1, :] = jnp.sum(jnp.where(sel, before, 0.0), axis=0, keepdims=True).astype(jnp.int32)
        idx_ref[k:k + 1, :] = picked[k]
        w_ref[k:k + 1, :] = weights[k] / wsum * ROUTED_SCALE
    carry = carry + _dot(chosen_b, ones_ref[...])
    carry_ref[...] = carry
    counts_ref[...] = carry


def _route(x1, router_w, router_bias):
    tokens, d_model = x1.shape
    n_exp = router_w.shape[1]
    cols = ROUTE_COLS
    assert tokens % cols == 0 and n_exp % N_GROUPS == 0
    j = np.arange(cols)
    triu = jnp.asarray((j[:, None] < j[None, :]), dtype=BF16)
    ones = jnp.ones((cols, V7X_LANES), BF16)
    out_row = lambda dt: jax.ShapeDtypeStruct((TOP_K, tokens), dt)
    row_spec = pl.BlockSpec((TOP_K, cols), lambda i: (0, i))
    return pl.pallas_call(
        _route_kernel,
        grid=(tokens // cols,),
        in_specs=[pl.BlockSpec((cols, d_model), lambda i: (i, 0)),
                  _const_spec((n_exp, d_model)),
                  _const_spec((n_exp, 1)),
                  _const_spec((cols, cols)),
                  _const_spec((cols, V7X_LANES))],
        out_specs=[row_spec, row_spec, row_spec, pl.BlockSpec((n_exp, V7X_LANES), lambda i: (0, 0))],
        out_shape=[out_row(jnp.int32), out_row(F32), out_row(jnp.int32),
                   jax.ShapeDtypeStruct((n_exp, V7X_LANES), F32)],
        scratch_shapes=[pltpu.VMEM((n_exp, V7X_LANES), F32)],
        compiler_params=pltpu.CompilerParams(dimension_semantics=("arbitrary",), vmem_limit_bytes=V7X_VMEM_LIMIT),
        name="route",
    )(x1, router_w.T.astype(BF16), router_bias.reshape(n_exp, 1).astype(F32), triu, ones)


def _plan_kernel(idx_ref, rank_ref, pstart_ref, dest_ref):
    n_exp = pstart_ref.shape[0]
    cols = idx_ref.shape[1]
    rowid = lax.broadcasted_iota(jnp.int32, (n_exp, cols), 0)
    pstart = pstart_ref[...]
    for k in range(TOP_K):
        base = jnp.sum(jnp.where(rowid == idx_ref[k:k + 1, :], pstart, 0.0), axis=0, keepdims=True)
        dest_ref[k:k + 1, :] = base.astype(jnp.int32) + rank_ref[k:k + 1, :]


def _plan(idx, rank, pad_starts):
    tokens = idx.shape[1]
    n_exp = pad_starts.shape[0]
    cols = ROUTE_COLS
    row_spec = pl.BlockSpec((TOP_K, cols), lambda i: (0, i))
    return pl.pallas_call(
        _plan_kernel,
        grid=(tokens // cols,),
        in_specs=[row_spec, row_spec, _const_spec((n_exp, 1))],
        out_specs=row_spec,
        out_shape=jax.ShapeDtypeStruct((TOP_K, tokens), jnp.int32),
        compiler_params=pltpu.CompilerParams(dimension_semantics=("arbitrary",)),
        name="plan",
    )(idx, rank, pad_starts.reshape(n_exp, 1).astype(F32))


def _dispatch_kernel(pad_end_ref, padded_ref, dest_ref, x_ref, xs_hbm, zero_ref, sem):
    rows = x_ref.shape[0]
    n_exp = pad_end_ref.shape[0]

    def zero_copy(e):
        start = pl.multiple_of(pad_end_ref[e] - ROW_BLOCK, ROW_BLOCK)
        return pltpu.make_async_copy(zero_ref, xs_hbm.at[pl.ds(start, ROW_BLOCK), :], sem)

    @pl.when(pl.program_id(0) == 0)
    def _():
        zero_ref[...] = jnp.zeros_like(zero_ref)

        def issue(e, c):
            @pl.when(padded_ref[e] > 0)
            def _():
                zero_copy(e).start()
            return c
        lax.fori_loop(0, n_exp, issue, 0)

        def drain(e, c):
            @pl.when(padded_ref[e] > 0)
            def _():
                zero_copy(e).wait()
            return c
        lax.fori_loop(0, n_exp, drain, 0)

    def row_copy(k, r):
        return pltpu.make_async_copy(x_ref.at[pl.ds(r, 1), :], xs_hbm.at[pl.ds(dest_ref[k, r], 1), :], sem)

    def issue(r, c):
        for k in range(TOP_K):
            row_copy(k, r).start()
        return c
    lax.fori_loop(0, rows, issue, 0)

    def drain(r, c):
        for k in range(TOP_K):
            row_copy(k, r).wait()
        return c
    lax.fori_loop(0, rows, drain, 0)


def _dispatch(x1, dest, pad_ends, padded, n_rows):
    tokens, d_model = x1.shape
    rows = DISPATCH_ROWS
    grid_spec = pltpu.PrefetchScalarGridSpec(
        num_scalar_prefetch=2,
        grid=(tokens // rows,),
        in_specs=[pl.BlockSpec((TOP_K, rows), lambda i, pe, pd: (0, i), memory_space=pltpu.SMEM),
                  pl.BlockSpec((rows, d_model), lambda i, pe, pd: (i, 0))],
        out_specs=pl.BlockSpec(memory_space=pl.ANY),
        scratch_shapes=[pltpu.VMEM((ROW_BLOCK, d_model), F32), pltpu.SemaphoreType.DMA(())],
    )
    return pl.pallas_call(
        _dispatch_kernel,
        grid_spec=grid_spec,
        out_shape=jax.ShapeDtypeStruct((n_rows, d_model), F32),
        compiler_params=pltpu.CompilerParams(dimension_semantics=("arbitrary",), has_side_effects=True),
        name="dispatch",
    )(pad_ends, padded, dest, x1)


def _experts_kernel(block_e_ref, n_used_ref, xs_ref, wg_ref, wu_ref, wd_ref, y_ref, wg_b, wu_b, wd_b):
    i = pl.program_id(0)

    @pl.when(i < n_used_ref[0])
    def _():
        @pl.when((i == 0) | (block_e_ref[i] != block_e_ref[jnp.maximum(i - 1, 0)]))
        def _():
            wg_b[...] = wg_ref[...].astype(BF16)
            wu_b[...] = wu_ref[...].astype(BF16)
            wd_b[...] = wd_ref[...].astype(BF16)

        xb = xs_ref[...].astype(BF16)
        hidden = _silu(_dot(xb, wg_b[...])) * _dot(xb, wu_b[...])
        y_ref[...] = _dot(hidden.astype(BF16), wd_b[...])

    @pl.when(i >= n_used_ref[0])
    def _():
        y_ref[...] = jnp.zeros_like(y_ref)


def _experts(xs, block_e, n_used, w_gate, w_up, w_down):
    n_rows, d_model = xs.shape
    n_exp, _, d_exp = w_gate.shape
    n_blocks = block_e.shape[0]
    rows = ROW_BLOCK
    assert n_rows == n_blocks * rows
    used_row = lambda i, be, nu: (jnp.minimum(i, jnp.maximum(nu[0] - 1, 0)), 0)
    grid_spec = pltpu.PrefetchScalarGridSpec(
        num_scalar_prefetch=2,
        grid=(n_blocks,),
        in_specs=[pl.BlockSpec((rows, d_model), used_row),
                  pl.BlockSpec((None, d_model, d_exp), lambda i, be, nu: (be[i], 0, 0)),
                  pl.BlockSpec((None, d_model, d_exp), lambda i, be, nu: (be[i], 0, 0)),
                  pl.BlockSpec((None, d_exp, d_model), lambda i, be, nu: (be[i], 0, 0))],
        out_specs=pl.BlockSpec((rows, d_model), lambda i, be, nu: (i, 0)),
        scratch_shapes=[pltpu.VMEM((d_model, d_exp), BF16),
                        pltpu.VMEM((d_model, d_exp), BF16),
                        pltpu.VMEM((d_exp, d_model), BF16)],
    )
    return pl.pallas_call(
        _experts_kernel,
        grid_spec=grid_spec,
        out_shape=jax.ShapeDtypeStruct((n_rows, d_model), F32),
        compiler_params=pltpu.CompilerParams(dimension_semantics=("arbitrary",), vmem_limit_bytes=V7X_VMEM_LIMIT),
        name="experts",
    )(block_e, n_used, xs, w_gate, w_up, w_down)


def _combine_kernel(dest_ref, wt_ref, x_ref, y_hbm, sg_ref, su_ref, sd_ref, g_ref, b_ref,
                    out_ref, ybuf_ref, sem, *, alpha):
    rows = x_ref.shape[0]

    def row_copy(k, r):
        return pltpu.make_async_copy(y_hbm.at[pl.ds(dest_ref[k, r], 1), :], ybuf_ref.at[k, pl.ds(r, 1), :], sem)

    def issue(r, c):
        for k in range(TOP_K):
            row_copy(k, r).start()
        return c
    lax.fori_loop(0, rows, issue, 0)

    x = x_ref[...]
    xb = x.astype(BF16)
    hidden = _silu(_dot(xb, sg_ref[...])) * _dot(xb, su_ref[...])
    acc = alpha * x + _dot(hidden.astype(BF16), sd_ref[...])

    def drain(r, c):
        for k in range(TOP_K):
            row_copy(k, r).wait()
        return c
    lax.fori_loop(0, rows, drain, 0)

    wt = wt_ref[...]
    for k in range(TOP_K):
        acc = acc + ybuf_ref[k] * wt[:, k:k + 1]
    out_ref[...] = _layernorm_rows(acc, g_ref[...], b_ref[...])


def _combine(x1, y, dest, w_tok, sw_gate, sw_up, sw_down, ln2_g, ln2_b, alpha):
    tokens, d_model = x1.shape
    d_shared = sw_gate.shape[1]
    rows = COMBINE_ROWS
    assert tokens % rows == 0
    row2 = lambda a: a.reshape(1, -1).astype(F32)
    return pl.pallas_call(
        functools.partial(_combine_kernel, alpha=alpha),
        grid=(tokens // rows,),
        in_specs=[pl.BlockSpec((TOP_K, rows), lambda i: (0, i), memory_space=pltpu.SMEM),
                  pl.BlockSpec((rows, TOP_K), lambda i: (i, 0)),
                  pl.BlockSpec((rows, d_model), lambda i: (i, 0)),
                  pl.BlockSpec(memory_space=pl.ANY),
                  _const_spec((d_model, d_shared)),
                  _const_spec((d_model, d_shared)),
                  _const_spec((d_shared, d_model)),
                  _const_spec((1, d_model)),
                  _const_spec((1, d_model))],
        out_specs=pl.BlockSpec((rows, d_model), lambda i: (i, 0)),
        out_shape=jax.ShapeDtypeStruct((tokens, d_model), F32),
        scratch_shapes=[pltpu.VMEM((TOP_K, rows, d_model), F32), pltpu.SemaphoreType.DMA(())],
        compiler_params=pltpu.CompilerParams(dimension_semantics=("arbitrary",), vmem_limit_bytes=V7X_VMEM_LIMIT),
        name="combine",
    )(dest, w_tok, x1, y, sw_gate.astype(BF16), sw_up.astype(BF16), sw_down.astype(BF16), row2(ln2_g), row2(ln2_b))


def _block_layout(counts, tokens):
    n_exp = counts.shape[0]
    padded = (counts + ROW_BLOCK - 1) // ROW_BLOCK * ROW_BLOCK
    pad_ends = jnp.cumsum(padded)
    pad_starts = pad_ends - padded
    n_blocks = -(-(tokens * TOP_K + n_exp * (ROW_BLOCK - 1)) // ROW_BLOCK)
    block_start = jnp.arange(n_blocks, dtype=jnp.int32) * ROW_BLOCK
    block_e = jnp.sum((block_start[:, None] >= pad_ends[None, :]).astype(jnp.int32), axis=1)
    block_e = jnp.minimum(block_e, n_exp - 1)
    n_used = (pad_ends[-1:] // ROW_BLOCK).astype(jnp.int32)
    return padded, pad_ends, pad_starts, block_e, n_used, n_blocks * ROW_BLOCK


def kernel(x, w_in, ret_norm_g, gla_gate_w2, gla_gate_b, gla_norm_g, w_ret_out, w_gla_out, w_o, ln1_g, ln1_b, router_w, router_bias, exp_w_gate, exp_w_up, exp_w_down, shared_w_gate, shared_w_up, shared_w_down, ln2_g, ln2_b):
    batch, seq, d_model = x.shape
    depth = w_in.shape[0]
    alpha = (2.0 * depth) ** 0.25
    for l in range(depth):
        x1 = _mix(x, w_in[l], ret_norm_g[l], gla_gate_w2[l], gla_gate_b[l], gla_norm_g[l],
                  w_ret_out[l], w_gla_out[l], w_o[l], ln1_g[l], ln1_b[l], alpha)
        x1 = x1.reshape(batch * seq, d_model)
        idx, w_sel, rank, counts = _route(x1, router_w[l], router_bias[l])
        padded, pad_ends, pad_starts, block_e, n_used, n_rows = _block_layout(
            counts[:, 0].astype(jnp.int32), batch * seq)
        dest = _plan(idx, rank, pad_starts)
        xs = _dispatch(x1, dest, pad_ends, padded, n_rows)
        y = _experts(xs, block_e, n_used, exp_w_gate[l], exp_w_up[l], exp_w_down[l])
        out = _combine(x1, y, dest, w_sel.T, shared_w_gate[l], shared_w_up[l], shared_w_down[l],
                       ln2_g[l], ln2_b[l], alpha)
        x = out.reshape(batch, seq, d_model)
    return x
```

```python
import functools

import jax
import jax.numpy as jnp
import numpy as np
from jax import lax
from jax.experimental import pallas as pl
from jax.experimental.pallas import tpu as pltpu

CHUNK = 64
RET_HEADS = 4
RET_DK = 128
RET_DV = 256
GLA_HEADS = 4
GLA_DK = 128
GLA_DV = 256
GLA_GATE_RANK = 16
GLA_GATE_TAU = 16.0
ROPE_THETA = 10000.0
N_EXPERTS = 256
TOP_K = 8
N_GROUPS = 8
TOPK_GROUPS = 4
ROUTED_SCALE = 2.5
LN_EPS = 1e-5
NORM_EPS = 1e-6

V7X_LANES = 128
V7X_VMEM_LIMIT = 60 * 1024 * 1024

MIX_ROWS = 256
ROUTE_COLS = 512
ROW_BLOCK = 256
DISPATCH_ROWS = 512
DMA_UNROLL = 2
COMBINE_ROWS = 256

F32 = jnp.float32
BF16 = jnp.bfloat16


def _dot(a, b):
    return jnp.dot(a, b, preferred_element_type=F32)


def _dot_nt(a, b):
    return lax.dot_general(a, b, (((1,), (1,)), ((), ())), preferred_element_type=F32)


def _dot_tn(a, b):
    return lax.dot_general(a, b, (((0,), (0,)), ((), ())), preferred_element_type=F32)


def _sigmoid(v):
    return 1.0 / (1.0 + jnp.exp(-v))


def _silu(v):
    return v * _sigmoid(v)


def _layernorm_rows(v, g, b):
    mu = jnp.mean(v, axis=-1, keepdims=True)
    vc = v - mu
    var = jnp.mean(vc * vc, axis=-1, keepdims=True)
    return vc * lax.rsqrt(var + LN_EPS) * g + b


def _mix_kernel(x_ref, wrq_ref, wrk_ref, wrv_ref, wrg_ref, wgq_ref, wgk_ref, wgv_ref, wgg_ref, wga_ref, wmg_ref,
                cos_ref, sin_ref, dmask_ref, qdec_ref, kdec_ref, tri_ref, w2_ref, gb_ref, retg_ref, glag_ref,
                wro_ref, wgo_ref, wo_ref, ln1g_ref, ln1b_ref,
                out_ref, rstate_ref, gstate_ref, yret_ref, ygla_ref, *, block_decay, alpha):
    rows = x_ref.shape[0]

    @pl.when(pl.program_id(1) == 0)
    def _():
        rstate_ref[...] = jnp.zeros_like(rstate_ref)
        gstate_ref[...] = jnp.zeros_like(gstate_ref)

    x = x_ref[...]
    xb = x.astype(BF16)

    def proj(w_ref):
        return _dot(xb, w_ref[...])

    cos = cos_ref[...]
    sin = sin_ref[...]

    def rope(t):
        return t * cos + pltpu.roll(t, RET_DK // 2, 1) * sin

    rq = proj(wrq_ref)
    rk = proj(wrk_ref)
    rv = proj(wrv_ref)
    rg = proj(wrg_ref)
    for h in range(RET_HEADS):
        qk = slice(h * RET_DK, (h + 1) * RET_DK)
        vv = slice(h * RET_DV, (h + 1) * RET_DV)
        q = rope(rq[:, qk])
        k = rope(rk[:, qk])
        v = rv[:, vv].astype(BF16)
        scores = _dot_nt(q.astype(BF16), k.astype(BF16)) * dmask_ref[h]
        o = _dot(scores.astype(BF16), v)
        state = rstate_ref[h]
        o = o + _dot((q * qdec_ref[h]).astype(BF16), state.astype(BF16))
        rstate_ref[h] = state * block_decay[h] + _dot_tn((k * kdec_ref[h]).astype(BF16), v)
        mu = jnp.mean(o, axis=-1, keepdims=True)
        oc = o - mu
        var = jnp.mean(oc * oc, axis=-1, keepdims=True)
        y = oc * lax.rsqrt(var + LN_EPS) * retg_ref[:, vv] * _silu(rg[:, vv])
        yret_ref[:, vv] = y.astype(BF16)

    gq = proj(wgq_ref) * (GLA_DK ** -0.5)
    gk = proj(wgk_ref)
    gv = proj(wgv_ref)
    gg = proj(wgg_ref)
    ga = proj(wga_ref)
    z = _dot(ga.astype(BF16), w2_ref[...]) + gb_ref[...]
    log_a = (jnp.minimum(z, 0.0) - jnp.log1p(jnp.exp(-jnp.abs(z)))) * (1.0 / GLA_GATE_TAU)
    la_hi = log_a.astype(BF16)
    la_lo = (log_a - la_hi.astype(F32)).astype(BF16)
    tri = tri_ref[...]
    bcum = _dot(tri, la_hi) + _dot(tri, la_lo)
    for c in range(rows // CHUNK):
        rs = slice(c * CHUNK, (c + 1) * CHUNK)
        b_end = bcum[(c + 1) * CHUNK - 1:(c + 1) * CHUNK, :]
        kt = gk[rs, :] * jnp.exp(b_end - bcum[rs, :])
        dec = jnp.exp(b_end)
        for h in range(GLA_HEADS):
            qk = slice(h * GLA_DK, (h + 1) * GLA_DK)
            vv = slice(h * GLA_DV, (h + 1) * GLA_DV)
            state_t = gstate_ref[h] * dec[:, qk] + _dot_tn(gv[rs, vv].astype(BF16), kt[:, qk].astype(BF16))
            gstate_ref[h] = state_t
            o = _dot_nt(gq[rs, qk].astype(BF16), state_t.astype(BF16))
            ms = jnp.mean(o * o, axis=-1, keepdims=True)
            y = o * lax.rsqrt(ms + NORM_EPS) * glag_ref[:, vv] * _silu(gg[rs, vv])
            ygla_ref[rs, vv] = y.astype(BF16)

    d_model = x.shape[1]
    u_ret = _dot(yret_ref[...], wro_ref[...])
    u_gla = _dot(ygla_ref[...], wgo_ref[...])
    gate = _sigmoid(proj(wmg_ref))
    merged = gate[:, :d_model] * u_ret + gate[:, d_model:] * u_gla
    mix = _dot(merged.astype(BF16), wo_ref[...])
    out_ref[...] = _layernorm_rows(alpha * x + mix, ln1g_ref[...], ln1b_ref[...])


def _mix_tables(seq, rows):
    half = RET_DK // 2
    inv = ROPE_THETA ** (-np.arange(half, dtype=np.float64) / half)
    ang = np.arange(seq, dtype=np.float64)[:, None] * inv[None, :]
    cos2 = np.concatenate([np.cos(ang), np.cos(ang)], axis=1)
    sin2 = np.concatenate([-np.sin(ang), np.sin(ang)], axis=1)
    log_g = np.log1p(-np.exp2(-5.0 - np.arange(RET_HEADS, dtype=np.float64)))
    j = np.arange(rows, dtype=np.float64)
    same_or_earlier_chunk = (j[None, :] // CHUNK) <= (j[:, None] // CHUNK)
    k_scale = RET_DK ** -0.5
    dmask = np.exp(log_g[:, None, None] * np.abs(j[:, None] - j[None, :])) * same_or_earlier_chunk[None] * k_scale
    qdec = np.exp(log_g[:, None] * (j[None, :] + 1.0))
    kdec = np.exp(log_g[:, None] * (rows - 1.0 - j[None, :])) * k_scale
    qdec = np.broadcast_to(qdec[:, :, None], (RET_HEADS, rows, RET_DK))
    kdec = np.broadcast_to(kdec[:, :, None], (RET_HEADS, rows, RET_DK))
    block_decay = tuple(float(v) for v in np.exp(log_g * rows))
    tri = ((j[None, :] <= j[:, None]) & ((j[None, :] // CHUNK) == (j[:, None] // CHUNK)))
    to = lambda a, dt: jnp.asarray(np.ascontiguousarray(a), dtype=dt)
    return (to(cos2, F32), to(sin2, F32), to(dmask, F32), to(qdec, F32), to(kdec, F32), to(tri, BF16)), block_decay


def _const_spec(shape):
    nd = len(shape)
    return pl.BlockSpec(shape, lambda *_: (0,) * nd, pipeline_mode=pl.Buffered(1))


def _mix(x, w_in, ret_norm_g, gla_gate_w2, gla_gate_b, gla_norm_g, w_ret_out, w_gla_out, w_o, ln1_g, ln1_b, alpha):
    batch, seq, d_model = x.shape
    rows = MIX_ROWS
    assert seq % rows == 0 and rows % CHUNK == 0
    ret_qk, ret_v = RET_HEADS * RET_DK, RET_HEADS * RET_DV
    gla_qk, gla_v = GLA_HEADS * GLA_DK, GLA_HEADS * GLA_DV
    splits = (ret_qk, ret_qk, ret_v, ret_v, gla_qk, gla_qk, gla_v, gla_v, GLA_GATE_RANK, 2 * d_model)
    assert w_in.shape == (d_model, sum(splits))
    offs = np.cumsum((0,) + splits)
    parts = [w_in[:, offs[i]:offs[i + 1]].astype(BF16) for i in range(len(splits))]
    parts[8] = jnp.pad(parts[8], ((0, 0), (0, V7X_LANES - GLA_GATE_RANK)))
    w2 = jnp.pad(gla_gate_w2.astype(BF16), ((0, V7X_LANES - GLA_GATE_RANK), (0, 0)))
    (cos2, sin2, dmask, qdec, kdec, tri), block_decay = _mix_tables(seq, rows)
    row2 = lambda a: a.reshape(1, -1).astype(F32)
    consts = [dmask, qdec, kdec, tri, w2, row2(gla_gate_b), row2(ret_norm_g), row2(gla_norm_g),
              w_ret_out.astype(BF16), w_gla_out.astype(BF16), w_o.astype(BF16), row2(ln1_g), row2(ln1_b)]
    pos_spec = pl.BlockSpec((rows, RET_DK), lambda b, s: (s, 0))
    in_specs = ([pl.BlockSpec((None, rows, d_model), lambda b, s: (b, s, 0))]
                + [_const_spec(p.shape) for p in parts]
                + [pos_spec, pos_spec]
                + [_const_spec(c.shape) for c in consts])
    return pl.pallas_call(
        functools.partial(_mix_kernel, block_decay=block_decay, alpha=alpha),
        grid=(batch, seq // rows),
        in_specs=in_specs,
        out_specs=pl.BlockSpec((None, rows, d_model), lambda b, s: (b, s, 0)),
        out_shape=jax.ShapeDtypeStruct((batch, seq, d_model), F32),
        scratch_shapes=[pltpu.VMEM((RET_HEADS, RET_DK, RET_DV), F32),
                        pltpu.VMEM((GLA_HEADS, GLA_DV, GLA_DK), F32),
                        pltpu.VMEM((rows, ret_v), BF16),
                        pltpu.VMEM((rows, gla_v), BF16)],
        compiler_params=pltpu.CompilerParams(dimension_semantics=("arbitrary", "arbitrary"),
                                             vmem_limit_bytes=V7X_VMEM_LIMIT),
        name="mix",
    )(x, *parts, cos2, sin2, *consts)


def _route_kernel(x_ref, rwt_ref, bias_ref, triu_ref, ones_ref,
                  idx_ref, w_ref, rank_ref, counts_ref, carry_ref):
    cols = x_ref.shape[0]
    n_exp = rwt_ref.shape[0]
    per_group = n_exp // N_GROUPS
    neg_inf = -jnp.inf

    @pl.when(pl.program_id(0) == 0)
    def _():
        carry_ref[...] = jnp.zeros_like(carry_ref)

    logits = _dot_nt(rwt_ref[...], x_ref[...].astype(BF16))
    scores = _sigmoid(logits)
    biased = scores + bias_ref[...]

    sub = lax.broadcasted_iota(jnp.int32, (per_group, cols), 0)
    gscore = []
    for g in range(N_GROUPS):
        blk = biased[g * per_group:(g + 1) * per_group, :]
        m1 = jnp.max(blk, axis=0, keepdims=True)
        i1 = jnp.min(jnp.where(blk == m1, sub, per_group), axis=0, keepdims=True)
        m2 = jnp.max(jnp.where(sub == i1, neg_inf, blk), axis=0, keepdims=True)
        gscore.append(m1 + m2)
    masked = []
    for g in range(N_GROUPS):
        ahead = jnp.zeros((1, cols), jnp.int32)
        for o in range(N_GROUPS):
            if o == g:
                continue
            before = (gscore[o] >= gscore[g]) if o < g else (gscore[o] > gscore[g])
            ahead = ahead + before.astype(jnp.int32)
        keep = ahead < TOPK_GROUPS
        blk = biased[g * per_group:(g + 1) * per_group, :]
        masked.append(jnp.where(keep, blk, neg_inf))
    candidates = jnp.concatenate(masked, axis=0)
    cur = candidates

    rowid = lax.broadcasted_iota(jnp.int32, (n_exp, cols), 0)
    picked = []
    weights = []
    for _ in range(TOP_K):
        m = jnp.max(cur, axis=0, keepdims=True)
        ik = jnp.min(jnp.where(cur == m, rowid, n_exp), axis=0, keepdims=True)
        sel = rowid == ik
        weights.append(jnp.sum(jnp.where(sel, scores, 0.0), axis=0, keepdims=True))
        cur = jnp.where(sel, neg_inf, cur)
        picked.append(ik)
    wsum = weights[0]
    for wk in weights[1:]:
        wsum = wsum + wk

    chosen = (cur == neg_inf) & (candidates != neg_inf)
    chosen_b = jnp.where(chosen, 1.0, 0.0).astype(BF16)
    carry = carry_ref[...]
    before = _dot(chosen_b, triu_ref[...]) + jnp.concatenate([carry] * (cols // V7X_LANES), axis=1)
    for k in range(TOP_K):
        sel = rowid == picked[k]
        rank_ref[k:k + 1, :] = jnp.sum(jnp.where(sel, before, 0.0), axis=0, keepdims=True).astype(jnp.int32)
        idx_ref[k:k + 1, :] = picked[k]
        w_ref[k:k + 1, :] = weights[k] / wsum * ROUTED_SCALE
    carry = carry + _dot(chosen_b, ones_ref[...])
    carry_ref[...] = carry
    counts_ref[...] = carry


def _route(x1, router_w, router_bias):
    tokens, d_model = x1.shape
    n_exp = router_w.shape[1]
    cols = ROUTE_COLS
    assert tokens % cols == 0 and n_exp % N_GROUPS == 0
    j = np.arange(cols)
    triu = jnp.asarray((j[:, None] < j[None, :]), dtype=BF16)
    ones = jnp.ones((cols, V7X_LANES), BF16)
    out_row = lambda dt: jax.ShapeDtypeStruct((TOP_K, tokens), dt)
    row_spec = pl.BlockSpec((TOP_K, cols), lambda i: (0, i))
    return pl.pallas_call(
        _route_kernel,
        grid=(tokens // cols,),
        in_specs=[pl.BlockSpec((cols, d_model), lambda i: (i, 0)),
                  _const_spec((n_exp, d_model)),
                  _const_spec((n_exp, 1)),
                  _const_spec((cols, cols)),
                  _const_spec((cols, V7X_LANES))],
        out_specs=[row_spec, row_spec, row_spec, pl.BlockSpec((n_exp, V7X_LANES), lambda i: (0, 0))],
        out_shape=[out_row(jnp.int32), out_row(F32), out_row(jnp.int32),
                   jax.ShapeDtypeStruct((n_exp, V7X_LANES), F32)],
        scratch_shapes=[pltpu.VMEM((n_exp, V7X_LANES), F32)],
        compiler_params=pltpu.CompilerParams(dimension_semantics=("arbitrary",), vmem_limit_bytes=V7X_VMEM_LIMIT),
        name="route",
    )(x1, router_w.T.astype(BF16), router_bias.reshape(n_exp, 1).astype(F32), triu, ones)


def _plan_kernel(idx_ref, rank_ref, pstart_ref, dest_ref):
    n_exp = pstart_ref.shape[0]
    cols = idx_ref.shape[1]
    rowid = lax.broadcasted_iota(jnp.int32, (n_exp, cols), 0)
    pstart = pstart_ref[...]
    for k in range(TOP_K):
        base = jnp.sum(jnp.where(rowid == idx_ref[k:k + 1, :], pstart, 0.0), axis=0, keepdims=True)
        dest_ref[k:k + 1, :] = base.astype(jnp.int32) + rank_ref[k:k + 1, :]


def _plan(idx, rank, pad_starts):
    tokens = idx.shape[1]
    n_exp = pad_starts.shape[0]
    cols = ROUTE_COLS
    row_spec = pl.BlockSpec((TOP_K, cols), lambda i: (0, i))
    return pl.pallas_call(
        _plan_kernel,
        grid=(tokens // cols,),
        in_specs=[row_spec, row_spec, _const_spec((n_exp, 1))],
        out_specs=row_spec,
        out_shape=jax.ShapeDtypeStruct((TOP_K, tokens), jnp.int32),
        compiler_params=pltpu.CompilerParams(dimension_semantics=("arbitrary",)),
        name="plan",
    )(idx, rank, pad_starts.reshape(n_exp, 1).astype(F32))


def _dispatch_kernel(pad_end_ref, padded_ref, dest_ref, x_ref, xs_hbm, zero_ref, sem):
    rows = x_ref.shape[0]
    n_exp = pad_end_ref.shape[0]

    def zero_copy(e):
        start = pl.multiple_of(pad_end_ref[e] - ROW_BLOCK, ROW_BLOCK)
        return pltpu.make_async_copy(zero_ref, xs_hbm.at[pl.ds(start, ROW_BLOCK), :], sem)

    @pl.when(pl.program_id(0) == 0)
    def _():
        zero_ref[...] = jnp.zeros_like(zero_ref)

        def issue(e, c):
            @pl.when(padded_ref[e] > 0)
            def _():
                zero_copy(e).start()
            return c
        lax.fori_loop(0, n_exp, issue, 0)

        def drain(e, c):
            @pl.when(padded_ref[e] > 0)
            def _():
                zero_copy(e).wait()
            return c
        lax.fori_loop(0, n_exp, drain, 0)

    def row_copy(k, r):
        return pltpu.make_async_copy(x_ref.at[pl.ds(r, 1), :], xs_hbm.at[pl.ds(dest_ref[r * TOP_K + k], 1), :], sem)

    def issue(g, c):
        for u in range(DMA_UNROLL):
            for k in range(TOP_K):
                row_copy(k, g * DMA_UNROLL + u).start(priority=k % 2)
        return c
    lax.fori_loop(0, rows // DMA_UNROLL, issue, 0)

    def drain(g, c):
        for u in range(DMA_UNROLL):
            for k in range(TOP_K):
                row_copy(k, g * DMA_UNROLL + u).wait()
        return c
    lax.fori_loop(0, rows // DMA_UNROLL, drain, 0)


def _dispatch(x1, dest, pad_ends, padded, n_rows):
    tokens, d_model = x1.shape
    rows = DISPATCH_ROWS
    grid_spec = pltpu.PrefetchScalarGridSpec(
        num_scalar_prefetch=2,
        grid=(tokens // rows,),
        in_specs=[pl.BlockSpec((TOP_K * rows,), lambda i, pe, pd: (i,), memory_space=pltpu.SMEM),
                  pl.BlockSpec((rows, d_model), lambda i, pe, pd: (i, 0))],
        out_specs=pl.BlockSpec(memory_space=pl.ANY),
        scratch_shapes=[pltpu.VMEM((ROW_BLOCK, d_model), F32), pltpu.SemaphoreType.DMA(())],
    )
    return pl.pallas_call(
        _dispatch_kernel,
        grid_spec=grid_spec,
        out_shape=jax.ShapeDtypeStruct((n_rows, d_model), F32),
        compiler_params=pltpu.CompilerParams(dimension_semantics=("arbitrary",), has_side_effects=True),
        name="dispatch",
    )(pad_ends, padded, dest, x1)


def _experts_kernel(block_e_ref, n_used_ref, xs_ref, wg_ref, wu_ref, wd_ref, y_ref, wg_b, wu_b, wd_b):
    i = pl.program_id(0)

    @pl.when(i < n_used_ref[0])
    def _():
        @pl.when((i == 0) | (block_e_ref[i] != block_e_ref[jnp.maximum(i - 1, 0)]))
        def _():
            wg_b[...] = wg_ref[...].astype(BF16)
            wu_b[...] = wu_ref[...].astype(BF16)
            wd_b[...] = wd_ref[...].astype(BF16)

        xb = xs_ref[...].astype(BF16)
        hidden = _silu(_dot(xb, wg_b[...])) * _dot(xb, wu_b[...])
        y_ref[...] = _dot(hidden.astype(BF16), wd_b[...])

    @pl.when(i >= n_used_ref[0])
    def _():
        y_ref[...] = jnp.zeros_like(y_ref)


def _experts(xs, block_e, n_used, w_gate, w_up, w_down):
    n_rows, d_model = xs.shape
    n_exp, _, d_exp = w_gate.shape
    n_blocks = block_e.shape[0]
    rows = ROW_BLOCK
    assert n_rows == n_blocks * rows
    used_row = lambda i, be, nu: (jnp.minimum(i, jnp.maximum(nu[0] - 1, 0)), 0)
    grid_spec = pltpu.PrefetchScalarGridSpec(
        num_scalar_prefetch=2,
        grid=(n_blocks,),
        in_specs=[pl.BlockSpec((rows, d_model), used_row),
                  pl.BlockSpec((None, d_model, d_exp), lambda i, be, nu: (be[i], 0, 0)),
                  pl.BlockSpec((None, d_model, d_exp), lambda i, be, nu: (be[i], 0, 0)),
                  pl.BlockSpec((None, d_exp, d_model), lambda i, be, nu: (be[i], 0, 0))],
        out_specs=pl.BlockSpec((rows, d_model), lambda i, be, nu: (i, 0)),
        scratch_shapes=[pltpu.VMEM((d_model, d_exp), BF16),
                        pltpu.VMEM((d_model, d_exp), BF16),
                        pltpu.VMEM((d_exp, d_model), BF16)],
    )
    return pl.pallas_call(
        _experts_kernel,
        grid_spec=grid_spec,
        out_shape=jax.ShapeDtypeStruct((n_rows, d_model), F32),
        compiler_params=pltpu.CompilerParams(dimension_semantics=("arbitrary",), vmem_limit_bytes=V7X_VMEM_LIMIT),
        name="experts",
    )(block_e, n_used, xs, w_gate, w_up, w_down)


def _combine_kernel(dest_ref, wt_ref, x_ref, y_hbm, sg_ref, su_ref, sd_ref, g_ref, b_ref,
                    out_ref, ybuf_ref, sem, *, alpha):
    rows = x_ref.shape[0]

    def row_copy(k, r):
        return pltpu.make_async_copy(y_hbm.at[pl.ds(dest_ref[r * TOP_K + k], 1), :],
                                     ybuf_ref.at[k, pl.ds(r, 1), :], sem)

    def issue(g, c):
        for u in range(DMA_UNROLL):
            for k in range(TOP_K):
                row_copy(k, g * DMA_UNROLL + u).start(priority=k % 2)
        return c
    lax.fori_loop(0, rows // DMA_UNROLL, issue, 0)

    x = x_ref[...]
    xb = x.astype(BF16)
    hidden = _silu(_dot(xb, sg_ref[...])) * _dot(xb, su_ref[...])
    acc = alpha * x + _dot(hidden.astype(BF16), sd_ref[...])

    def drain(g, c):
        for u in range(DMA_UNROLL):
            for k in range(TOP_K):
                row_copy(k, g * DMA_UNROLL + u).wait()
        return c
    lax.fori_loop(0, rows // DMA_UNROLL, drain, 0)

    wt = wt_ref[...]
    for k in range(TOP_K):
        acc = acc + ybuf_ref[k] * wt[:, k:k + 1]
    out_ref[...] = _layernorm_rows(acc, g_ref[...], b_ref[...])


def _combine(x1, y, dest, w_tok, sw_gate, sw_up, sw_down, ln2_g, ln2_b, alpha):
    tokens, d_model = x1.shape
    d_shared = sw_gate.shape[1]
    rows = COMBINE_ROWS
    assert tokens % rows == 0
    row2 = lambda a: a.reshape(1, -1).astype(F32)
    return pl.pallas_call(
        functools.partial(_combine_kernel, alpha=alpha),
        grid=(tokens // rows,),
        in_specs=[pl.BlockSpec((TOP_K * rows,), lambda i: (i,), memory_space=pltpu.SMEM),
                  pl.BlockSpec((rows, TOP_K), lambda i: (i, 0)),
                  pl.BlockSpec((rows, d_model), lambda i: (i, 0)),
                  pl.BlockSpec(memory_space=pl.ANY),
                  _const_spec((d_model, d_shared)),
                  _const_spec((d_model, d_shared)),
                  _const_spec((d_shared, d_model)),
                  _const_spec((1, d_model)),
                  _const_spec((1, d_model))],
        out_specs=pl.BlockSpec((rows, d_model), lambda i: (i, 0)),
        out_shape=jax.ShapeDtypeStruct((tokens, d_model), F32),
        scratch_shapes=[pltpu.VMEM((TOP_K, rows, d_model), F32), pltpu.SemaphoreType.DMA(())],
        compiler_params=pltpu.CompilerParams(dimension_semantics=("arbitrary",), vmem_limit_bytes=V7X_VMEM_LIMIT),
        name="combine",
    )(dest, w_tok, x1, y, sw_gate.astype(BF16), sw_up.astype(BF16), sw_down.astype(BF16), row2(ln2_g), row2(ln2_b))


def _block_layout(counts, tokens):
    n_exp = counts.shape[0]
    padded = (counts + ROW_BLOCK - 1) // ROW_BLOCK * ROW_BLOCK
    pad_ends = jnp.cumsum(padded)
    pad_starts = pad_ends - padded
    n_blocks = -(-(tokens * TOP_K + n_exp * (ROW_BLOCK - 1)) // ROW_BLOCK)
    block_start = jnp.arange(n_blocks, dtype=jnp.int32) * ROW_BLOCK
    block_e = jnp.sum((block_start[:, None] >= pad_ends[None, :]).astype(jnp.int32), axis=1)
    block_e = jnp.minimum(block_e, n_exp - 1)
    n_used = (pad_ends[-1:] // ROW_BLOCK).astype(jnp.int32)
    return padded, pad_ends, pad_starts, block_e, n_used, n_blocks * ROW_BLOCK


def kernel(x, w_in, ret_norm_g, gla_gate_w2, gla_gate_b, gla_norm_g, w_ret_out, w_gla_out, w_o, ln1_g, ln1_b, router_w, router_bias, exp_w_gate, exp_w_up, exp_w_down, shared_w_gate, shared_w_up, shared_w_down, ln2_g, ln2_b):
    batch, seq, d_model = x.shape
    depth = w_in.shape[0]
    alpha = (2.0 * depth) ** 0.25
    for l in range(depth):
        x1 = _mix(x, w_in[l], ret_norm_g[l], gla_gate_w2[l], gla_gate_b[l], gla_norm_g[l],
                  w_ret_out[l], w_gla_out[l], w_o[l], ln1_g[l], ln1_b[l], alpha)
        x1 = x1.reshape(batch * seq, d_model)
        idx, w_sel, rank, counts = _route(x1, router_w[l], router_bias[l])
        padded, pad_ends, pad_starts, block_e, n_used, n_rows = _block_layout(
            counts[:, 0].astype(jnp.int32), batch * seq)
        dest = _plan(idx, rank, pad_starts).T.reshape(-1)
        xs = _dispatch(x1, dest, pad_ends, padded, n_rows)
        y = _experts(xs, block_e, n_used, exp_w_gate[l], exp_w_up[l], exp_w_down[l])
        out = _combine(x1, y, dest, w_sel.T, shared_w_gate[l], shared_w_up[l], shared_w_down[l],
                       ln2_g[l], ln2_b[l], alpha)
        x = out.reshape(batch, seq, d_model)
    return x
```

```python
import functools

import jax
import jax.numpy as jnp
import numpy as np
from jax import lax
from jax.experimental import pallas as pl
from jax.experimental.pallas import tpu as pltpu

CHUNK = 64
RET_HEADS = 4
RET_DK = 128
RET_DV = 256
GLA_HEADS = 4
GLA_DK = 128
GLA_DV = 256
GLA_GATE_RANK = 16
GLA_GATE_TAU = 16.0
ROPE_THETA = 10000.0
N_EXPERTS = 256
TOP_K = 8
N_GROUPS = 8
TOPK_GROUPS = 4
ROUTED_SCALE = 2.5
LN_EPS = 1e-5
NORM_EPS = 1e-6

V7X_LANES = 128
V7X_VMEM_LIMIT = 60 * 1024 * 1024

MIX_ROWS = 256
ROUTE_COLS = 512
ROW_BLOCK = 256
DISPATCH_ROWS = 512
DMA_UNROLL = 2
COMBINE_ROWS = 256

F32 = jnp.float32
BF16 = jnp.bfloat16


def _dot(a, b):
    return jnp.dot(a, b, preferred_element_type=F32)


def _dot_nt(a, b):
    return lax.dot_general(a, b, (((1,), (1,)), ((), ())), preferred_element_type=F32)


def _dot_tn(a, b):
    return lax.dot_general(a, b, (((0,), (0,)), ((), ())), preferred_element_type=F32)


def _sigmoid(v):
    return 1.0 / (1.0 + jnp.exp(-v))


def _silu(v):
    return v * _sigmoid(v)


def _pack_rows(v):
    half = v.shape[1] // 2
    hi = lax.bitcast_convert_type(v[:, :half].astype(BF16).astype(F32), jnp.uint32)
    lo = lax.bitcast_convert_type(v[:, half:].astype(BF16).astype(F32), jnp.uint32)
    return hi | (lo >> 16)


def _unpack_rows(p):
    hi = lax.bitcast_convert_type(p & jnp.uint32(0xFFFF0000), F32)
    lo = lax.bitcast_convert_type(p << 16, F32)
    return hi, lo


def _layernorm_rows(v, g, b):
    mu = jnp.mean(v, axis=-1, keepdims=True)
    vc = v - mu
    var = jnp.mean(vc * vc, axis=-1, keepdims=True)
    return vc * lax.rsqrt(var + LN_EPS) * g + b


def _mix_kernel(x_ref, wrq_ref, wrk_ref, wrv_ref, wrg_ref, wgq_ref, wgk_ref, wgv_ref, wgg_ref, wga_ref, wmg_ref,
                cos_ref, sin_ref, dmask_ref, qdec_ref, kdec_ref, tri_ref, w2_ref, gb_ref, retg_ref, glag_ref,
                wro_ref, wgo_ref, wo_ref, ln1g_ref, ln1b_ref,
                out_ref, packed_ref, rstate_ref, gstate_ref, yret_ref, ygla_ref, *, block_decay, alpha):
    rows = x_ref.shape[0]

    @pl.when(pl.program_id(1) == 0)
    def _():
        rstate_ref[...] = jnp.zeros_like(rstate_ref)
        gstate_ref[...] = jnp.zeros_like(gstate_ref)

    x = x_ref[...]
    xb = x.astype(BF16)

    def proj(w_ref):
        return _dot(xb, w_ref[...])

    cos = cos_ref[...]
    sin = sin_ref[...]

    def rope(t):
        return t * cos + pltpu.roll(t, RET_DK // 2, 1) * sin

    rq = proj(wrq_ref)
    rk = proj(wrk_ref)
    rv = proj(wrv_ref)
    rg = proj(wrg_ref)
    for h in range(RET_HEADS):
        qk = slice(h * RET_DK, (h + 1) * RET_DK)
        vv = slice(h * RET_DV, (h + 1) * RET_DV)
        q = rope(rq[:, qk])
        k = rope(rk[:, qk])
        v = rv[:, vv].astype(BF16)
        scores = _dot_nt(q.astype(BF16), k.astype(BF16)) * dmask_ref[h]
        o = _dot(scores.astype(BF16), v)
        state = rstate_ref[h]
        o = o + _dot((q * qdec_ref[h]).astype(BF16), state.astype(BF16))
        rstate_ref[h] = state * block_decay[h] + _dot_tn((k * kdec_ref[h]).astype(BF16), v)
        mu = jnp.mean(o, axis=-1, keepdims=True)
        oc = o - mu
        var = jnp.mean(oc * oc, axis=-1, keepdims=True)
        y = oc * lax.rsqrt(var + LN_EPS) * retg_ref[:, vv] * _silu(rg[:, vv])
        yret_ref[:, vv] = y.astype(BF16)

    gq = proj(wgq_ref) * (GLA_DK ** -0.5)
    gk = proj(wgk_ref)
    gv = proj(wgv_ref)
    gg = proj(wgg_ref)
    ga = proj(wga_ref)
    z = _dot(ga.astype(BF16), w2_ref[...]) + gb_ref[...]
    log_a = (jnp.minimum(z, 0.0) - jnp.log1p(jnp.exp(-jnp.abs(z)))) * (1.0 / GLA_GATE_TAU)
    la_hi = log_a.astype(BF16)
    la_lo = (log_a - la_hi.astype(F32)).astype(BF16)
    tri = tri_ref[...]
    bcum = _dot(tri, la_hi) + _dot(tri, la_lo)
    for c in range(rows // CHUNK):
        rs = slice(c * CHUNK, (c + 1) * CHUNK)
        b_end = bcum[(c + 1) * CHUNK - 1:(c + 1) * CHUNK, :]
        kt = gk[rs, :] * jnp.exp(b_end - bcum[rs, :])
        dec = jnp.exp(b_end)
        for h in range(GLA_HEADS):
            qk = slice(h * GLA_DK, (h + 1) * GLA_DK)
            vv = slice(h * GLA_DV, (h + 1) * GLA_DV)
            state_t = gstate_ref[h] * dec[:, qk] + _dot_tn(gv[rs, vv].astype(BF16), kt[:, qk].astype(BF16))
            gstate_ref[h] = state_t
            o = _dot_nt(gq[rs, qk].astype(BF16), state_t.astype(BF16))
            ms = jnp.mean(o * o, axis=-1, keepdims=True)
            y = o * lax.rsqrt(ms + NORM_EPS) * glag_ref[:, vv] * _silu(gg[rs, vv])
            ygla_ref[rs, vv] = y.astype(BF16)

    d_model = x.shape[1]
    u_ret = _dot(yret_ref[...], wro_ref[...])
    u_gla = _dot(ygla_ref[...], wgo_ref[...])
    gate = _sigmoid(proj(wmg_ref))
    merged = gate[:, :d_model] * u_ret + gate[:, d_model:] * u_gla
    mix = _dot(merged.astype(BF16), wo_ref[...])
    out = _layernorm_rows(alpha * x + mix, ln1g_ref[...], ln1b_ref[...])
    out_ref[...] = out
    packed_ref[...] = _pack_rows(out)


def _mix_tables(seq, rows):
    half = RET_DK // 2
    inv = ROPE_THETA ** (-np.arange(half, dtype=np.float64) / half)
    ang = np.arange(seq, dtype=np.float64)[:, None] * inv[None, :]
    cos2 = np.concatenate([np.cos(ang), np.cos(ang)], axis=1)
    sin2 = np.concatenate([-np.sin(ang), np.sin(ang)], axis=1)
    log_g = np.log1p(-np.exp2(-5.0 - np.arange(RET_HEADS, dtype=np.float64)))
    j = np.arange(rows, dtype=np.float64)
    same_or_earlier_chunk = (j[None, :] // CHUNK) <= (j[:, None] // CHUNK)
    k_scale = RET_DK ** -0.5
    dmask = np.exp(log_g[:, None, None] * np.abs(j[:, None] - j[None, :])) * same_or_earlier_chunk[None] * k_scale
    qdec = np.exp(log_g[:, None] * (j[None, :] + 1.0))
    kdec = np.exp(log_g[:, None] * (rows - 1.0 - j[None, :])) * k_scale
    qdec = np.broadcast_to(qdec[:, :, None], (RET_HEADS, rows, RET_DK))
    kdec = np.broadcast_to(kdec[:, :, None], (RET_HEADS, rows, RET_DK))
    block_decay = tuple(float(v) for v in np.exp(log_g * rows))
    tri = ((j[None, :] <= j[:, None]) & ((j[None, :] // CHUNK) == (j[:, None] // CHUNK)))
    to = lambda a, dt: jnp.asarray(np.ascontiguousarray(a), dtype=dt)
    return (to(cos2, F32), to(sin2, F32), to(dmask, F32), to(qdec, F32), to(kdec, F32), to(tri, BF16)), block_decay


def _const_spec(shape):
    nd = len(shape)
    return pl.BlockSpec(shape, lambda *_: (0,) * nd, pipeline_mode=pl.Buffered(1))


def _mix(x, w_in, ret_norm_g, gla_gate_w2, gla_gate_b, gla_norm_g, w_ret_out, w_gla_out, w_o, ln1_g, ln1_b, alpha):
    batch, seq, d_model = x.shape
    rows = MIX_ROWS
    assert seq % rows == 0 and rows % CHUNK == 0
    ret_qk, ret_v = RET_HEADS * RET_DK, RET_HEADS * RET_DV
    gla_qk, gla_v = GLA_HEADS * GLA_DK, GLA_HEADS * GLA_DV
    splits = (ret_qk, ret_qk, ret_v, ret_v, gla_qk, gla_qk, gla_v, gla_v, GLA_GATE_RANK, 2 * d_model)
    assert w_in.shape == (d_model, sum(splits))
    offs = np.cumsum((0,) + splits)
    parts = [w_in[:, offs[i]:offs[i + 1]].astype(BF16) for i in range(len(splits))]
    parts[8] = jnp.pad(parts[8], ((0, 0), (0, V7X_LANES - GLA_GATE_RANK)))
    w2 = jnp.pad(gla_gate_w2.astype(BF16), ((0, V7X_LANES - GLA_GATE_RANK), (0, 0)))
    (cos2, sin2, dmask, qdec, kdec, tri), block_decay = _mix_tables(seq, rows)
    row2 = lambda a: a.reshape(1, -1).astype(F32)
    consts = [dmask, qdec, kdec, tri, w2, row2(gla_gate_b), row2(ret_norm_g), row2(gla_norm_g),
              w_ret_out.astype(BF16), w_gla_out.astype(BF16), w_o.astype(BF16), row2(ln1_g), row2(ln1_b)]
    pos_spec = pl.BlockSpec((rows, RET_DK), lambda b, s: (s, 0))
    in_specs = ([pl.BlockSpec((None, rows, d_model), lambda b, s: (b, s, 0))]
                + [_const_spec(p.shape) for p in parts]
                + [pos_spec, pos_spec]
                + [_const_spec(c.shape) for c in consts])
    return pl.pallas_call(
        functools.partial(_mix_kernel, block_decay=block_decay, alpha=alpha),
        grid=(batch, seq // rows),
        in_specs=in_specs,
        out_specs=[pl.BlockSpec((None, rows, d_model), lambda b, s: (b, s, 0)),
                   pl.BlockSpec((None, rows, d_model // 2), lambda b, s: (b, s, 0))],
        out_shape=[jax.ShapeDtypeStruct((batch, seq, d_model), F32),
                   jax.ShapeDtypeStruct((batch, seq, d_model // 2), jnp.uint32)],
        scratch_shapes=[pltpu.VMEM((RET_HEADS, RET_DK, RET_DV), F32),
                        pltpu.VMEM((GLA_HEADS, GLA_DV, GLA_DK), F32),
                        pltpu.VMEM((rows, ret_v), BF16),
                        pltpu.VMEM((rows, gla_v), BF16)],
        compiler_params=pltpu.CompilerParams(dimension_semantics=("arbitrary", "arbitrary"),
                                             vmem_limit_bytes=V7X_VMEM_LIMIT),
        name="mix",
    )(x, *parts, cos2, sin2, *consts)


def _route_kernel(x_ref, rwt_ref, bias_ref, triu_ref, ones_ref,
                  idx_ref, w_ref, rank_ref, counts_ref, carry_ref):
    cols = x_ref.shape[0]
    n_exp = rwt_ref.shape[0]
    per_group = n_exp // N_GROUPS
    neg_inf = -jnp.inf

    @pl.when(pl.program_id(0) == 0)
    def _():
        carry_ref[...] = jnp.zeros_like(carry_ref)

    logits = _dot_nt(rwt_ref[...], x_ref[...].astype(BF16))
    scores = _sigmoid(logits)
    biased = scores + bias_ref[...]

    sub = lax.broadcasted_iota(jnp.int32, (per_group, cols), 0)
    gscore = []
    for g in range(N_GROUPS):
        blk = biased[g * per_group:(g + 1) * per_group, :]
        m1 = jnp.max(blk, axis=0, keepdims=True)
        i1 = jnp.min(jnp.where(blk == m1, sub, per_group), axis=0, keepdims=True)
        m2 = jnp.max(jnp.where(sub == i1, neg_inf, blk), axis=0, keepdims=True)
        gscore.append(m1 + m2)
    masked = []
    for g in range(N_GROUPS):
        ahead = jnp.zeros((1, cols), jnp.int32)
        for o in range(N_GROUPS):
            if o == g:
                continue
            before = (gscore[o] >= gscore[g]) if o < g else (gscore[o] > gscore[g])
            ahead = ahead + before.astype(jnp.int32)
        keep = ahead < TOPK_GROUPS
        blk = biased[g * per_group:(g + 1) * per_group, :]
        masked.append(jnp.where(keep, blk, neg_inf))
    candidates = jnp.concatenate(masked, axis=0)
    cur = candidates

    rowid = lax.broadcasted_iota(jnp.int32, (n_exp, cols), 0)
    picked = []
    weights = []
    for _ in range(TOP_K):
        m = jnp.max(cur, axis=0, keepdims=True)
        ik = jnp.min(jnp.where(cur == m, rowid, n_exp), axis=0, keepdims=True)
        sel = rowid == ik
        weights.append(jnp.sum(jnp.where(sel, scores, 0.0), axis=0, keepdims=True))
        cur = jnp.where(sel, neg_inf, cur)
        picked.append(ik)
    wsum = weights[0]
    for wk in weights[1:]:
        wsum = wsum + wk

    chosen = (cur == neg_inf) & (candidates != neg_inf)
    chosen_b = jnp.where(chosen, 1.0, 0.0).astype(BF16)
    carry = carry_ref[...]
    before = _dot(chosen_b, triu_ref[...]) + jnp.concatenate([carry] * (cols // V7X_LANES), axis=1)
    for k in range(TOP_K):
        sel = rowid == picked[k]
        rank_ref[k:k + 1, :] = jnp.sum(jnp.where(sel, before, 0.0), axis=0, keepdims=True).astype(jnp.int32)
        idx_ref[k:k + 1, :] = picked[k]
        w_ref[k:k + 1, :] = weights[k] / wsum * ROUTED_SCALE
    carry = carry + _dot(chosen_b, ones_ref[...])
    carry_ref[...] = carry
    counts_ref[...] = carry


def _route(x1, router_w, router_bias):
    tokens, d_model = x1.shape
    n_exp = router_w.shape[1]
    cols = ROUTE_COLS
    assert tokens % cols == 0 and n_exp % N_GROUPS == 0
    j = np.arange(cols)
    triu = jnp.asarray((j[:, None] < j[None, :]), dtype=BF16)
    ones = jnp.ones((cols, V7X_LANES), BF16)
    out_row = lambda dt: jax.ShapeDtypeStruct((TOP_K, tokens), dt)
    row_spec = pl.BlockSpec((TOP_K, cols), lambda i: (0, i))
    return pl.pallas_call(
        _route_kernel,
        grid=(tokens // cols,),
        in_specs=[pl.BlockSpec((cols, d_model), lambda i: (i, 0)),
                  _const_spec((n_exp, d_model)),
                  _const_spec((n_exp, 1)),
                  _const_spec((cols, cols)),
                  _const_spec((cols, V7X_LANES))],
        out_specs=[row_spec, row_spec, row_spec, pl.BlockSpec((n_exp, V7X_LANES), lambda i: (0, 0))],
        out_shape=[out_row(jnp.int32), out_row(F32), out_row(jnp.int32),
                   jax.ShapeDtypeStruct((n_exp, V7X_LANES), F32)],
        scratch_shapes=[pltpu.VMEM((n_exp, V7X_LANES), F32)],
        compiler_params=pltpu.CompilerParams(dimension_semantics=("arbitrary",), vmem_limit_bytes=V7X_VMEM_LIMIT),
        name="route",
    )(x1, router_w.T.astype(BF16), router_bias.reshape(n_exp, 1).astype(F32), triu, ones)


def _plan_kernel(idx_ref, rank_ref, pstart_ref, dest_ref):
    n_exp = pstart_ref.shape[0]
    cols = idx_ref.shape[1]
    rowid = lax.broadcasted_iota(jnp.int32, (n_exp, cols), 0)
    pstart = pstart_ref[...]
    for k in range(TOP_K):
        base = jnp.sum(jnp.where(rowid == idx_ref[k:k + 1, :], pstart, 0.0), axis=0, keepdims=True)
        dest_ref[k:k + 1, :] = base.astype(jnp.int32) + rank_ref[k:k + 1, :]


def _plan(idx, rank, pad_starts):
    tokens = idx.shape[1]
    n_exp = pad_starts.shape[0]
    cols = ROUTE_COLS
    row_spec = pl.BlockSpec((TOP_K, cols), lambda i: (0, i))
    return pl.pallas_call(
        _plan_kernel,
        grid=(tokens // cols,),
        in_specs=[row_spec, row_spec, _const_spec((n_exp, 1))],
        out_specs=row_spec,
        out_shape=jax.ShapeDtypeStruct((TOP_K, tokens), jnp.int32),
        compiler_params=pltpu.CompilerParams(dimension_semantics=("arbitrary",)),
        name="plan",
    )(idx, rank, pad_starts.reshape(n_exp, 1).astype(F32))


def _dispatch_kernel(pad_end_ref, padded_ref, dest_ref, x_ref, xs_hbm, zero_ref, sem):
    rows = x_ref.shape[0]
    n_exp = pad_end_ref.shape[0]

    def zero_copy(e):
        start = pl.multiple_of(pad_end_ref[e] - ROW_BLOCK, ROW_BLOCK)
        return pltpu.make_async_copy(zero_ref, xs_hbm.at[pl.ds(start, ROW_BLOCK), :], sem)

    @pl.when(pl.program_id(0) == 0)
    def _():
        zero_ref[...] = jnp.zeros_like(zero_ref)

        def issue(e, c):
            @pl.when(padded_ref[e] > 0)
            def _():
                zero_copy(e).start()
            return c
        lax.fori_loop(0, n_exp, issue, 0)

        def drain(e, c):
            @pl.when(padded_ref[e] > 0)
            def _():
                zero_copy(e).wait()
            return c
        lax.fori_loop(0, n_exp, drain, 0)

    def row_copy(k, r):
        return pltpu.make_async_copy(x_ref.at[pl.ds(r, 1), :], xs_hbm.at[pl.ds(dest_ref[r * TOP_K + k], 1), :], sem)

    def issue(g, c):
        for u in range(DMA_UNROLL):
            for k in range(TOP_K):
                row_copy(k, g * DMA_UNROLL + u).start(priority=k % 2)
        return c
    lax.fori_loop(0, rows // DMA_UNROLL, issue, 0)

    def drain(g, c):
        for u in range(DMA_UNROLL):
            for k in range(TOP_K):
                row_copy(k, g * DMA_UNROLL + u).wait()
        return c
    lax.fori_loop(0, rows // DMA_UNROLL, drain, 0)


def _dispatch(x1p, dest, pad_ends, padded, n_rows):
    tokens, width = x1p.shape
    rows = DISPATCH_ROWS
    grid_spec = pltpu.PrefetchScalarGridSpec(
        num_scalar_prefetch=2,
        grid=(tokens // rows,),
        in_specs=[pl.BlockSpec((TOP_K * rows,), lambda i, pe, pd: (i,), memory_space=pltpu.SMEM),
                  pl.BlockSpec((rows, width), lambda i, pe, pd: (i, 0))],
        out_specs=pl.BlockSpec(memory_space=pl.ANY),
        scratch_shapes=[pltpu.VMEM((ROW_BLOCK, width), x1p.dtype), pltpu.SemaphoreType.DMA(())],
    )
    return pl.pallas_call(
        _dispatch_kernel,
        grid_spec=grid_spec,
        out_shape=jax.ShapeDtypeStruct((n_rows, width), x1p.dtype),
        compiler_params=pltpu.CompilerParams(dimension_semantics=("arbitrary",), has_side_effects=True),
        name="dispatch",
    )(pad_ends, padded, dest, x1p)


def _experts_kernel(first_ref, nblk_ref, xs_hbm, wg_ref, wu_ref, wd_ref, y_hbm,
                    xbuf_ref, ybuf_ref, wg_b, wu_b, wd_b, in_sem, out_sem):
    e = pl.program_id(0)
    n = nblk_ref[e]
    first = first_ref[e]

    def block_rows(j):
        return pl.ds(pl.multiple_of((first + j) * ROW_BLOCK, ROW_BLOCK), ROW_BLOCK)

    def in_copy(j, slot):
        return pltpu.make_async_copy(xs_hbm.at[block_rows(j), :], xbuf_ref.at[slot], in_sem.at[slot])

    def out_copy(j, slot):
        return pltpu.make_async_copy(ybuf_ref.at[slot], y_hbm.at[block_rows(j), :], out_sem.at[slot])

    @pl.when(n > 0)
    def _():
        in_copy(0, 0).start()
        wg_b[...] = wg_ref[...].astype(BF16)
        wu_b[...] = wu_ref[...].astype(BF16)
        wd_b[...] = wd_ref[...].astype(BF16)

        def body(j, c):
            slot = lax.rem(j, 2)
            in_copy(j, slot).wait()

            @pl.when(j + 1 < n)
            def _():
                in_copy(j + 1, 1 - slot).start()

            @pl.when(j >= 2)
            def _():
                out_copy(j - 2, slot).wait()

            hi, lo = _unpack_rows(xbuf_ref[slot])
            xb = jnp.concatenate([hi, lo], axis=1).astype(BF16)
            hidden = _silu(_dot(xb, wg_b[...])) * _dot(xb, wu_b[...])
            ybuf_ref[slot] = _pack_rows(_dot(hidden.astype(BF16), wd_b[...]))
            out_copy(j, slot).start()
            return c
        lax.fori_loop(0, n, body, 0)

        @pl.when(n >= 2)
        def _():
            out_copy(n - 2, lax.rem(n, 2)).wait()
        out_copy(n - 1, lax.rem(n - 1, 2)).wait()


def _experts(xs, first_block, n_block, w_gate, w_up, w_down):
    n_rows, width = xs.shape
    n_exp, d_model, d_exp = w_gate.shape
    assert width * 2 == d_model and n_rows % ROW_BLOCK == 0
    grid_spec = pltpu.PrefetchScalarGridSpec(
        num_scalar_prefetch=2,
        grid=(n_exp,),
        in_specs=[pl.BlockSpec(memory_space=pl.ANY),
                  pl.BlockSpec((None, d_model, d_exp), lambda e, fb, nb: (e, 0, 0)),
                  pl.BlockSpec((None, d_model, d_exp), lambda e, fb, nb: (e, 0, 0)),
                  pl.BlockSpec((None, d_exp, d_model), lambda e, fb, nb: (e, 0, 0))],
        out_specs=pl.BlockSpec(memory_space=pl.ANY),
        scratch_shapes=[pltpu.VMEM((2, ROW_BLOCK, width), xs.dtype),
                        pltpu.VMEM((2, ROW_BLOCK, width), xs.dtype),
                        pltpu.VMEM((d_model, d_exp), BF16),
                        pltpu.VMEM((d_model, d_exp), BF16),
                        pltpu.VMEM((d_exp, d_model), BF16),
                        pltpu.SemaphoreType.DMA((2,)),
                        pltpu.SemaphoreType.DMA((2,))],
    )
    return pl.pallas_call(
        _experts_kernel,
        grid_spec=grid_spec,
        out_shape=jax.ShapeDtypeStruct((n_rows, width), xs.dtype),
        compiler_params=pltpu.CompilerParams(dimension_semantics=("arbitrary",), vmem_limit_bytes=V7X_VMEM_LIMIT,
                                             has_side_effects=True),
        name="experts",
    )(first_block, n_block, xs, w_gate, w_up, w_down)


def _combine_kernel(dest_ref, wt_ref, x_ref, y_hbm, sg_ref, su_ref, sd_ref, g_ref, b_ref,
                    out_ref, ybuf_ref, sem, *, alpha):
    rows = x_ref.shape[0]

    def row_copy(k, r):
        return pltpu.make_async_copy(y_hbm.at[pl.ds(dest_ref[r * TOP_K + k], 1), :],
                                     ybuf_ref.at[k, pl.ds(r, 1), :], sem)

    def issue(g, c):
        for u in range(DMA_UNROLL):
            for k in range(TOP_K):
                row_copy(k, g * DMA_UNROLL + u).start(priority=k % 2)
        return c
    lax.fori_loop(0, rows // DMA_UNROLL, issue, 0)

    x = x_ref[...]
    xb = x.astype(BF16)
    hidden = _silu(_dot(xb, sg_ref[...])) * _dot(xb, su_ref[...])
    acc = alpha * x + _dot(hidden.astype(BF16), sd_ref[...])

    def drain(g, c):
        for u in range(DMA_UNROLL):
            for k in range(TOP_K):
                row_copy(k, g * DMA_UNROLL + u).wait()
        return c
    lax.fori_loop(0, rows // DMA_UNROLL, drain, 0)

    wt = wt_ref[...]
    half = x.shape[1] // 2
    acc_hi = acc[:, :half]
    acc_lo = acc[:, half:]
    for k in range(TOP_K):
        hi, lo = _unpack_rows(ybuf_ref[k])
        acc_hi = acc_hi + hi * wt[:, k:k + 1]
        acc_lo = acc_lo + lo * wt[:, k:k + 1]
    out_ref[...] = _layernorm_rows(jnp.concatenate([acc_hi, acc_lo], axis=1), g_ref[...], b_ref[...])


def _combine(x1, y, dest, w_tok, sw_gate, sw_up, sw_down, ln2_g, ln2_b, alpha):
    tokens, d_model = x1.shape
    d_shared = sw_gate.shape[1]
    rows = COMBINE_ROWS
    assert tokens % rows == 0
    row2 = lambda a: a.reshape(1, -1).astype(F32)
    return pl.pallas_call(
        functools.partial(_combine_kernel, alpha=alpha),
        grid=(tokens // rows,),
        in_specs=[pl.BlockSpec((TOP_K * rows,), lambda i: (i,), memory_space=pltpu.SMEM),
                  pl.BlockSpec((rows, TOP_K), lambda i: (i, 0)),
                  pl.BlockSpec((rows, d_model), lambda i: (i, 0)),
                  pl.BlockSpec(memory_space=pl.ANY),
                  _const_spec((d_model, d_shared)),
                  _const_spec((d_model, d_shared)),
                  _const_spec((d_shared, d_model)),
                  _const_spec((1, d_model)),
                  _const_spec((1, d_model))],
        out_specs=pl.BlockSpec((rows, d_model), lambda i: (i, 0)),
        out_shape=jax.ShapeDtypeStruct((tokens, d_model), F32),
        scratch_shapes=[pltpu.VMEM((TOP_K, rows, y.shape[1]), y.dtype), pltpu.SemaphoreType.DMA(())],
        compiler_params=pltpu.CompilerParams(dimension_semantics=("arbitrary",), vmem_limit_bytes=V7X_VMEM_LIMIT),
        name="combine",
    )(dest, w_tok, x1, y, sw_gate.astype(BF16), sw_up.astype(BF16), sw_down.astype(BF16), row2(ln2_g), row2(ln2_b))


def _block_layout(counts, tokens):
    n_exp = counts.shape[0]
    padded = (counts + ROW_BLOCK - 1) // ROW_BLOCK * ROW_BLOCK
    pad_ends = jnp.cumsum(padded)
    pad_starts = pad_ends - padded
    n_blocks = -(-(tokens * TOP_K + n_exp * (ROW_BLOCK - 1)) // ROW_BLOCK)
    return padded, pad_ends, pad_starts, pad_starts // ROW_BLOCK, padded // ROW_BLOCK, n_blocks * ROW_BLOCK


def kernel(x, w_in, ret_norm_g, gla_gate_w2, gla_gate_b, gla_norm_g, w_ret_out, w_gla_out, w_o, ln1_g, ln1_b, router_w, router_bias, exp_w_gate, exp_w_up, exp_w_down, shared_w_gate, shared_w_up, shared_w_down, ln2_g, ln2_b):
    batch, seq, d_model = x.shape
    depth = w_in.shape[0]
    alpha = (2.0 * depth) ** 0.25
    for l in range(depth):
        x1, x1p = _mix(x, w_in[l], ret_norm_g[l], gla_gate_w2[l], gla_gate_b[l], gla_norm_g[l],
                       w_ret_out[l], w_gla_out[l], w_o[l], ln1_g[l], ln1_b[l], alpha)
        x1 = x1.reshape(batch * seq, d_model)
        x1p = x1p.reshape(batch * seq, d_model // 2)
        idx, w_sel, rank, counts = _route(x1, router_w[l], router_bias[l])
        padded, pad_ends, pad_starts, first_block, n_block, n_rows = _block_layout(
            counts[:, 0].astype(jnp.int32), batch * seq)
        dest = _plan(idx, rank, pad_starts).T.reshape(-1)
        xs = _dispatch(x1p, dest, pad_ends, padded, n_rows)
        y = _experts(xs, first_block, n_block, exp_w_gate[l], exp_w_up[l], exp_w_down[l])
        out = _combine(x1, y, dest, w_sel.T, shared_w_gate[l], shared_w_up[l], shared_w_down[l],
                       ln2_g[l], ln2_b[l], alpha)
        x = out.reshape(batch, seq, d_model)
    return x
```

```python
import functools

import jax
import jax.numpy as jnp
import numpy as np
from jax import lax
from jax.experimental import pallas as pl
from jax.experimental.pallas import tpu as pltpu

CHUNK = 64
RET_HEADS = 4
RET_DK = 128
RET_DV = 256
GLA_HEADS = 4
GLA_DK = 128
GLA_DV = 256
GLA_GATE_RANK = 16
GLA_GATE_TAU = 16.0
ROPE_THETA = 10000.0
N_EXPERTS = 256
TOP_K = 8
N_GROUPS = 8
TOPK_GROUPS = 4
ROUTED_SCALE = 2.5
LN_EPS = 1e-5
NORM_EPS = 1e-6

V7X_LANES = 128
V7X_VMEM_LIMIT = 60 * 1024 * 1024

MIX_ROWS = 256
ROUTE_COLS = 512
ROW_BLOCK = 256
DISPATCH_ROWS = 512
DMA_UNROLL = 2
EXPERT_SLOTS = 4
COMBINE_ROWS = 256

F32 = jnp.float32
BF16 = jnp.bfloat16


def _dot(a, b):
    return jnp.dot(a, b, preferred_element_type=F32)


def _dot_nt(a, b):
    return lax.dot_general(a, b, (((1,), (1,)), ((), ())), preferred_element_type=F32)


def _dot_tn(a, b):
    return lax.dot_general(a, b, (((0,), (0,)), ((), ())), preferred_element_type=F32)


def _sigmoid(v):
    return 1.0 / (1.0 + jnp.exp(-v))


def _silu(v):
    return v * _sigmoid(v)


def _pack_rows(v):
    half = v.shape[1] // 2
    hi = lax.bitcast_convert_type(v[:, :half].astype(BF16).astype(F32), jnp.uint32)
    lo = lax.bitcast_convert_type(v[:, half:].astype(BF16).astype(F32), jnp.uint32)
    return hi | (lo >> 16)


def _unpack_rows(p):
    hi = lax.bitcast_convert_type(p & jnp.uint32(0xFFFF0000), F32)
    lo = lax.bitcast_convert_type(p << 16, F32)
    return hi, lo


def _store_token_rows(ref, packed, lead=()):
    m, width = packed.shape
    sub = width // V7X_LANES
    for c in range(sub):
        ref[lead + (pl.ds(c, m, stride=sub), slice(None))] = packed[:, c * V7X_LANES:(c + 1) * V7X_LANES]


def _load_token_rows(ref, m, lead=()):
    sub = ref.shape[-2] // m
    return [ref[lead + (pl.ds(c, m, stride=sub), slice(None))] for c in range(sub)]


def _layernorm_rows(v, g, b):
    mu = jnp.mean(v, axis=-1, keepdims=True)
    vc = v - mu
    var = jnp.mean(vc * vc, axis=-1, keepdims=True)
    return vc * lax.rsqrt(var + LN_EPS) * g + b


def _mix_kernel(x_ref, wrq_ref, wrk_ref, wrv_ref, wrg_ref, wgq_ref, wgk_ref, wgv_ref, wgg_ref, wga_ref, wmg_ref,
                cos_ref, sin_ref, dmask_ref, qdec_ref, kdec_ref, tri_ref, w2_ref, gb_ref, retg_ref, glag_ref,
                wro_ref, wgo_ref, wo_ref, ln1g_ref, ln1b_ref,
                out_ref, packed_ref, rstate_ref, gstate_ref, yret_ref, ygla_ref, *, block_decay, alpha):
    rows = x_ref.shape[0]

    @pl.when(pl.program_id(1) == 0)
    def _():
        rstate_ref[...] = jnp.zeros_like(rstate_ref)
        gstate_ref[...] = jnp.zeros_like(gstate_ref)

    x = x_ref[...]
    xb = x.astype(BF16)

    def proj(w_ref):
        return _dot(xb, w_ref[...])

    cos = cos_ref[...]
    sin = sin_ref[...]

    def rope(t):
        return t * cos + pltpu.roll(t, RET_DK // 2, 1) * sin

    rq = proj(wrq_ref)
    rk = proj(wrk_ref)
    rv = proj(wrv_ref)
    rg = proj(wrg_ref)
    for h in range(RET_HEADS):
        qk = slice(h * RET_DK, (h + 1) * RET_DK)
        vv = slice(h * RET_DV, (h + 1) * RET_DV)
        q = rope(rq[:, qk])
        k = rope(rk[:, qk])
        v = rv[:, vv].astype(BF16)
        scores = _dot_nt(q.astype(BF16), k.astype(BF16)) * dmask_ref[h]
        o = _dot(scores.astype(BF16), v)
        state = rstate_ref[h]
        o = o + _dot((q * qdec_ref[h]).astype(BF16), state.astype(BF16))
        rstate_ref[h] = state * block_decay[h] + _dot_tn((k * kdec_ref[h]).astype(BF16), v)
        mu = jnp.mean(o, axis=-1, keepdims=True)
        oc = o - mu
        var = jnp.mean(oc * oc, axis=-1, keepdims=True)
        y = oc * lax.rsqrt(var + LN_EPS) * retg_ref[:, vv] * _silu(rg[:, vv])
        yret_ref[:, vv] = y.astype(BF16)

    gq = proj(wgq_ref) * (GLA_DK ** -0.5)
    gk = proj(wgk_ref)
    gv = proj(wgv_ref)
    gg = proj(wgg_ref)
    ga = proj(wga_ref)
    z = _dot(ga.astype(BF16), w2_ref[...]) + gb_ref[...]
    log_a = (jnp.minimum(z, 0.0) - jnp.log1p(jnp.exp(-jnp.abs(z)))) * (1.0 / GLA_GATE_TAU)
    la_hi = log_a.astype(BF16)
    la_lo = (log_a - la_hi.astype(F32)).astype(BF16)
    tri = tri_ref[...]
    bcum = _dot(tri, la_hi) + _dot(tri, la_lo)
    for c in range(rows // CHUNK):
        rs = slice(c * CHUNK, (c + 1) * CHUNK)
        b_end = bcum[(c + 1) * CHUNK - 1:(c + 1) * CHUNK, :]
        kt = gk[rs, :] * jnp.exp(b_end - bcum[rs, :])
        dec = jnp.exp(b_end)
        for h in range(GLA_HEADS):
            qk = slice(h * GLA_DK, (h + 1) * GLA_DK)
            vv = slice(h * GLA_DV, (h + 1) * GLA_DV)
            state_t = gstate_ref[h] * dec[:, qk] + _dot_tn(gv[rs, vv].astype(BF16), kt[:, qk].astype(BF16))
            gstate_ref[h] = state_t
            o = _dot_nt(gq[rs, qk].astype(BF16), state_t.astype(BF16))
            ms = jnp.mean(o * o, axis=-1, keepdims=True)
            y = o * lax.rsqrt(ms + NORM_EPS) * glag_ref[:, vv] * _silu(gg[rs, vv])
            ygla_ref[rs, vv] = y.astype(BF16)

    d_model = x.shape[1]
    u_ret = _dot(yret_ref[...], wro_ref[...])
    u_gla = _dot(ygla_ref[...], wgo_ref[...])
    gate = _sigmoid(proj(wmg_ref))
    merged = gate[:, :d_model] * u_ret + gate[:, d_model:] * u_gla
    mix = _dot(merged.astype(BF16), wo_ref[...])
    out = _layernorm_rows(alpha * x + mix, ln1g_ref[...], ln1b_ref[...])
    out_ref[...] = out
    _store_token_rows(packed_ref, _pack_rows(out))


def _mix_tables(seq, rows):
    half = RET_DK // 2
    inv = ROPE_THETA ** (-np.arange(half, dtype=np.float64) / half)
    ang = np.arange(seq, dtype=np.float64)[:, None] * inv[None, :]
    cos2 = np.concatenate([np.cos(ang), np.cos(ang)], axis=1)
    sin2 = np.concatenate([-np.sin(ang), np.sin(ang)], axis=1)
    log_g = np.log1p(-np.exp2(-5.0 - np.arange(RET_HEADS, dtype=np.float64)))
    j = np.arange(rows, dtype=np.float64)
    same_or_earlier_chunk = (j[None, :] // CHUNK) <= (j[:, None] // CHUNK)
    k_scale = RET_DK ** -0.5
    dmask = np.exp(log_g[:, None, None] * np.abs(j[:, None] - j[None, :])) * same_or_earlier_chunk[None] * k_scale
    qdec = np.exp(log_g[:, None] * (j[None, :] + 1.0))
    kdec = np.exp(log_g[:, None] * (rows - 1.0 - j[None, :])) * k_scale
    qdec = np.broadcast_to(qdec[:, :, None], (RET_HEADS, rows, RET_DK))
    kdec = np.broadcast_to(kdec[:, :, None], (RET_HEADS, rows, RET_DK))
    block_decay = tuple(float(v) for v in np.exp(log_g * rows))
    tri = ((j[None, :] <= j[:, None]) & ((j[None, :] // CHUNK) == (j[:, None] // CHUNK)))
    to = lambda a, dt: jnp.asarray(np.ascontiguousarray(a), dtype=dt)
    return (to(cos2, F32), to(sin2, F32), to(dmask, F32), to(qdec, F32), to(kdec, F32), to(tri, BF16)), block_decay


def _const_spec(shape):
    nd = len(shape)
    return pl.BlockSpec(shape, lambda *_: (0,) * nd, pipeline_mode=pl.Buffered(1))


def _mix(x, w_in, ret_norm_g, gla_gate_w2, gla_gate_b, gla_norm_g, w_ret_out, w_gla_out, w_o, ln1_g, ln1_b, alpha):
    batch, seq, d_model = x.shape
    rows = MIX_ROWS
    assert seq % rows == 0 and rows % CHUNK == 0
    ret_qk, ret_v = RET_HEADS * RET_DK, RET_HEADS * RET_DV
    gla_qk, gla_v = GLA_HEADS * GLA_DK, GLA_HEADS * GLA_DV
    splits = (ret_qk, ret_qk, ret_v, ret_v, gla_qk, gla_qk, gla_v, gla_v, GLA_GATE_RANK, 2 * d_model)
    assert w_in.shape == (d_model, sum(splits))
    offs = np.cumsum((0,) + splits)
    parts = [w_in[:, offs[i]:offs[i + 1]].astype(BF16) for i in range(len(splits))]
    parts[8] = jnp.pad(parts[8], ((0, 0), (0, V7X_LANES - GLA_GATE_RANK)))
    w2 = jnp.pad(gla_gate_w2.astype(BF16), ((0, V7X_LANES - GLA_GATE_RANK), (0, 0)))
    (cos2, sin2, dmask, qdec, kdec, tri), block_decay = _mix_tables(seq, rows)
    row2 = lambda a: a.reshape(1, -1).astype(F32)
    consts = [dmask, qdec, kdec, tri, w2, row2(gla_gate_b), row2(ret_norm_g), row2(gla_norm_g),
              w_ret_out.astype(BF16), w_gla_out.astype(BF16), w_o.astype(BF16), row2(ln1_g), row2(ln1_b)]
    sub = d_model // 2 // V7X_LANES
    pos_spec = pl.BlockSpec((rows, RET_DK), lambda b, s: (s, 0))
    in_specs = ([pl.BlockSpec((None, rows, d_model), lambda b, s: (b, s, 0))]
                + [_const_spec(p.shape) for p in parts]
                + [pos_spec, pos_spec]
                + [_const_spec(c.shape) for c in consts])
    return pl.pallas_call(
        functools.partial(_mix_kernel, block_decay=block_decay, alpha=alpha),
        grid=(batch, seq // rows),
        in_specs=in_specs,
        out_specs=[pl.BlockSpec((None, rows, d_model), lambda b, s: (b, s, 0)),
                   pl.BlockSpec((None, rows * sub, V7X_LANES), lambda b, s: (b, s, 0))],
        out_shape=[jax.ShapeDtypeStruct((batch, seq, d_model), F32),
                   jax.ShapeDtypeStruct((batch, seq * sub, V7X_LANES), jnp.uint32)],
        scratch_shapes=[pltpu.VMEM((RET_HEADS, RET_DK, RET_DV), F32),
                        pltpu.VMEM((GLA_HEADS, GLA_DV, GLA_DK), F32),
                        pltpu.VMEM((rows, ret_v), BF16),
                        pltpu.VMEM((rows, gla_v), BF16)],
        compiler_params=pltpu.CompilerParams(dimension_semantics=("arbitrary", "arbitrary"),
                                             vmem_limit_bytes=V7X_VMEM_LIMIT),
        name="mix",
    )(x, *parts, cos2, sin2, *consts)


def _route_kernel(x_ref, rwt_ref, bias_ref, triu_ref, ones_ref,
                  idx_ref, w_ref, rank_ref, counts_ref, carry_ref):
    cols = x_ref.shape[0]
    n_exp = rwt_ref.shape[0]
    per_group = n_exp // N_GROUPS
    neg_inf = -jnp.inf

    @pl.when(pl.program_id(0) == 0)
    def _():
        carry_ref[...] = jnp.zeros_like(carry_ref)

    logits = _dot_nt(rwt_ref[...], x_ref[...].astype(BF16))
    scores = _sigmoid(logits)
    biased = scores + bias_ref[...]

    sub = lax.broadcasted_iota(jnp.int32, (per_group, cols), 0)
    gscore = []
    for g in range(N_GROUPS):
        blk = biased[g * per_group:(g + 1) * per_group, :]
        m1 = jnp.max(blk, axis=0, keepdims=True)
        i1 = jnp.min(jnp.where(blk == m1, sub, per_group), axis=0, keepdims=True)
        m2 = jnp.max(jnp.where(sub == i1, neg_inf, blk), axis=0, keepdims=True)
        gscore.append(m1 + m2)
    masked = []
    for g in range(N_GROUPS):
        ahead = jnp.zeros((1, cols), jnp.int32)
        for o in range(N_GROUPS):
            if o == g:
                continue
            before = (gscore[o] >= gscore[g]) if o < g else (gscore[o] > gscore[g])
            ahead = ahead + before.astype(jnp.int32)
        keep = ahead < TOPK_GROUPS
        blk = biased[g * per_group:(g + 1) * per_group, :]
        masked.append(jnp.where(keep, blk, neg_inf))
    candidates = jnp.concatenate(masked, axis=0)
    cur = candidates

    rowid = lax.broadcasted_iota(jnp.int32, (n_exp, cols), 0)
    picked = []
    weights = []
    for _ in range(TOP_K):
        m = jnp.max(cur, axis=0, keepdims=True)
        ik = jnp.min(jnp.where(cur == m, rowid, n_exp), axis=0, keepdims=True)
        sel = rowid == ik
        weights.append(jnp.sum(jnp.where(sel, scores, 0.0), axis=0, keepdims=True))
        cur = jnp.where(sel, neg_inf, cur)
        picked.append(ik)
    wsum = weights[0]
    for wk in weights[1:]:
        wsum = wsum + wk

    chosen = (cur == neg_inf) & (candidates != neg_inf)
    chosen_b = jnp.where(chosen, 1.0, 0.0).astype(BF16)
    carry = carry_ref[...]
    before = _dot(chosen_b, triu_ref[...]) + jnp.concatenate([carry] * (cols // V7X_LANES), axis=1)
    for k in range(TOP_K):
        sel = rowid == picked[k]
        rank_ref[k:k + 1, :] = jnp.sum(jnp.where(sel, before, 0.0), axis=0, keepdims=True).astype(jnp.int32)
        idx_ref[k:k + 1, :] = picked[k]
        w_ref[k:k + 1, :] = weights[k] / wsum * ROUTED_SCALE
    carry = carry + _dot(chosen_b, ones_ref[...])
    carry_ref[...] = carry
    counts_ref[...] = carry


def _route(x1, router_w, router_bias):
    tokens, d_model = x1.shape
    n_exp = router_w.shape[1]
    cols = ROUTE_COLS
    assert tokens % cols == 0 and n_exp % N_GROUPS == 0
    j = np.arange(cols)
    triu = jnp.asarray((j[:, None] < j[None, :]), dtype=BF16)
    ones = jnp.ones((cols, V7X_LANES), BF16)
    out_row = lambda dt: jax.ShapeDtypeStruct((TOP_K, tokens), dt)
    row_spec = pl.BlockSpec((TOP_K, cols), lambda i: (0, i))
    return pl.pallas_call(
        _route_kernel,
        grid=(tokens // cols,),
        in_specs=[pl.BlockSpec((cols, d_model), lambda i: (i, 0)),
                  _const_spec((n_exp, d_model)),
                  _const_spec((n_exp, 1)),
                  _const_spec((cols, cols)),
                  _const_spec((cols, V7X_LANES))],
        out_specs=[row_spec, row_spec, row_spec, pl.BlockSpec((n_exp, V7X_LANES), lambda i: (0, 0))],
        out_shape=[out_row(jnp.int32), out_row(F32), out_row(jnp.int32),
                   jax.ShapeDtypeStruct((n_exp, V7X_LANES), F32)],
        scratch_shapes=[pltpu.VMEM((n_exp, V7X_LANES), F32)],
        compiler_params=pltpu.CompilerParams(dimension_semantics=("arbitrary",), vmem_limit_bytes=V7X_VMEM_LIMIT),
        name="route",
    )(x1, router_w.T.astype(BF16), router_bias.reshape(n_exp, 1).astype(F32), triu, ones)


def _plan_kernel(idx_ref, rank_ref, pstart_ref, dest_ref):
    n_exp = pstart_ref.shape[0]
    cols = idx_ref.shape[1]
    rowid = lax.broadcasted_iota(jnp.int32, (n_exp, cols), 0)
    pstart = pstart_ref[...]
    for k in range(TOP_K):
        base = jnp.sum(jnp.where(rowid == idx_ref[k:k + 1, :], pstart, 0.0), axis=0, keepdims=True)
        dest_ref[k:k + 1, :] = base.astype(jnp.int32) + rank_ref[k:k + 1, :]


def _plan(idx, rank, pad_starts):
    tokens = idx.shape[1]
    n_exp = pad_starts.shape[0]
    cols = ROUTE_COLS
    row_spec = pl.BlockSpec((TOP_K, cols), lambda i: (0, i))
    return pl.pallas_call(
        _plan_kernel,
        grid=(tokens // cols,),
        in_specs=[row_spec, row_spec, _const_spec((n_exp, 1))],
        out_specs=row_spec,
        out_shape=jax.ShapeDtypeStruct((TOP_K, tokens), jnp.int32),
        compiler_params=pltpu.CompilerParams(dimension_semantics=("arbitrary",)),
        name="plan",
    )(idx, rank, pad_starts.reshape(n_exp, 1).astype(F32))


def _token_rows(tok, sub):
    return pl.ds(pl.multiple_of(tok * sub, sub), sub)


def _dispatch_kernel(pad_end_ref, padded_ref, dest_ref, x_ref, xs_hbm, zero_ref, sem, *, sub):
    rows = x_ref.shape[0] // sub
    n_exp = pad_end_ref.shape[0]

    def zero_copy(e):
        start = pl.multiple_of((pad_end_ref[e] - ROW_BLOCK) * sub, ROW_BLOCK * sub)
        return pltpu.make_async_copy(zero_ref, xs_hbm.at[pl.ds(start, ROW_BLOCK * sub), :], sem)

    @pl.when(pl.program_id(0) == 0)
    def _():
        zero_ref[...] = jnp.zeros_like(zero_ref)

        def issue(e, c):
            @pl.when(padded_ref[e] > 0)
            def _():
                zero_copy(e).start()
            return c
        lax.fori_loop(0, n_exp, issue, 0)

        def drain(e, c):
            @pl.when(padded_ref[e] > 0)
            def _():
                zero_copy(e).wait()
            return c
        lax.fori_loop(0, n_exp, drain, 0)

    def row_copy(k, r):
        return pltpu.make_async_copy(x_ref.at[_token_rows(r, sub), :],
                                     xs_hbm.at[_token_rows(dest_ref[r * TOP_K + k], sub), :], sem)

    def issue(g, c):
        for u in range(DMA_UNROLL):
            for k in range(TOP_K):
                row_copy(k, g * DMA_UNROLL + u).start(priority=k % 2)
        return c
    lax.fori_loop(0, rows // DMA_UNROLL, issue, 0)

    def drain(g, c):
        for u in range(DMA_UNROLL):
            for k in range(TOP_K):
                row_copy(k, g * DMA_UNROLL + u).wait()
        return c
    lax.fori_loop(0, rows // DMA_UNROLL, drain, 0)


def _dispatch(x1p, dest, pad_ends, padded, n_rows, sub):
    tokens = x1p.shape[0] // sub
    rows = DISPATCH_ROWS
    grid_spec = pltpu.PrefetchScalarGridSpec(
        num_scalar_prefetch=2,
        grid=(tokens // rows,),
        in_specs=[pl.BlockSpec((TOP_K * rows,), lambda i, pe, pd: (i,), memory_space=pltpu.SMEM),
                  pl.BlockSpec((rows * sub, V7X_LANES), lambda i, pe, pd: (i, 0))],
        out_specs=pl.BlockSpec(memory_space=pl.ANY),
        scratch_shapes=[pltpu.VMEM((ROW_BLOCK * sub, V7X_LANES), x1p.dtype), pltpu.SemaphoreType.DMA(())],
    )
    return pl.pallas_call(
        functools.partial(_dispatch_kernel, sub=sub),
        grid_spec=grid_spec,
        out_shape=jax.ShapeDtypeStruct((n_rows * sub, V7X_LANES), x1p.dtype),
        compiler_params=pltpu.CompilerParams(dimension_semantics=("arbitrary",), has_side_effects=True),
        name="dispatch",
    )(pad_ends, padded, dest, x1p)


def _experts_kernel(first_ref, nblk_ref, total_ref, xs_hbm, wg_ref, wu_ref, wd_ref, y_hbm,
                    xbuf_ref, ybuf_ref, wg_b, wu_b, wd_b, in_sem, out_sem, *, sub):
    e = pl.program_id(0)
    n = nblk_ref[e]
    first = first_ref[e]
    total = total_ref[0]
    ahead = EXPERT_SLOTS - 1
    block = ROW_BLOCK * sub

    def block_rows(g):
        return pl.ds(pl.multiple_of(g * block, block), block)

    def in_copy(g):
        slot = lax.rem(g, EXPERT_SLOTS)
        return pltpu.make_async_copy(xs_hbm.at[block_rows(g), :], xbuf_ref.at[slot], in_sem.at[slot])

    def out_copy(g):
        slot = lax.rem(g, EXPERT_SLOTS)
        return pltpu.make_async_copy(ybuf_ref.at[slot], y_hbm.at[block_rows(g), :], out_sem.at[slot])

    @pl.when(e == 0)
    def _():
        for g in range(ahead):
            @pl.when(g < total)
            def _():
                in_copy(g).start()

    @pl.when(n > 0)
    def _():
        wg_b[...] = wg_ref[...].astype(BF16)
        wu_b[...] = wu_ref[...].astype(BF16)
        wd_b[...] = wd_ref[...].astype(BF16)

        def body(j, c):
            g = first + j
            slot = lax.rem(g, EXPERT_SLOTS)
            in_copy(g).wait()

            @pl.when(g + ahead < total)
            def _():
                in_copy(g + ahead).start()

            @pl.when(g >= EXPERT_SLOTS)
            def _():
                out_copy(g - EXPERT_SLOTS).wait()

            halves = [_unpack_rows(p) for p in _load_token_rows(xbuf_ref, ROW_BLOCK, (slot,))]
            xb = jnp.concatenate([h for h, _ in halves] + [l for _, l in halves], axis=1).astype(BF16)
            hidden = _silu(_dot(xb, wg_b[...])) * _dot(xb, wu_b[...])
            _store_token_rows(ybuf_ref, _pack_rows(_dot(hidden.astype(BF16), wd_b[...])), (slot,))
            out_copy(g).start()
            return c
        lax.fori_loop(0, n, body, 0)

    @pl.when(e == pl.num_programs(0) - 1)
    def _():
        for d in range(EXPERT_SLOTS):
            @pl.when(total - 1 - d >= 0)
            def _():
                out_copy(total - 1 - d).wait()


def _experts(xs, first_block, n_block, w_gate, w_up, w_down, sub):
    n_exp, d_model, d_exp = w_gate.shape
    assert sub * V7X_LANES * 2 == d_model and xs.shape[0] % (ROW_BLOCK * sub) == 0
    total = jnp.sum(n_block).reshape(1).astype(jnp.int32)
    grid_spec = pltpu.PrefetchScalarGridSpec(
        num_scalar_prefetch=3,
        grid=(n_exp,),
        in_specs=[pl.BlockSpec(memory_space=pl.ANY),
                  pl.BlockSpec((None, d_model, d_exp), lambda e, fb, nb, tt: (e, 0, 0)),
                  pl.BlockSpec((None, d_model, d_exp), lambda e, fb, nb, tt: (e, 0, 0)),
                  pl.BlockSpec((None, d_exp, d_model), lambda e, fb, nb, tt: (e, 0, 0))],
        out_specs=pl.BlockSpec(memory_space=pl.ANY),
        scratch_shapes=[pltpu.VMEM((EXPERT_SLOTS, ROW_BLOCK * sub, V7X_LANES), xs.dtype),
                        pltpu.VMEM((EXPERT_SLOTS, ROW_BLOCK * sub, V7X_LANES), xs.dtype),
                        pltpu.VMEM((d_model, d_exp), BF16),
                        pltpu.VMEM((d_model, d_exp), BF16),
                        pltpu.VMEM((d_exp, d_model), BF16),
                        pltpu.SemaphoreType.DMA((EXPERT_SLOTS,)),
                        pltpu.SemaphoreType.DMA((EXPERT_SLOTS,))],
    )
    return pl.pallas_call(
        functools.partial(_experts_kernel, sub=sub),
        grid_spec=grid_spec,
        out_shape=jax.ShapeDtypeStruct(xs.shape, xs.dtype),
        compiler_params=pltpu.CompilerParams(dimension_semantics=("arbitrary",), vmem_limit_bytes=V7X_VMEM_LIMIT,
                                             has_side_effects=True),
        name="experts",
    )(first_block, n_block, total, xs, w_gate, w_up, w_down)


def _combine_kernel(dest_ref, wt_ref, x_ref, y_hbm, sg_ref, su_ref, sd_ref, g_ref, b_ref,
                    out_ref, ybuf_ref, sem, *, alpha, sub):
    rows = x_ref.shape[0]

    def row_copy(k, r):
        return pltpu.make_async_copy(y_hbm.at[_token_rows(dest_ref[r * TOP_K + k], sub), :],
                                     ybuf_ref.at[k, _token_rows(r, sub), :], sem)

    def issue(g, c):
        for u in range(DMA_UNROLL):
            for k in range(TOP_K):
                row_copy(k, g * DMA_UNROLL + u).start(priority=k % 2)
        return c
    lax.fori_loop(0, rows // DMA_UNROLL, issue, 0)

    x = x_ref[...]
    xb = x.astype(BF16)
    hidden = _silu(_dot(xb, sg_ref[...])) * _dot(xb, su_ref[...])
    acc = alpha * x + _dot(hidden.astype(BF16), sd_ref[...])

    def drain(g, c):
        for u in range(DMA_UNROLL):
            for k in range(TOP_K):
                row_copy(k, g * DMA_UNROLL + u).wait()
        return c
    lax.fori_loop(0, rows // DMA_UNROLL, drain, 0)

    wt = wt_ref[...]
    chunks = [acc[:, c * V7X_LANES:(c + 1) * V7X_LANES] for c in range(2 * sub)]
    for k in range(TOP_K):
        wk = wt[:, k:k + 1]
        for c, p in enumerate(_load_token_rows(ybuf_ref, rows, (k,))):
            hi, lo = _unpack_rows(p)
            chunks[c] = chunks[c] + hi * wk
            chunks[sub + c] = chunks[sub + c] + lo * wk
    out_ref[...] = _layernorm_rows(jnp.concatenate(chunks, axis=1), g_ref[...], b_ref[...])


def _combine(x1, y, dest, w_tok, sw_gate, sw_up, sw_down, ln2_g, ln2_b, alpha, sub):
    tokens, d_model = x1.shape
    d_shared = sw_gate.shape[1]
    rows = COMBINE_ROWS
    assert tokens % rows == 0
    row2 = lambda a: a.reshape(1, -1).astype(F32)
    return pl.pallas_call(
        functools.partial(_combine_kernel, alpha=alpha, sub=sub),
        grid=(tokens // rows,),
        in_specs=[pl.BlockSpec((TOP_K * rows,), lambda i: (i,), memory_space=pltpu.SMEM),
                  pl.BlockSpec((rows, TOP_K), lambda i: (i, 0)),
                  pl.BlockSpec((rows, d_model), lambda i: (i, 0)),
                  pl.BlockSpec(memory_space=pl.ANY),
                  _const_spec((d_model, d_shared)),
                  _const_spec((d_model, d_shared)),
                  _const_spec((d_shared, d_model)),
                  _const_spec((1, d_model)),
                  _const_spec((1, d_model))],
        out_specs=pl.BlockSpec((rows, d_model), lambda i: (i, 0)),
        out_shape=jax.ShapeDtypeStruct((tokens, d_model), F32),
        scratch_shapes=[pltpu.VMEM((TOP_K, rows * sub, V7X_LANES), y.dtype), pltpu.SemaphoreType.DMA(())],
        compiler_params=pltpu.CompilerParams(dimension_semantics=("arbitrary",), vmem_limit_bytes=V7X_VMEM_LIMIT),
        name="combine",
    )(dest, w_tok, x1, y, sw_gate.astype(BF16), sw_up.astype(BF16), sw_down.astype(BF16), row2(ln2_g), row2(ln2_b))


def _block_layout(counts, tokens):
    n_exp = counts.shape[0]
    padded = (counts + ROW_BLOCK - 1) // ROW_BLOCK * ROW_BLOCK
    pad_ends = jnp.cumsum(padded)
    pad_starts = pad_ends - padded
    n_blocks = -(-(tokens * TOP_K + n_exp * (ROW_BLOCK - 1)) // ROW_BLOCK)
    return padded, pad_ends, pad_starts, pad_starts // ROW_BLOCK, padded // ROW_BLOCK, n_blocks * ROW_BLOCK


def kernel(x, w_in, ret_norm_g, gla_gate_w2, gla_gate_b, gla_norm_g, w_ret_out, w_gla_out, w_o, ln1_g, ln1_b, router_w, router_bias, exp_w_gate, exp_w_up, exp_w_down, shared_w_gate, shared_w_up, shared_w_down, ln2_g, ln2_b):
    batch, seq, d_model = x.shape
    depth = w_in.shape[0]
    alpha = (2.0 * depth) ** 0.25
    for l in range(depth):
        x1, x1p = _mix(x, w_in[l], ret_norm_g[l], gla_gate_w2[l], gla_gate_b[l], gla_norm_g[l],
                       w_ret_out[l], w_gla_out[l], w_o[l], ln1_g[l], ln1_b[l], alpha)
        x1 = x1.reshape(batch * seq, d_model)
        sub = x1p.shape[1] // seq
        x1p = x1p.reshape(batch * seq * sub, V7X_LANES)
        idx, w_sel, rank, counts = _route(x1, router_w[l], router_bias[l])
        padded, pad_ends, pad_starts, first_block, n_block, n_rows = _block_layout(
            counts[:, 0].astype(jnp.int32), batch * seq)
        dest = _plan(idx, rank, pad_starts).T.reshape(-1)
        xs = _dispatch(x1p, dest, pad_ends, padded, n_rows, sub)
        y = _experts(xs, first_block, n_block, exp_w_gate[l], exp_w_up[l], exp_w_down[l], sub)
        out = _combine(x1, y, dest, w_sel.T, shared_w_gate[l], shared_w_up[l], shared_w_down[l],
                       ln2_g[l], ln2_b[l], alpha, sub)
        x = out.reshape(batch, seq, d_model)
    return x
```

```python
import functools

import jax
import jax.numpy as jnp
import numpy as np
from jax import lax
from jax.experimental import pallas as pl
from jax.experimental.pallas import tpu as pltpu

CHUNK = 64
RET_HEADS = 4
RET_DK = 128
RET_DV = 256
GLA_HEADS = 4
GLA_DK = 128
GLA_DV = 256
GLA_GATE_RANK = 16
GLA_GATE_TAU = 16.0
ROPE_THETA = 10000.0
N_EXPERTS = 256
TOP_K = 8
N_GROUPS = 8
TOPK_GROUPS = 4
ROUTED_SCALE = 2.5
LN_EPS = 1e-5
NORM_EPS = 1e-6

V7X_LANES = 128
V7X_VMEM_LIMIT = 60 * 1024 * 1024

MIX_ROWS = 256
ROUTE_COLS = 512
ROW_BLOCK = 256
DISPATCH_ROWS = 512
DMA_UNROLL = 2
EXPERT_SLOTS = 4
COMBINE_ROWS = 256

F32 = jnp.float32
BF16 = jnp.bfloat16


def _dot(a, b):
    return jnp.dot(a, b, preferred_element_type=F32)


def _dot_nt(a, b):
    return lax.dot_general(a, b, (((1,), (1,)), ((), ())), preferred_element_type=F32)


def _dot_tn(a, b):
    return lax.dot_general(a, b, (((0,), (0,)), ((), ())), preferred_element_type=F32)


def _sigmoid(v):
    return 1.0 / (1.0 + jnp.exp(-v))


def _silu(v):
    return v * _sigmoid(v)


def _pack_rows(v):
    half = v.shape[1] // 2
    hi = lax.bitcast_convert_type(v[:, :half].astype(BF16).astype(F32), jnp.uint32)
    lo = lax.bitcast_convert_type(v[:, half:].astype(BF16).astype(F32), jnp.uint32)
    return hi | (lo >> 16)


def _unpack_rows(p):
    hi = lax.bitcast_convert_type(p & jnp.uint32(0xFFFF0000), F32)
    lo = lax.bitcast_convert_type(p << 16, F32)
    return hi, lo


def _store_token_rows(ref, packed, lead=()):
    m, width = packed.shape
    sub = width // V7X_LANES
    for c in range(sub):
        ref[lead + (pl.ds(c, m, stride=sub), slice(None))] = packed[:, c * V7X_LANES:(c + 1) * V7X_LANES]


def _load_token_rows(ref, m, lead=()):
    sub = ref.shape[-2] // m
    return [ref[lead + (pl.ds(c, m, stride=sub), slice(None))] for c in range(sub)]


def _layernorm_rows(v, g, b):
    mu = jnp.mean(v, axis=-1, keepdims=True)
    vc = v - mu
    var = jnp.mean(vc * vc, axis=-1, keepdims=True)
    return vc * lax.rsqrt(var + LN_EPS) * g + b


def _mix_kernel(x_ref, wrq_ref, wrk_ref, wrv_ref, wrg_ref, wgq_ref, wgk_ref, wgv_ref, wgg_ref, wga_ref, wmg_ref,
                cos_ref, sin_ref, dmask_ref, qdec_ref, kdec_ref, tri_ref, w2_ref, gb_ref, retg_ref, glag_ref,
                wro_ref, wgo_ref, wo_ref, ln1g_ref, ln1b_ref,
                out_ref, packed_ref, rstate_ref, gstate_ref, yret_ref, ygla_ref, *, block_decay, alpha):
    rows = x_ref.shape[0]

    @pl.when(pl.program_id(1) == 0)
    def _():
        rstate_ref[...] = jnp.zeros_like(rstate_ref)
        gstate_ref[...] = jnp.zeros_like(gstate_ref)

    x = x_ref[...]
    xb = x.astype(BF16)

    def proj(w_ref):
        return _dot(xb, w_ref[...])

    cos = cos_ref[...]
    sin = sin_ref[...]

    def rope(t):
        return t * cos + pltpu.roll(t, RET_DK // 2, 1) * sin

    rq = proj(wrq_ref)
    rk = proj(wrk_ref)
    rv = proj(wrv_ref)
    rg = proj(wrg_ref)
    ret_states = [rstate_ref[h] for h in range(RET_HEADS)]
    for h in range(RET_HEADS):
        qk = slice(h * RET_DK, (h + 1) * RET_DK)
        vv = slice(h * RET_DV, (h + 1) * RET_DV)
        q = rope(rq[:, qk])
        k = rope(rk[:, qk])
        v = rv[:, vv].astype(BF16)
        scores = _dot_nt(q.astype(BF16), k.astype(BF16)) * dmask_ref[h]
        o = _dot(scores.astype(BF16), v)
        state = ret_states[h]
        o = o + _dot((q * qdec_ref[h]).astype(BF16), state.astype(BF16))
        ret_states[h] = state * block_decay[h] + _dot_tn((k * kdec_ref[h]).astype(BF16), v)
        mu = jnp.mean(o, axis=-1, keepdims=True)
        oc = o - mu
        var = jnp.mean(oc * oc, axis=-1, keepdims=True)
        y = oc * lax.rsqrt(var + LN_EPS) * retg_ref[:, vv] * _silu(rg[:, vv])
        yret_ref[:, vv] = y.astype(BF16)
    for h in range(RET_HEADS):
        rstate_ref[h] = ret_states[h]

    gq = proj(wgq_ref) * (GLA_DK ** -0.5)
    gk = proj(wgk_ref)
    gv = proj(wgv_ref)
    gg = proj(wgg_ref)
    ga = proj(wga_ref)
    z = _dot(ga.astype(BF16), w2_ref[...]) + gb_ref[...]
    log_a = (jnp.minimum(z, 0.0) - jnp.log1p(jnp.exp(-jnp.abs(z)))) * (1.0 / GLA_GATE_TAU)
    la_hi = log_a.astype(BF16)
    la_lo = (log_a - la_hi.astype(F32)).astype(BF16)
    tri = tri_ref[...]
    bcum = _dot(tri, la_hi) + _dot(tri, la_lo)
    n_chunks = rows // CHUNK
    gla_states = [gstate_ref[h] for h in range(GLA_HEADS)]
    decays = []
    updates = []
    for c in range(n_chunks):
        rs = slice(c * CHUNK, (c + 1) * CHUNK)
        b_end = bcum[(c + 1) * CHUNK - 1:(c + 1) * CHUNK, :]
        kt = (gk[rs, :] * jnp.exp(b_end - bcum[rs, :])).astype(BF16)
        decays.append(jnp.exp(b_end))
        updates.append([_dot_tn(gv[rs, h * GLA_DV:(h + 1) * GLA_DV].astype(BF16), kt[:, h * GLA_DK:(h + 1) * GLA_DK])
                        for h in range(GLA_HEADS)])
    for c in range(n_chunks):
        rs = slice(c * CHUNK, (c + 1) * CHUNK)
        for h in range(GLA_HEADS):
            qk = slice(h * GLA_DK, (h + 1) * GLA_DK)
            vv = slice(h * GLA_DV, (h + 1) * GLA_DV)
            gla_states[h] = gla_states[h] * decays[c][:, qk] + updates[c][h]
            o = _dot_nt(gq[rs, qk].astype(BF16), gla_states[h].astype(BF16))
            ms = jnp.mean(o * o, axis=-1, keepdims=True)
            y = o * lax.rsqrt(ms + NORM_EPS) * glag_ref[:, vv] * _silu(gg[rs, vv])
            ygla_ref[rs, vv] = y.astype(BF16)
    for h in range(GLA_HEADS):
        gstate_ref[h] = gla_states[h]

    d_model = x.shape[1]
    u_ret = _dot(yret_ref[...], wro_ref[...])
    u_gla = _dot(ygla_ref[...], wgo_ref[...])
    gate = _sigmoid(proj(wmg_ref))
    merged = gate[:, :d_model] * u_ret + gate[:, d_model:] * u_gla
    mix = _dot(merged.astype(BF16), wo_ref[...])
    out = _layernorm_rows(alpha * x + mix, ln1g_ref[...], ln1b_ref[...])
    out_ref[...] = out
    _store_token_rows(packed_ref, _pack_rows(out))


def _mix_tables(seq, rows):
    half = RET_DK // 2
    inv = ROPE_THETA ** (-np.arange(half, dtype=np.float64) / half)
    ang = np.arange(seq, dtype=np.float64)[:, None] * inv[None, :]
    cos2 = np.concatenate([np.cos(ang), np.cos(ang)], axis=1)
    sin2 = np.concatenate([-np.sin(ang), np.sin(ang)], axis=1)
    log_g = np.log1p(-np.exp2(-5.0 - np.arange(RET_HEADS, dtype=np.float64)))
    j = np.arange(rows, dtype=np.float64)
    same_or_earlier_chunk = (j[None, :] // CHUNK) <= (j[:, None] // CHUNK)
    k_scale = RET_DK ** -0.5
    dmask = np.exp(log_g[:, None, None] * np.abs(j[:, None] - j[None, :])) * same_or_earlier_chunk[None] * k_scale
    qdec = np.exp(log_g[:, None] * (j[None, :] + 1.0))
    kdec = np.exp(log_g[:, None] * (rows - 1.0 - j[None, :])) * k_scale
    qdec = np.broadcast_to(qdec[:, :, None], (RET_HEADS, rows, RET_DK))
    kdec = np.broadcast_to(kdec[:, :, None], (RET_HEADS, rows, RET_DK))
    block_decay = tuple(float(v) for v in np.exp(log_g * rows))
    tri = ((j[None, :] <= j[:, None]) & ((j[None, :] // CHUNK) == (j[:, None] // CHUNK)))
    to = lambda a, dt: jnp.asarray(np.ascontiguousarray(a), dtype=dt)
    return (to(cos2, F32), to(sin2, F32), to(dmask, F32), to(qdec, F32), to(kdec, F32), to(tri, BF16)), block_decay


def _const_spec(shape):
    nd = len(shape)
    return pl.BlockSpec(shape, lambda *_: (0,) * nd, pipeline_mode=pl.Buffered(1))


def _mix(x, w_in, ret_norm_g, gla_gate_w2, gla_gate_b, gla_norm_g, w_ret_out, w_gla_out, w_o, ln1_g, ln1_b, alpha):
    batch, seq, d_model = x.shape
    rows = MIX_ROWS
    assert seq % rows == 0 and rows % CHUNK == 0
    ret_qk, ret_v = RET_HEADS * RET_DK, RET_HEADS * RET_DV
    gla_qk, gla_v = GLA_HEADS * GLA_DK, GLA_HEADS * GLA_DV
    splits = (ret_qk, ret_qk, ret_v, ret_v, gla_qk, gla_qk, gla_v, gla_v, GLA_GATE_RANK, 2 * d_model)
    assert w_in.shape == (d_model, sum(splits))
    offs = np.cumsum((0,) + splits)
    parts = [w_in[:, offs[i]:offs[i + 1]].astype(BF16) for i in range(len(splits))]
    parts[8] = jnp.pad(parts[8], ((0, 0), (0, V7X_LANES - GLA_GATE_RANK)))
    w2 = jnp.pad(gla_gate_w2.astype(BF16), ((0, V7X_LANES - GLA_GATE_RANK), (0, 0)))
    (cos2, sin2, dmask, qdec, kdec, tri), block_decay = _mix_tables(seq, rows)
    row2 = lambda a: a.reshape(1, -1).astype(F32)
    consts = [dmask, qdec, kdec, tri, w2, row2(gla_gate_b), row2(ret_norm_g), row2(gla_norm_g),
              w_ret_out.astype(BF16), w_gla_out.astype(BF16), w_o.astype(BF16), row2(ln1_g), row2(ln1_b)]
    sub = d_model // 2 // V7X_LANES
    pos_spec = pl.BlockSpec((rows, RET_DK), lambda b, s: (s, 0))
    in_specs = ([pl.BlockSpec((None, rows, d_model), lambda b, s: (b, s, 0))]
                + [_const_spec(p.shape) for p in parts]
                + [pos_spec, pos_spec]
                + [_const_spec(c.shape) for c in consts])
    return pl.pallas_call(
        functools.partial(_mix_kernel, block_decay=block_decay, alpha=alpha),
        grid=(batch, seq // rows),
        in_specs=in_specs,
        out_specs=[pl.BlockSpec((None, rows, d_model), lambda b, s: (b, s, 0)),
                   pl.BlockSpec((None, rows * sub, V7X_LANES), lambda b, s: (b, s, 0))],
        out_shape=[jax.ShapeDtypeStruct((batch, seq, d_model), F32),
                   jax.ShapeDtypeStruct((batch, seq * sub, V7X_LANES), jnp.uint32)],
        scratch_shapes=[pltpu.VMEM((RET_HEADS, RET_DK, RET_DV), F32),
                        pltpu.VMEM((GLA_HEADS, GLA_DV, GLA_DK), F32),
                        pltpu.VMEM((rows, ret_v), BF16),
                        pltpu.VMEM((rows, gla_v), BF16)],
        compiler_params=pltpu.CompilerParams(dimension_semantics=("arbitrary", "arbitrary"),
                                             vmem_limit_bytes=V7X_VMEM_LIMIT),
        name="mix",
    )(x, *parts, cos2, sin2, *consts)


def _route_kernel(x_ref, rwt_ref, bias_ref, triu_ref, ones_ref,
                  idx_ref, w_ref, rank_ref, counts_ref, carry_ref):
    cols = x_ref.shape[0]
    n_exp = rwt_ref.shape[0]
    per_group = n_exp // N_GROUPS
    neg_inf = -jnp.inf

    @pl.when(pl.program_id(0) == 0)
    def _():
        carry_ref[...] = jnp.zeros_like(carry_ref)

    logits = _dot_nt(rwt_ref[...], x_ref[...].astype(BF16))
    scores = _sigmoid(logits)
    biased = scores + bias_ref[...]

    sub = lax.broadcasted_iota(jnp.int32, (per_group, cols), 0)
    gscore = []
    for g in range(N_GROUPS):
        blk = biased[g * per_group:(g + 1) * per_group, :]
        m1 = jnp.max(blk, axis=0, keepdims=True)
        i1 = jnp.min(jnp.where(blk == m1, sub, per_group), axis=0, keepdims=True)
        m2 = jnp.max(jnp.where(sub == i1, neg_inf, blk), axis=0, keepdims=True)
        gscore.append(m1 + m2)
    masked = []
    for g in range(N_GROUPS):
        ahead = jnp.zeros((1, cols), jnp.int32)
        for o in range(N_GROUPS):
            if o == g:
                continue
            before = (gscore[o] >= gscore[g]) if o < g else (gscore[o] > gscore[g])
            ahead = ahead + before.astype(jnp.int32)
        keep = ahead < TOPK_GROUPS
        blk = biased[g * per_group:(g + 1) * per_group, :]
        masked.append(jnp.where(keep, blk, neg_inf))
    candidates = jnp.concatenate(masked, axis=0)
    cur = candidates

    rowid = lax.broadcasted_iota(jnp.int32, (n_exp, cols), 0)
    picked = []
    weights = []
    for _ in range(TOP_K):
        m = jnp.max(cur, axis=0, keepdims=True)
        ik = jnp.min(jnp.where(cur == m, rowid, n_exp), axis=0, keepdims=True)
        sel = rowid == ik
        weights.append(jnp.sum(jnp.where(sel, scores, 0.0), axis=0, keepdims=True))
        cur = jnp.where(sel, neg_inf, cur)
        picked.append(ik)
    wsum = weights[0]
    for wk in weights[1:]:
        wsum = wsum + wk

    chosen = (cur == neg_inf) & (candidates != neg_inf)
    chosen_b = jnp.where(chosen, 1.0, 0.0).astype(BF16)
    carry = carry_ref[...]
    before = _dot(chosen_b, triu_ref[...]) + jnp.concatenate([carry] * (cols // V7X_LANES), axis=1)
    for k in range(TOP_K):
        sel = rowid == picked[k]
        rank_ref[k:k + 1, :] = jnp.sum(jnp.where(sel, before, 0.0), axis=0, keepdims=True).astype(jnp.int32)
        idx_ref[k:k + 1, :] = picked[k]
        w_ref[k:k + 1, :] = weights[k] / wsum * ROUTED_SCALE
    carry = carry + _dot(chosen_b, ones_ref[...])
    carry_ref[...] = carry
    counts_ref[...] = carry


def _route(x1, router_w, router_bias):
    tokens, d_model = x1.shape
    n_exp = router_w.shape[1]
    cols = ROUTE_COLS
    assert tokens % cols == 0 and n_exp % N_GROUPS == 0
    j = np.arange(cols)
    triu = jnp.asarray((j[:, None] < j[None, :]), dtype=BF16)
    ones = jnp.ones((cols, V7X_LANES), BF16)
    out_row = lambda dt: jax.ShapeDtypeStruct((TOP_K, tokens), dt)
    row_spec = pl.BlockSpec((TOP_K, cols), lambda i: (0, i))
    return pl.pallas_call(
        _route_kernel,
        grid=(tokens // cols,),
        in_specs=[pl.BlockSpec((cols, d_model), lambda i: (i, 0)),
                  _const_spec((n_exp, d_model)),
                  _const_spec((n_exp, 1)),
                  _const_spec((cols, cols)),
                  _const_spec((cols, V7X_LANES))],
        out_specs=[row_spec, row_spec, row_spec, pl.BlockSpec((n_exp, V7X_LANES), lambda i: (0, 0))],
        out_shape=[out_row(jnp.int32), out_row(F32), out_row(jnp.int32),
                   jax.ShapeDtypeStruct((n_exp, V7X_LANES), F32)],
        scratch_shapes=[pltpu.VMEM((n_exp, V7X_LANES), F32)],
        compiler_params=pltpu.CompilerParams(dimension_semantics=("arbitrary",), vmem_limit_bytes=V7X_VMEM_LIMIT),
        name="route",
    )(x1, router_w.T.astype(BF16), router_bias.reshape(n_exp, 1).astype(F32), triu, ones)


def _plan_kernel(idx_ref, rank_ref, pstart_ref, dest_ref):
    n_exp = pstart_ref.shape[0]
    cols = idx_ref.shape[1]
    rowid = lax.broadcasted_iota(jnp.int32, (n_exp, cols), 0)
    pstart = pstart_ref[...]
    for k in range(TOP_K):
        base = jnp.sum(jnp.where(rowid == idx_ref[k:k + 1, :], pstart, 0.0), axis=0, keepdims=True)
        dest_ref[k:k + 1, :] = base.astype(jnp.int32) + rank_ref[k:k + 1, :]


def _plan(idx, rank, pad_starts):
    tokens = idx.shape[1]
    n_exp = pad_starts.shape[0]
    cols = ROUTE_COLS
    row_spec = pl.BlockSpec((TOP_K, cols), lambda i: (0, i))
    return pl.pallas_call(
        _plan_kernel,
        grid=(tokens // cols,),
        in_specs=[row_spec, row_spec, _const_spec((n_exp, 1))],
        out_specs=row_spec,
        out_shape=jax.ShapeDtypeStruct((TOP_K, tokens), jnp.int32),
        compiler_params=pltpu.CompilerParams(dimension_semantics=("arbitrary",)),
        name="plan",
    )(idx, rank, pad_starts.reshape(n_exp, 1).astype(F32))


def _token_rows(tok, sub):
    return pl.ds(pl.multiple_of(tok * sub, sub), sub)


def _dispatch_kernel(pad_end_ref, padded_ref, dest_ref, x_ref, xs_hbm, zero_ref, sem, *, sub):
    rows = x_ref.shape[0] // sub
    n_exp = pad_end_ref.shape[0]

    def zero_copy(e):
        start = pl.multiple_of((pad_end_ref[e] - ROW_BLOCK) * sub, ROW_BLOCK * sub)
        return pltpu.make_async_copy(zero_ref, xs_hbm.at[pl.ds(start, ROW_BLOCK * sub), :], sem)

    @pl.when(pl.program_id(0) == 0)
    def _():
        zero_ref[...] = jnp.zeros_like(zero_ref)

        def issue(e, c):
            @pl.when(padded_ref[e] > 0)
            def _():
                zero_copy(e).start()
            return c
        lax.fori_loop(0, n_exp, issue, 0)

        def drain(e, c):
            @pl.when(padded_ref[e] > 0)
            def _():
                zero_copy(e).wait()
            return c
        lax.fori_loop(0, n_exp, drain, 0)

    def row_copy(k, r):
        return pltpu.make_async_copy(x_ref.at[_token_rows(r, sub), :],
                                     xs_hbm.at[_token_rows(dest_ref[r * TOP_K + k], sub), :], sem)

    def issue(g, c):
        for u in range(DMA_UNROLL):
            for k in range(TOP_K):
                row_copy(k, g * DMA_UNROLL + u).start(priority=k % 2)
        return c
    lax.fori_loop(0, rows // DMA_UNROLL, issue, 0)

    def drain(g, c):
        for u in range(DMA_UNROLL):
            for k in range(TOP_K):
                row_copy(k, g * DMA_UNROLL + u).wait()
        return c
    lax.fori_loop(0, rows // DMA_UNROLL, drain, 0)


def _dispatch(x1p, dest, pad_ends, padded, n_rows, sub):
    tokens = x1p.shape[0] // sub
    rows = DISPATCH_ROWS
    grid_spec = pltpu.PrefetchScalarGridSpec(
        num_scalar_prefetch=2,
        grid=(tokens // rows,),
        in_specs=[pl.BlockSpec((TOP_K * rows,), lambda i, pe, pd: (i,), memory_space=pltpu.SMEM),
                  pl.BlockSpec((rows * sub, V7X_LANES), lambda i, pe, pd: (i, 0))],
        out_specs=pl.BlockSpec(memory_space=pl.ANY),
        scratch_shapes=[pltpu.VMEM((ROW_BLOCK * sub, V7X_LANES), x1p.dtype), pltpu.SemaphoreType.DMA(())],
    )
    return pl.pallas_call(
        functools.partial(_dispatch_kernel, sub=sub),
        grid_spec=grid_spec,
        out_shape=jax.ShapeDtypeStruct((n_rows * sub, V7X_LANES), x1p.dtype),
        compiler_params=pltpu.CompilerParams(dimension_semantics=("arbitrary",), has_side_effects=True),
        name="dispatch",
    )(pad_ends, padded, dest, x1p)


def _experts_kernel(first_ref, nblk_ref, total_ref, xs_hbm, wg_ref, wu_ref, wd_ref, y_hbm,
                    xbuf_ref, ybuf_ref, wg_b, wu_b, wd_b, in_sem, out_sem, *, sub):
    e = pl.program_id(0)
    n = nblk_ref[e]
    first = first_ref[e]
    total = total_ref[0]
    ahead = EXPERT_SLOTS - 1
    block = ROW_BLOCK * sub

    def block_rows(g):
        return pl.ds(pl.multiple_of(g * block, block), block)

    def in_copy(g):
        slot = lax.rem(g, EXPERT_SLOTS)
        return pltpu.make_async_copy(xs_hbm.at[block_rows(g), :], xbuf_ref.at[slot], in_sem.at[slot])

    def out_copy(g):
        slot = lax.rem(g, EXPERT_SLOTS)
        return pltpu.make_async_copy(ybuf_ref.at[slot], y_hbm.at[block_rows(g), :], out_sem.at[slot])

    @pl.when(e == 0)
    def _():
        for g in range(ahead):
            @pl.when(g < total)
            def _():
                in_copy(g).start()

    @pl.when(n > 0)
    def _():
        wg_b[...] = wg_ref[...].astype(BF16)
        wu_b[...] = wu_ref[...].astype(BF16)
        wd_b[...] = wd_ref[...].astype(BF16)

        def body(j, c):
            g = first + j
            slot = lax.rem(g, EXPERT_SLOTS)
            in_copy(g).wait()

            @pl.when(g + ahead < total)
            def _():
                in_copy(g + ahead).start()

            @pl.when(g >= EXPERT_SLOTS)
            def _():
                out_copy(g - EXPERT_SLOTS).wait()

            halves = [_unpack_rows(p) for p in _load_token_rows(xbuf_ref, ROW_BLOCK, (slot,))]
            xb = jnp.concatenate([h for h, _ in halves] + [l for _, l in halves], axis=1).astype(BF16)
            hidden = _silu(_dot(xb, wg_b[...])) * _dot(xb, wu_b[...])
            _store_token_rows(ybuf_ref, _pack_rows(_dot(hidden.astype(BF16), wd_b[...])), (slot,))
            out_copy(g).start()
            return c
        lax.fori_loop(0, n, body, 0)

    @pl.when(e == pl.num_programs(0) - 1)
    def _():
        for d in range(EXPERT_SLOTS):
            @pl.when(total - 1 - d >= 0)
            def _():
                out_copy(total - 1 - d).wait()


def _experts(xs, first_block, n_block, w_gate, w_up, w_down, sub):
    n_exp, d_model, d_exp = w_gate.shape
    assert sub * V7X_LANES * 2 == d_model and xs.shape[0] % (ROW_BLOCK * sub) == 0
    total = jnp.sum(n_block).reshape(1).astype(jnp.int32)
    grid_spec = pltpu.PrefetchScalarGridSpec(
        num_scalar_prefetch=3,
        grid=(n_exp,),
        in_specs=[pl.BlockSpec(memory_space=pl.ANY),
                  pl.BlockSpec((None, d_model, d_exp), lambda e, fb, nb, tt: (e, 0, 0)),
                  pl.BlockSpec((None, d_model, d_exp), lambda e, fb, nb, tt: (e, 0, 0)),
                  pl.BlockSpec((None, d_exp, d_model), lambda e, fb, nb, tt: (e, 0, 0))],
        out_specs=pl.BlockSpec(memory_space=pl.ANY),
        scratch_shapes=[pltpu.VMEM((EXPERT_SLOTS, ROW_BLOCK * sub, V7X_LANES), xs.dtype),
                        pltpu.VMEM((EXPERT_SLOTS, ROW_BLOCK * sub, V7X_LANES), xs.dtype),
                        pltpu.VMEM((d_model, d_exp), BF16),
                        pltpu.VMEM((d_model, d_exp), BF16),
                        pltpu.VMEM((d_exp, d_model), BF16),
                        pltpu.SemaphoreType.DMA((EXPERT_SLOTS,)),
                        pltpu.SemaphoreType.DMA((EXPERT_SLOTS,))],
    )
    return pl.pallas_call(
        functools.partial(_experts_kernel, sub=sub),
        grid_spec=grid_spec,
        out_shape=jax.ShapeDtypeStruct(xs.shape, xs.dtype),
        compiler_params=pltpu.CompilerParams(dimension_semantics=("arbitrary",), vmem_limit_bytes=V7X_VMEM_LIMIT,
                                             has_side_effects=True),
        name="experts",
    )(first_block, n_block, total, xs, w_gate, w_up, w_down)


def _combine_kernel(dest_ref, wt_ref, x_ref, y_hbm, sg_ref, su_ref, sd_ref, g_ref, b_ref,
                    out_ref, ybuf_ref, sem, *, alpha, sub):
    rows = x_ref.shape[0]

    def row_copy(k, r):
        return pltpu.make_async_copy(y_hbm.at[_token_rows(dest_ref[r * TOP_K + k], sub), :],
                                     ybuf_ref.at[k, _token_rows(r, sub), :], sem)

    def issue(g, c):
        for u in range(DMA_UNROLL):
            for k in range(TOP_K):
                row_copy(k, g * DMA_UNROLL + u).start(priority=k % 2)
        return c
    lax.fori_loop(0, rows // DMA_UNROLL, issue, 0)

    x = x_ref[...]
    xb = x.astype(BF16)
    hidden = _silu(_dot(xb, sg_ref[...])) * _dot(xb, su_ref[...])
    acc = alpha * x + _dot(hidden.astype(BF16), sd_ref[...])

    def drain(g, c):
        for u in range(DMA_UNROLL):
            for k in range(TOP_K):
                row_copy(k, g * DMA_UNROLL + u).wait()
        return c
    lax.fori_loop(0, rows // DMA_UNROLL, drain, 0)

    wt = wt_ref[...]
    chunks = [acc[:, c * V7X_LANES:(c + 1) * V7X_LANES] for c in range(2 * sub)]
    for k in range(TOP_K):
        wk = wt[:, k:k + 1]
        for c, p in enumerate(_load_token_rows(ybuf_ref, rows, (k,))):
            hi, lo = _unpack_rows(p)
            chunks[c] = chunks[c] + hi * wk
            chunks[sub + c] = chunks[sub + c] + lo * wk
    out_ref[...] = _layernorm_rows(jnp.concatenate(chunks, axis=1), g_ref[...], b_ref[...])


def _combine(x1, y, dest, w_tok, sw_gate, sw_up, sw_down, ln2_g, ln2_b, alpha, sub):
    tokens, d_model = x1.shape
    d_shared = sw_gate.shape[1]
    rows = COMBINE_ROWS
    assert tokens % rows == 0
    row2 = lambda a: a.reshape(1, -1).astype(F32)
    return pl.pallas_call(
        functools.partial(_combine_kernel, alpha=alpha, sub=sub),
        grid=(tokens // rows,),
        in_specs=[pl.BlockSpec((TOP_K * rows,), lambda i: (i,), memory_space=pltpu.SMEM),
                  pl.BlockSpec((rows, TOP_K), lambda i: (i, 0)),
                  pl.BlockSpec((rows, d_model), lambda i: (i, 0)),
                  pl.BlockSpec(memory_space=pl.ANY),
                  _const_spec((d_model, d_shared)),
                  _const_spec((d_model, d_shared)),
                  _const_spec((d_shared, d_model)),
                  _const_spec((1, d_model)),
                  _const_spec((1, d_model))],
        out_specs=pl.BlockSpec((rows, d_model), lambda i: (i, 0)),
        out_shape=jax.ShapeDtypeStruct((tokens, d_model), F32),
        scratch_shapes=[pltpu.VMEM((TOP_K, rows * sub, V7X_LANES), y.dtype), pltpu.SemaphoreType.DMA(())],
        compiler_params=pltpu.CompilerParams(dimension_semantics=("arbitrary",), vmem_limit_bytes=V7X_VMEM_LIMIT),
        name="combine",
    )(dest, w_tok, x1, y, sw_gate.astype(BF16), sw_up.astype(BF16), sw_down.astype(BF16), row2(ln2_g), row2(ln2_b))


def _block_layout(counts, tokens):
    n_exp = counts.shape[0]
    padded = (counts + ROW_BLOCK - 1) // ROW_BLOCK * ROW_BLOCK
    pad_ends = jnp.cumsum(padded)
    pad_starts = pad_ends - padded
    n_blocks = -(-(tokens * TOP_K + n_exp * (ROW_BLOCK - 1)) // ROW_BLOCK)
    return padded, pad_ends, pad_starts, pad_starts // ROW_BLOCK, padded // ROW_BLOCK, n_blocks * ROW_BLOCK


def kernel(x, w_in, ret_norm_g, gla_gate_w2, gla_gate_b, gla_norm_g, w_ret_out, w_gla_out, w_o, ln1_g, ln1_b, router_w, router_bias, exp_w_gate, exp_w_up, exp_w_down, shared_w_gate, shared_w_up, shared_w_down, ln2_g, ln2_b):
    batch, seq, d_model = x.shape
    depth = w_in.shape[0]
    alpha = (2.0 * depth) ** 0.25
    for l in range(depth):
        x1, x1p = _mix(x, w_in[l], ret_norm_g[l], gla_gate_w2[l], gla_gate_b[l], gla_norm_g[l],
                       w_ret_out[l], w_gla_out[l], w_o[l], ln1_g[l], ln1_b[l], alpha)
        x1 = x1.reshape(batch * seq, d_model)
        sub = x1p.shape[1] // seq
        x1p = x1p.reshape(batch * seq * sub, V7X_LANES)
        idx, w_sel, rank, counts = _route(x1, router_w[l], router_bias[l])
        padded, pad_ends, pad_starts, first_block, n_block, n_rows = _block_layout(
            counts[:, 0].astype(jnp.int32), batch * seq)
        dest = _plan(idx, rank, pad_starts).T.reshape(-1)
        xs = _dispatch(x1p, dest, pad_ends, padded, n_rows, sub)
        y = _experts(xs, first_block, n_block, exp_w_gate[l], exp_w_up[l], exp_w_down[l], sub)
        out = _combine(x1, y, dest, w_sel.T, shared_w_gate[l], shared_w_up[l], shared_w_down[l],
                       ln2_g[l], ln2_b[l], alpha, sub)
        x = out.reshape(batch, seq, d_model)
    return x
```

```python
import functools

import jax
import jax.numpy as jnp
import numpy as np
from jax import lax
from jax.experimental import pallas as pl
from jax.experimental.pallas import tpu as pltpu

CHUNK = 64
RET_HEADS = 4
RET_DK = 128
RET_DV = 256
GLA_HEADS = 4
GLA_DK = 128
GLA_DV = 256
GLA_GATE_RANK = 16
GLA_GATE_TAU = 16.0
ROPE_THETA = 10000.0
N_EXPERTS = 256
TOP_K = 8
N_GROUPS = 8
TOPK_GROUPS = 4
ROUTED_SCALE = 2.5
LN_EPS = 1e-5
NORM_EPS = 1e-6

V7X_LANES = 128
V7X_VMEM_LIMIT = 60 * 1024 * 1024

MIX_ROWS = 256
ROUTE_COLS = 512
ROW_BLOCK = 256
DISPATCH_ROWS = 512
DMA_UNROLL = 2
EXPERT_SLOTS = 4
COMBINE_ROWS = 256

F32 = jnp.float32
BF16 = jnp.bfloat16


def _dot(a, b):
    return jnp.dot(a, b, preferred_element_type=F32)


def _dot_nt(a, b):
    return lax.dot_general(a, b, (((1,), (1,)), ((), ())), preferred_element_type=F32)


def _dot_tn(a, b):
    return lax.dot_general(a, b, (((0,), (0,)), ((), ())), preferred_element_type=F32)


def _sigmoid(v):
    return 1.0 / (1.0 + jnp.exp(-v))


def _silu(v):
    return v * _sigmoid(v)


def _pack_rows(v):
    half = v.shape[1] // 2
    hi = lax.bitcast_convert_type(v[:, :half].astype(BF16).astype(F32), jnp.uint32)
    lo = lax.bitcast_convert_type(v[:, half:].astype(BF16).astype(F32), jnp.uint32)
    return hi | (lo >> 16)


def _unpack_rows(p):
    hi = lax.bitcast_convert_type(p & jnp.uint32(0xFFFF0000), F32)
    lo = lax.bitcast_convert_type(p << 16, F32)
    return hi, lo


def _store_token_rows(ref, packed, lead=()):
    m, width = packed.shape
    sub = width // V7X_LANES
    for c in range(sub):
        ref[lead + (pl.ds(c, m, stride=sub), slice(None))] = packed[:, c * V7X_LANES:(c + 1) * V7X_LANES]


def _load_token_rows(ref, m, lead=()):
    sub = ref.shape[-2] // m
    return [ref[lead + (pl.ds(c, m, stride=sub), slice(None))] for c in range(sub)]


def _layernorm_rows(v, g, b):
    mu = jnp.mean(v, axis=-1, keepdims=True)
    vc = v - mu
    var = jnp.mean(vc * vc, axis=-1, keepdims=True)
    return vc * lax.rsqrt(var + LN_EPS) * g + b


def _mix_kernel(x_ref, wrq_ref, wrk_ref, wrv_ref, wrg_ref, wgq_ref, wgk_ref, wgv_ref, wgg_ref, wga_ref, wmg_ref,
                cos_ref, sin_ref, dmask_ref, qdec_ref, kdec_ref, tri_ref, w2_ref, gb_ref, retg_ref, glag_ref,
                wro_ref, wgo_ref, wo_ref, ln1g_ref, ln1b_ref,
                out_ref, packed_ref, rstate_ref, gstate_ref, yret_ref, ygla_ref, *, block_decay, alpha):
    rows = x_ref.shape[0]

    @pl.when(pl.program_id(1) == 0)
    def _():
        rstate_ref[...] = jnp.zeros_like(rstate_ref)
        gstate_ref[...] = jnp.zeros_like(gstate_ref)

    x = x_ref[...]
    xb = x.astype(BF16)

    def proj(w_ref):
        return _dot(xb, w_ref[...])

    cos = cos_ref[...]
    sin = sin_ref[...]

    def rope(t):
        return t * cos + pltpu.roll(t, RET_DK // 2, 1) * sin

    rq = proj(wrq_ref)
    rk = proj(wrk_ref)
    rv = proj(wrv_ref)
    rg = proj(wrg_ref)
    ret_states = [rstate_ref[h] for h in range(RET_HEADS)]
    for h in range(RET_HEADS):
        qk = slice(h * RET_DK, (h + 1) * RET_DK)
        vv = slice(h * RET_DV, (h + 1) * RET_DV)
        q = rope(rq[:, qk])
        k = rope(rk[:, qk])
        v = rv[:, vv].astype(BF16)
        scores = _dot_nt(q.astype(BF16), k.astype(BF16)) * dmask_ref[h]
        o = _dot(scores.astype(BF16), v)
        state = ret_states[h]
        o = o + _dot((q * qdec_ref[h]).astype(BF16), state.astype(BF16))
        ret_states[h] = state * block_decay[h] + _dot_tn((k * kdec_ref[h]).astype(BF16), v)
        mu = jnp.mean(o, axis=-1, keepdims=True)
        oc = o - mu
        var = jnp.mean(oc * oc, axis=-1, keepdims=True)
        y = oc * lax.rsqrt(var + LN_EPS) * retg_ref[:, vv] * _silu(rg[:, vv])
        yret_ref[:, vv] = y.astype(BF16)
    for h in range(RET_HEADS):
        rstate_ref[h] = ret_states[h]

    gq = proj(wgq_ref) * (GLA_DK ** -0.5)
    gk = proj(wgk_ref)
    gv = proj(wgv_ref)
    gg = proj(wgg_ref)
    ga = proj(wga_ref)
    z = _dot(ga.astype(BF16), w2_ref[...]) + gb_ref[...]
    log_a = (jnp.minimum(z, 0.0) - jnp.log1p(jnp.exp(-jnp.abs(z)))) * (1.0 / GLA_GATE_TAU)
    la_hi = log_a.astype(BF16)
    la_lo = (log_a - la_hi.astype(F32)).astype(BF16)
    tri = tri_ref[...]
    bcum = _dot(tri, la_hi) + _dot(tri, la_lo)
    n_chunks = rows // CHUNK
    gla_states = [gstate_ref[h] for h in range(GLA_HEADS)]
    decays = []
    updates = []
    for c in range(n_chunks):
        rs = slice(c * CHUNK, (c + 1) * CHUNK)
        b_end = bcum[(c + 1) * CHUNK - 1:(c + 1) * CHUNK, :]
        kt = (gk[rs, :] * jnp.exp(b_end - bcum[rs, :])).astype(BF16)
        decays.append(jnp.exp(b_end))
        updates.append([_dot_tn(gv[rs, h * GLA_DV:(h + 1) * GLA_DV].astype(BF16), kt[:, h * GLA_DK:(h + 1) * GLA_DK])
                        for h in range(GLA_HEADS)])
    for c in range(n_chunks):
        rs = slice(c * CHUNK, (c + 1) * CHUNK)
        for h in range(GLA_HEADS):
            qk = slice(h * GLA_DK, (h + 1) * GLA_DK)
            vv = slice(h * GLA_DV, (h + 1) * GLA_DV)
            gla_states[h] = gla_states[h] * decays[c][:, qk] + updates[c][h]
            o = _dot_nt(gq[rs, qk].astype(BF16), gla_states[h].astype(BF16))
            ms = jnp.mean(o * o, axis=-1, keepdims=True)
            y = o * lax.rsqrt(ms + NORM_EPS) * glag_ref[:, vv] * _silu(gg[rs, vv])
            ygla_ref[rs, vv] = y.astype(BF16)
    for h in range(GLA_HEADS):
        gstate_ref[h] = gla_states[h]

    d_model = x.shape[1]
    u_ret = _dot(yret_ref[...], wro_ref[...])
    u_gla = _dot(ygla_ref[...], wgo_ref[...])
    gate = _sigmoid(proj(wmg_ref))
    merged = gate[:, :d_model] * u_ret + gate[:, d_model:] * u_gla
    mix = _dot(merged.astype(BF16), wo_ref[...])
    out = _layernorm_rows(alpha * x + mix, ln1g_ref[...], ln1b_ref[...])
    out_ref[...] = out
    _store_token_rows(packed_ref, _pack_rows(out))


def _mix_tables(seq, rows):
    half = RET_DK // 2
    inv = ROPE_THETA ** (-np.arange(half, dtype=np.float64) / half)
    ang = np.arange(seq, dtype=np.float64)[:, None] * inv[None, :]
    cos2 = np.concatenate([np.cos(ang), np.cos(ang)], axis=1)
    sin2 = np.concatenate([-np.sin(ang), np.sin(ang)], axis=1)
    log_g = np.log1p(-np.exp2(-5.0 - np.arange(RET_HEADS, dtype=np.float64)))
    j = np.arange(rows, dtype=np.float64)
    same_or_earlier_chunk = (j[None, :] // CHUNK) <= (j[:, None] // CHUNK)
    k_scale = RET_DK ** -0.5
    dmask = np.exp(log_g[:, None, None] * np.abs(j[:, None] - j[None, :])) * same_or_earlier_chunk[None] * k_scale
    qdec = np.exp(log_g[:, None] * (j[None, :] + 1.0))
    kdec = np.exp(log_g[:, None] * (rows - 1.0 - j[None, :])) * k_scale
    qdec = np.broadcast_to(qdec[:, :, None], (RET_HEADS, rows, RET_DK))
    kdec = np.broadcast_to(kdec[:, :, None], (RET_HEADS, rows, RET_DK))
    block_decay = tuple(float(v) for v in np.exp(log_g * rows))
    tri = ((j[None, :] <= j[:, None]) & ((j[None, :] // CHUNK) == (j[:, None] // CHUNK)))
    to = lambda a, dt: jnp.asarray(np.ascontiguousarray(a), dtype=dt)
    return (to(cos2, F32), to(sin2, F32), to(dmask, F32), to(qdec, F32), to(kdec, F32), to(tri, BF16)), block_decay


def _const_spec(shape):
    nd = len(shape)
    return pl.BlockSpec(shape, lambda *_: (0,) * nd, pipeline_mode=pl.Buffered(1))


def _mix(x, w_in, ret_norm_g, gla_gate_w2, gla_gate_b, gla_norm_g, w_ret_out, w_gla_out, w_o, ln1_g, ln1_b, alpha):
    batch, seq, d_model = x.shape
    rows = MIX_ROWS
    assert seq % rows == 0 and rows % CHUNK == 0
    ret_qk, ret_v = RET_HEADS * RET_DK, RET_HEADS * RET_DV
    gla_qk, gla_v = GLA_HEADS * GLA_DK, GLA_HEADS * GLA_DV
    splits = (ret_qk, ret_qk, ret_v, ret_v, gla_qk, gla_qk, gla_v, gla_v, GLA_GATE_RANK, 2 * d_model)
    assert w_in.shape == (d_model, sum(splits))
    offs = np.cumsum((0,) + splits)
    parts = [w_in[:, offs[i]:offs[i + 1]].astype(BF16) for i in range(len(splits))]
    parts[8] = jnp.pad(parts[8], ((0, 0), (0, V7X_LANES - GLA_GATE_RANK)))
    w2 = jnp.pad(gla_gate_w2.astype(BF16), ((0, V7X_LANES - GLA_GATE_RANK), (0, 0)))
    (cos2, sin2, dmask, qdec, kdec, tri), block_decay = _mix_tables(seq, rows)
    row2 = lambda a: a.reshape(1, -1).astype(F32)
    consts = [dmask, qdec, kdec, tri, w2, row2(gla_gate_b), row2(ret_norm_g), row2(gla_norm_g),
              w_ret_out.astype(BF16), w_gla_out.astype(BF16), w_o.astype(BF16), row2(ln1_g), row2(ln1_b)]
    sub = d_model // 2 // V7X_LANES
    pos_spec = pl.BlockSpec((rows, RET_DK), lambda b, s: (s, 0))
    in_specs = ([pl.BlockSpec((None, rows, d_model), lambda b, s: (b, s, 0))]
                + [_const_spec(p.shape) for p in parts]
                + [pos_spec, pos_spec]
                + [_const_spec(c.shape) for c in consts])
    return pl.pallas_call(
        functools.partial(_mix_kernel, block_decay=block_decay, alpha=alpha),
        grid=(batch, seq // rows),
        in_specs=in_specs,
        out_specs=[pl.BlockSpec((None, rows, d_model), lambda b, s: (b, s, 0)),
                   pl.BlockSpec((None, rows * sub, V7X_LANES), lambda b, s: (b, s, 0))],
        out_shape=[jax.ShapeDtypeStruct((batch, seq, d_model), F32),
                   jax.ShapeDtypeStruct((batch, seq * sub, V7X_LANES), jnp.uint32)],
        scratch_shapes=[pltpu.VMEM((RET_HEADS, RET_DK, RET_DV), F32),
                        pltpu.VMEM((GLA_HEADS, GLA_DV, GLA_DK), F32),
                        pltpu.VMEM((rows, ret_v), BF16),
                        pltpu.VMEM((rows, gla_v), BF16)],
        compiler_params=pltpu.CompilerParams(dimension_semantics=("arbitrary", "arbitrary"),
                                             vmem_limit_bytes=V7X_VMEM_LIMIT),
        name="mix",
    )(x, *parts, cos2, sin2, *consts)


def _route_kernel(x_ref, rwt_ref, bias_ref, triu_ref, ones_ref,
                  idx_ref, w_ref, rank_ref, counts_ref, carry_ref):
    cols = x_ref.shape[0]
    n_exp = rwt_ref.shape[0]
    per_group = n_exp // N_GROUPS
    neg_inf = -jnp.inf

    @pl.when(pl.program_id(0) == 0)
    def _():
        carry_ref[...] = jnp.zeros_like(carry_ref)

    logits = _dot_nt(rwt_ref[...], x_ref[...].astype(BF16))
    scores = _sigmoid(logits)
    biased = scores + bias_ref[...]

    sub = lax.broadcasted_iota(jnp.int32, (per_group, cols), 0)
    gscore = []
    for g in range(N_GROUPS):
        blk = biased[g * per_group:(g + 1) * per_group, :]
        m1 = jnp.max(blk, axis=0, keepdims=True)
        i1 = jnp.min(jnp.where(blk == m1, sub, per_group), axis=0, keepdims=True)
        m2 = jnp.max(jnp.where(sub == i1, neg_inf, blk), axis=0, keepdims=True)
        gscore.append(m1 + m2)
    masked = []
    for g in range(N_GROUPS):
        ahead = jnp.zeros((1, cols), jnp.int32)
        for o in range(N_GROUPS):
            if o == g:
                continue
            before = (gscore[o] >= gscore[g]) if o < g else (gscore[o] > gscore[g])
            ahead = ahead + before.astype(jnp.int32)
        keep = ahead < TOPK_GROUPS
        blk = biased[g * per_group:(g + 1) * per_group, :]
        masked.append(jnp.where(keep, blk, neg_inf))
    candidates = jnp.concatenate(masked, axis=0)
    cur = candidates

    rowid = lax.broadcasted_iota(jnp.int32, (n_exp, cols), 0)
    picked = []
    weights = []
    for _ in range(TOP_K):
        m = jnp.max(cur, axis=0, keepdims=True)
        ik = jnp.min(jnp.where(cur == m, rowid, n_exp), axis=0, keepdims=True)
        sel = rowid == ik
        weights.append(jnp.sum(jnp.where(sel, scores, 0.0), axis=0, keepdims=True))
        cur = jnp.where(sel, neg_inf, cur)
        picked.append(ik)
    wsum = weights[0]
    for wk in weights[1:]:
        wsum = wsum + wk

    chosen = (cur == neg_inf) & (candidates != neg_inf)
    chosen_b = jnp.where(chosen, 1.0, 0.0).astype(BF16)
    carry = carry_ref[...]
    before = _dot(chosen_b, triu_ref[...]) + jnp.concatenate([carry] * (cols // V7X_LANES), axis=1)
    for k in range(TOP_K):
        sel = rowid == picked[k]
        rank_ref[k:k + 1, :] = jnp.sum(jnp.where(sel, before, 0.0), axis=0, keepdims=True).astype(jnp.int32)
        idx_ref[k:k + 1, :] = picked[k]
        w_ref[k:k + 1, :] = weights[k] / wsum * ROUTED_SCALE
    carry = carry + _dot(chosen_b, ones_ref[...])
    carry_ref[...] = carry
    counts_ref[...] = carry


def _route(x1, router_w, router_bias):
    tokens, d_model = x1.shape
    n_exp = router_w.shape[1]
    cols = ROUTE_COLS
    assert tokens % cols == 0 and n_exp % N_GROUPS == 0
    j = np.arange(cols)
    triu = jnp.asarray((j[:, None] < j[None, :]), dtype=BF16)
    ones = jnp.ones((cols, V7X_LANES), BF16)
    out_row = lambda dt: jax.ShapeDtypeStruct((TOP_K, tokens), dt)
    row_spec = pl.BlockSpec((TOP_K, cols), lambda i: (0, i))
    return pl.pallas_call(
        _route_kernel,
        grid=(tokens // cols,),
        in_specs=[pl.BlockSpec((cols, d_model), lambda i: (i, 0)),
                  _const_spec((n_exp, d_model)),
                  _const_spec((n_exp, 1)),
                  _const_spec((cols, cols)),
                  _const_spec((cols, V7X_LANES))],
        out_specs=[row_spec, row_spec, row_spec, pl.BlockSpec((n_exp, V7X_LANES), lambda i: (0, 0))],
        out_shape=[out_row(jnp.int32), out_row(F32), out_row(jnp.int32),
                   jax.ShapeDtypeStruct((n_exp, V7X_LANES), F32)],
        scratch_shapes=[pltpu.VMEM((n_exp, V7X_LANES), F32)],
        compiler_params=pltpu.CompilerParams(dimension_semantics=("arbitrary",), vmem_limit_bytes=V7X_VMEM_LIMIT),
        name="route",
    )(x1, router_w.T.astype(BF16), router_bias.reshape(n_exp, 1).astype(F32), triu, ones)


def _plan_kernel(idx_ref, rank_ref, pstart_ref, dest_ref):
    n_exp = pstart_ref.shape[0]
    cols = idx_ref.shape[1]
    rowid = lax.broadcasted_iota(jnp.int32, (n_exp, cols), 0)
    pstart = pstart_ref[...]
    for k in range(TOP_K):
        base = jnp.sum(jnp.where(rowid == idx_ref[k:k + 1, :], pstart, 0.0), axis=0, keepdims=True)
        dest_ref[k:k + 1, :] = base.astype(jnp.int32) + rank_ref[k:k + 1, :]


def _plan(idx, rank, pad_starts):
    tokens = idx.shape[1]
    n_exp = pad_starts.shape[0]
    cols = ROUTE_COLS
    row_spec = pl.BlockSpec((TOP_K, cols), lambda i: (0, i))
    return pl.pallas_call(
        _plan_kernel,
        grid=(tokens // cols,),
        in_specs=[row_spec, row_spec, _const_spec((n_exp, 1))],
        out_specs=row_spec,
        out_shape=jax.ShapeDtypeStruct((TOP_K, tokens), jnp.int32),
        compiler_params=pltpu.CompilerParams(dimension_semantics=("arbitrary",)),
        name="plan",
    )(idx, rank, pad_starts.reshape(n_exp, 1).astype(F32))


def _token_rows(tok, sub):
    return pl.ds(pl.multiple_of(tok * sub, sub), sub)


def _dispatch_kernel(pad_end_ref, padded_ref, dest_ref, x_ref, xs_hbm, zero_ref, sem, *, sub):
    rows = x_ref.shape[0] // sub
    n_exp = pad_end_ref.shape[0]

    def zero_copy(e):
        start = pl.multiple_of((pad_end_ref[e] - ROW_BLOCK) * sub, ROW_BLOCK * sub)
        return pltpu.make_async_copy(zero_ref, xs_hbm.at[pl.ds(start, ROW_BLOCK * sub), :], sem)

    @pl.when(pl.program_id(0) == 0)
    def _():
        zero_ref[...] = jnp.zeros_like(zero_ref)

        def issue(e, c):
            @pl.when(padded_ref[e] > 0)
            def _():
                zero_copy(e).start()
            return c
        lax.fori_loop(0, n_exp, issue, 0)

        def drain(e, c):
            @pl.when(padded_ref[e] > 0)
            def _():
                zero_copy(e).wait()
            return c
        lax.fori_loop(0, n_exp, drain, 0)

    def row_copy(k, r):
        return pltpu.make_async_copy(x_ref.at[_token_rows(r, sub), :],
                                     xs_hbm.at[_token_rows(dest_ref[r * TOP_K + k], sub), :], sem)

    def issue(g, c):
        for u in range(DMA_UNROLL):
            for k in range(TOP_K):
                row_copy(k, g * DMA_UNROLL + u).start(priority=k % 2)
        return c
    lax.fori_loop(0, rows // DMA_UNROLL, issue, 0)

    def drain(g, c):
        for u in range(DMA_UNROLL):
            for k in range(TOP_K):
                row_copy(k, g * DMA_UNROLL + u).wait()
        return c
    lax.fori_loop(0, rows // DMA_UNROLL, drain, 0)


def _dispatch(x1p, dest, pad_ends, padded, n_rows, sub):
    tokens = x1p.shape[0] // sub
    rows = DISPATCH_ROWS
    grid_spec = pltpu.PrefetchScalarGridSpec(
        num_scalar_prefetch=2,
        grid=(tokens // rows,),
        in_specs=[pl.BlockSpec((TOP_K * rows,), lambda i, pe, pd: (i,), memory_space=pltpu.SMEM),
                  pl.BlockSpec((rows * sub, V7X_LANES), lambda i, pe, pd: (i, 0))],
        out_specs=pl.BlockSpec(memory_space=pl.ANY),
        scratch_shapes=[pltpu.VMEM((ROW_BLOCK * sub, V7X_LANES), x1p.dtype), pltpu.SemaphoreType.DMA(())],
    )
    return pl.pallas_call(
        functools.partial(_dispatch_kernel, sub=sub),
        grid_spec=grid_spec,
        out_shape=jax.ShapeDtypeStruct((n_rows * sub, V7X_LANES), x1p.dtype),
        compiler_params=pltpu.CompilerParams(dimension_semantics=("arbitrary",), has_side_effects=True),
        name="dispatch",
    )(pad_ends, padded, dest, x1p)


def _experts_kernel(first_ref, nblk_ref, total_ref, xs_hbm, wg_ref, wu_ref, wd_ref, y_hbm,
                    xbuf_ref, ybuf_ref, wg_b, wu_b, wd_b, in_sem, out_sem, *, sub):
    e = pl.program_id(0)
    n = nblk_ref[e]
    first = first_ref[e]
    total = total_ref[0]
    ahead = EXPERT_SLOTS - 1
    block = ROW_BLOCK * sub

    def block_rows(g):
        return pl.ds(pl.multiple_of(g * block, block), block)

    def in_copy(g):
        slot = lax.rem(g, EXPERT_SLOTS)
        return pltpu.make_async_copy(xs_hbm.at[block_rows(g), :], xbuf_ref.at[slot], in_sem.at[slot])

    def out_copy(g):
        slot = lax.rem(g, EXPERT_SLOTS)
        return pltpu.make_async_copy(ybuf_ref.at[slot], y_hbm.at[block_rows(g), :], out_sem.at[slot])

    @pl.when(e == 0)
    def _():
        for g in range(ahead):
            @pl.when(g < total)
            def _():
                in_copy(g).start()

    @pl.when(n > 0)
    def _():
        wg_b[...] = wg_ref[...].astype(BF16)
        wu_b[...] = wu_ref[...].astype(BF16)
        wd_b[...] = wd_ref[...].astype(BF16)

        def body(j, c):
            g = first + j
            slot = lax.rem(g, EXPERT_SLOTS)
            in_copy(g).wait()

            @pl.when(g + ahead < total)
            def _():
                in_copy(g + ahead).start()

            @pl.when(g >= EXPERT_SLOTS)
            def _():
                out_copy(g - EXPERT_SLOTS).wait()

            halves = [_unpack_rows(p) for p in _load_token_rows(xbuf_ref, ROW_BLOCK, (slot,))]
            xb = jnp.concatenate([h for h, _ in halves] + [l for _, l in halves], axis=1).astype(BF16)
            hidden = _silu(_dot(xb, wg_b[...])) * _dot(xb, wu_b[...])
            _store_token_rows(ybuf_ref, _pack_rows(_dot(hidden.astype(BF16), wd_b[...])), (slot,))
            out_copy(g).start()
            return c
        lax.fori_loop(0, n, body, 0)

    @pl.when(e == pl.num_programs(0) - 1)
    def _():
        for d in range(EXPERT_SLOTS):
            @pl.when(total - 1 - d >= 0)
            def _():
                out_copy(total - 1 - d).wait()


def _experts(xs, first_block, n_block, w_gate, w_up, w_down, sub):
    n_exp, d_model, d_exp = w_gate.shape
    assert sub * V7X_LANES * 2 == d_model and xs.shape[0] % (ROW_BLOCK * sub) == 0
    total = jnp.sum(n_block).reshape(1).astype(jnp.int32)
    grid_spec = pltpu.PrefetchScalarGridSpec(
        num_scalar_prefetch=3,
        grid=(n_exp,),
        in_specs=[pl.BlockSpec(memory_space=pl.ANY),
                  pl.BlockSpec((None, d_model, d_exp), lambda e, fb, nb, tt: (e, 0, 0)),
                  pl.BlockSpec((None, d_model, d_exp), lambda e, fb, nb, tt: (e, 0, 0)),
                  pl.BlockSpec((None, d_exp, d_model), lambda e, fb, nb, tt: (e, 0, 0))],
        out_specs=pl.BlockSpec(memory_space=pl.ANY),
        scratch_shapes=[pltpu.VMEM((EXPERT_SLOTS, ROW_BLOCK * sub, V7X_LANES), xs.dtype),
                        pltpu.VMEM((EXPERT_SLOTS, ROW_BLOCK * sub, V7X_LANES), xs.dtype),
                        pltpu.VMEM((d_model, d_exp), BF16),
                        pltpu.VMEM((d_model, d_exp), BF16),
                        pltpu.VMEM((d_exp, d_model), BF16),
                        pltpu.SemaphoreType.DMA((EXPERT_SLOTS,)),
                        pltpu.SemaphoreType.DMA((EXPERT_SLOTS,))],
    )
    return pl.pallas_call(
        functools.partial(_experts_kernel, sub=sub),
        grid_spec=grid_spec,
        out_shape=jax.ShapeDtypeStruct(xs.shape, xs.dtype),
        compiler_params=pltpu.CompilerParams(dimension_semantics=("arbitrary",), vmem_limit_bytes=V7X_VMEM_LIMIT,
                                             has_side_effects=True),
        name="experts",
    )(first_block, n_block, total, xs, w_gate, w_up, w_down)


def _combine_kernel(dest_ref, next_ref, wt_ref, x_ref, y_hbm, sg_ref, su_ref, sd_ref, g_ref, b_ref,
                    out_ref, ybuf_a, ybuf_b, sem_a, sem_b, *, alpha, sub):
    rows = x_ref.shape[0] // 2
    step = pl.program_id(0)

    def row_copy(idx_ref, tok, k, r, buf, sem):
        return pltpu.make_async_copy(y_hbm.at[_token_rows(idx_ref[tok * TOP_K + k], sub), :],
                                     buf.at[k, _token_rows(r, sub), :], sem)

    def issue_rolled(idx_ref, first_tok, buf, sem):
        def body(g, c):
            for u in range(DMA_UNROLL):
                for k in range(TOP_K):
                    r = g * DMA_UNROLL + u
                    row_copy(idx_ref, first_tok + r, k, r, buf, sem).start(priority=k % 2)
            return c
        lax.fori_loop(0, rows // DMA_UNROLL, body, 0)

    def issue_inline(idx_ref, first_tok, buf, sem):
        for r in range(rows):
            for k in range(TOP_K):
                row_copy(idx_ref, first_tok + r, k, r, buf, sem).start(priority=k % 2)

    def drain(buf, sem):
        def body(g, c):
            for u in range(DMA_UNROLL):
                for k in range(TOP_K):
                    r = g * DMA_UNROLL + u
                    pltpu.make_async_copy(y_hbm.at[_token_rows(0, sub), :],
                                          buf.at[k, _token_rows(r, sub), :], sem).wait()
            return c
        lax.fori_loop(0, rows // DMA_UNROLL, body, 0)

    def reduce_block(first_tok, buf):
        rs = slice(first_tok, first_tok + rows)
        x = x_ref[rs, :]
        xb = x.astype(BF16)
        hidden = _silu(_dot(xb, sg_ref[...])) * _dot(xb, su_ref[...])
        acc = alpha * x + _dot(hidden.astype(BF16), sd_ref[...])
        wt = wt_ref[rs, :]
        chunks = [acc[:, c * V7X_LANES:(c + 1) * V7X_LANES] for c in range(2 * sub)]
        for k in range(TOP_K):
            wk = wt[:, k:k + 1]
            for c, p in enumerate(_load_token_rows(buf, rows, (k,))):
                hi, lo = _unpack_rows(p)
                chunks[c] = chunks[c] + hi * wk
                chunks[sub + c] = chunks[sub + c] + lo * wk
        out_ref[rs, :] = _layernorm_rows(jnp.concatenate(chunks, axis=1), g_ref[...], b_ref[...])

    @pl.when(step == 0)
    def _():
        issue_rolled(dest_ref, 0, ybuf_a, sem_a)

    drain(ybuf_a, sem_a)
    issue_inline(dest_ref, rows, ybuf_b, sem_b)
    reduce_block(0, ybuf_a)
    drain(ybuf_b, sem_b)
    issue_inline(next_ref, 0, ybuf_a, sem_a)
    reduce_block(rows, ybuf_b)

    @pl.when(step == pl.num_programs(0) - 1)
    def _():
        drain(ybuf_a, sem_a)


def _combine(x1, y, dest, w_tok, sw_gate, sw_up, sw_down, ln2_g, ln2_b, alpha, sub):
    tokens, d_model = x1.shape
    d_shared = sw_gate.shape[1]
    rows = 2 * COMBINE_ROWS
    assert tokens % rows == 0
    steps = tokens // rows
    row2 = lambda a: a.reshape(1, -1).astype(F32)
    return pl.pallas_call(
        functools.partial(_combine_kernel, alpha=alpha, sub=sub),
        grid=(steps,),
        in_specs=[pl.BlockSpec((TOP_K * rows,), lambda i: (i,), memory_space=pltpu.SMEM),
                  pl.BlockSpec((TOP_K * rows,), lambda i: (jnp.minimum(i + 1, steps - 1),), memory_space=pltpu.SMEM),
                  pl.BlockSpec((rows, TOP_K), lambda i: (i, 0)),
                  pl.BlockSpec((rows, d_model), lambda i: (i, 0)),
                  pl.BlockSpec(memory_space=pl.ANY),
                  _const_spec((d_model, d_shared)),
                  _const_spec((d_model, d_shared)),
                  _const_spec((d_shared, d_model)),
                  _const_spec((1, d_model)),
                  _const_spec((1, d_model))],
        out_specs=pl.BlockSpec((rows, d_model), lambda i: (i, 0)),
        out_shape=jax.ShapeDtypeStruct((tokens, d_model), F32),
        scratch_shapes=[pltpu.VMEM((TOP_K, COMBINE_ROWS * sub, V7X_LANES), y.dtype),
                        pltpu.VMEM((TOP_K, COMBINE_ROWS * sub, V7X_LANES), y.dtype),
                        pltpu.SemaphoreType.DMA(()), pltpu.SemaphoreType.DMA(())],
        compiler_params=pltpu.CompilerParams(dimension_semantics=("arbitrary",), vmem_limit_bytes=V7X_VMEM_LIMIT),
        name="combine",
    )(dest, dest, w_tok, x1, y, sw_gate.astype(BF16), sw_up.astype(BF16), sw_down.astype(BF16),
      row2(ln2_g), row2(ln2_b))


def _block_layout(counts, tokens):
    n_exp = counts.shape[0]
    padded = (counts + ROW_BLOCK - 1) // ROW_BLOCK * ROW_BLOCK
    pad_ends = jnp.cumsum(padded)
    pad_starts = pad_ends - padded
    n_blocks = -(-(tokens * TOP_K + n_exp * (ROW_BLOCK - 1)) // ROW_BLOCK)
    return padded, pad_ends, pad_starts, pad_starts // ROW_BLOCK, padded // ROW_BLOCK, n_blocks * ROW_BLOCK


def kernel(x, w_in, ret_norm_g, gla_gate_w2, gla_gate_b, gla_norm_g, w_ret_out, w_gla_out, w_o, ln1_g, ln1_b, router_w, router_bias, exp_w_gate, exp_w_up, exp_w_down, shared_w_gate, shared_w_up, shared_w_down, ln2_g, ln2_b):
    batch, seq, d_model = x.shape
    depth = w_in.shape[0]
    alpha = (2.0 * depth) ** 0.25
    for l in range(depth):
        x1, x1p = _mix(x, w_in[l], ret_norm_g[l], gla_gate_w2[l], gla_gate_b[l], gla_norm_g[l],
                       w_ret_out[l], w_gla_out[l], w_o[l], ln1_g[l], ln1_b[l], alpha)
        x1 = x1.reshape(batch * seq, d_model)
        sub = x1p.shape[1] // seq
        x1p = x1p.reshape(batch * seq * sub, V7X_LANES)
        idx, w_sel, rank, counts = _route(x1, router_w[l], router_bias[l])
        padded, pad_ends, pad_starts, first_block, n_block, n_rows = _block_layout(
            counts[:, 0].astype(jnp.int32), batch * seq)
        dest = _plan(idx, rank, pad_starts).T.reshape(-1)
        xs = _dispatch(x1p, dest, pad_ends, padded, n_rows, sub)
        y = _experts(xs, first_block, n_block, exp_w_gate[l], exp_w_up[l], exp_w_down[l], sub)
        out = _combine(x1, y, dest, w_sel.T, shared_w_gate[l], shared_w_up[l], shared_w_down[l],
                       ln2_g[l], ln2_b[l], alpha, sub)
        x = out.reshape(batch, seq, d_model)
    return x
```

```python
import functools

import jax
import jax.numpy as jnp
import numpy as np
from jax import lax
from jax.experimental import pallas as pl
from jax.experimental.pallas import tpu as pltpu

CHUNK = 64
RET_HEADS = 4
RET_DK = 128
RET_DV = 256
GLA_HEADS = 4
GLA_DK = 128
GLA_DV = 256
GLA_GATE_RANK = 16
GLA_GATE_TAU = 16.0
ROPE_THETA = 10000.0
N_EXPERTS = 256
TOP_K = 8
N_GROUPS = 8
TOPK_GROUPS = 4
ROUTED_SCALE = 2.5
LN_EPS = 1e-5
NORM_EPS = 1e-6

V7X_LANES = 128
V7X_VMEM_LIMIT = 60 * 1024 * 1024

MIX_ROWS = 256
ROUTE_COLS = 512
ROW_BLOCK = 256
DISPATCH_ROWS = 512
DMA_UNROLL = 2
EXPERT_SLOTS = 6
COMBINE_ROWS = 256

F32 = jnp.float32
BF16 = jnp.bfloat16


def _dot(a, b):
    return jnp.dot(a, b, preferred_element_type=F32)


def _dot_nt(a, b):
    return lax.dot_general(a, b, (((1,), (1,)), ((), ())), preferred_element_type=F32)


def _dot_tn(a, b):
    return lax.dot_general(a, b, (((0,), (0,)), ((), ())), preferred_element_type=F32)


def _sigmoid(v):
    return 1.0 / (1.0 + jnp.exp(-v))


def _silu(v):
    return v * _sigmoid(v)


def _pack_rows(v):
    half = v.shape[1] // 2
    hi = lax.bitcast_convert_type(v[:, :half].astype(BF16).astype(F32), jnp.uint32)
    lo = lax.bitcast_convert_type(v[:, half:].astype(BF16).astype(F32), jnp.uint32)
    return hi | (lo >> 16)


def _unpack_rows(p):
    hi = lax.bitcast_convert_type(p & jnp.uint32(0xFFFF0000), F32)
    lo = lax.bitcast_convert_type(p << 16, F32)
    return hi, lo


def _store_token_rows(ref, packed, lead=()):
    m, width = packed.shape
    sub = width // V7X_LANES
    for c in range(sub):
        ref[lead + (pl.ds(c, m, stride=sub), slice(None))] = packed[:, c * V7X_LANES:(c + 1) * V7X_LANES]


def _load_token_rows(ref, m, lead=()):
    sub = ref.shape[-2] // m
    return [ref[lead + (pl.ds(c, m, stride=sub), slice(None))] for c in range(sub)]


def _layernorm_rows(v, g, b):
    mu = jnp.mean(v, axis=-1, keepdims=True)
    vc = v - mu
    var = jnp.mean(vc * vc, axis=-1, keepdims=True)
    return vc * lax.rsqrt(var + LN_EPS) * g + b


def _mix_kernel(x_ref, wrq_ref, wrk_ref, wrv_ref, wrg_ref, wgq_ref, wgk_ref, wgv_ref, wgg_ref, wga_ref, wmg_ref,
                cos_ref, sin_ref, dmask_ref, qdec_ref, kdec_ref, tri_ref, w2_ref, gb_ref, retg_ref, glag_ref,
                wro_ref, wgo_ref, wo_ref, ln1g_ref, ln1b_ref,
                out_ref, packed_ref, rstate_ref, gstate_ref, yret_ref, ygla_ref, *, block_decay, alpha):
    rows = x_ref.shape[0]

    @pl.when(pl.program_id(1) == 0)
    def _():
        rstate_ref[...] = jnp.zeros_like(rstate_ref)
        gstate_ref[...] = jnp.zeros_like(gstate_ref)

    x = x_ref[...]
    xb = x.astype(BF16)

    def proj(w_ref):
        return _dot(xb, w_ref[...])

    cos = cos_ref[...]
    sin = sin_ref[...]

    def rope(t):
        return t * cos + pltpu.roll(t, RET_DK // 2, 1) * sin

    rq = proj(wrq_ref)
    rk = proj(wrk_ref)
    rv = proj(wrv_ref)
    rg = proj(wrg_ref)
    ret_states = [rstate_ref[h] for h in range(RET_HEADS)]
    for h in range(RET_HEADS):
        qk = slice(h * RET_DK, (h + 1) * RET_DK)
        vv = slice(h * RET_DV, (h + 1) * RET_DV)
        q = rope(rq[:, qk])
        k = rope(rk[:, qk])
        v = rv[:, vv].astype(BF16)
        scores = _dot_nt(q.astype(BF16), k.astype(BF16)) * dmask_ref[h]
        o = _dot(scores.astype(BF16), v)
        state = ret_states[h]
        o = o + _dot((q * qdec_ref[h]).astype(BF16), state.astype(BF16))
        ret_states[h] = state * block_decay[h] + _dot_tn((k * kdec_ref[h]).astype(BF16), v)
        mu = jnp.mean(o, axis=-1, keepdims=True)
        oc = o - mu
        var = jnp.mean(oc * oc, axis=-1, keepdims=True)
        y = oc * lax.rsqrt(var + LN_EPS) * retg_ref[:, vv] * _silu(rg[:, vv])
        yret_ref[:, vv] = y.astype(BF16)
    for h in range(RET_HEADS):
        rstate_ref[h] = ret_states[h]

    gq = proj(wgq_ref) * (GLA_DK ** -0.5)
    gk = proj(wgk_ref)
    gv = proj(wgv_ref)
    gg = proj(wgg_ref)
    ga = proj(wga_ref)
    z = _dot(ga.astype(BF16), w2_ref[...]) + gb_ref[...]
    log_a = (jnp.minimum(z, 0.0) - jnp.log1p(jnp.exp(-jnp.abs(z)))) * (1.0 / GLA_GATE_TAU)
    la_hi = log_a.astype(BF16)
    la_lo = (log_a - la_hi.astype(F32)).astype(BF16)
    tri = tri_ref[...]
    bcum = _dot(tri, la_hi) + _dot(tri, la_lo)
    n_chunks = rows // CHUNK
    gla_states = [gstate_ref[h] for h in range(GLA_HEADS)]
    decays = []
    updates = []
    for c in range(n_chunks):
        rs = slice(c * CHUNK, (c + 1) * CHUNK)
        b_end = bcum[(c + 1) * CHUNK - 1:(c + 1) * CHUNK, :]
        kt = (gk[rs, :] * jnp.exp(b_end - bcum[rs, :])).astype(BF16)
        decays.append(jnp.exp(b_end))
        updates.append([_dot_tn(gv[rs, h * GLA_DV:(h + 1) * GLA_DV].astype(BF16), kt[:, h * GLA_DK:(h + 1) * GLA_DK])
                        for h in range(GLA_HEADS)])
    for c in range(n_chunks):
        rs = slice(c * CHUNK, (c + 1) * CHUNK)
        for h in range(GLA_HEADS):
            qk = slice(h * GLA_DK, (h + 1) * GLA_DK)
            vv = slice(h * GLA_DV, (h + 1) * GLA_DV)
            gla_states[h] = gla_states[h] * decays[c][:, qk] + updates[c][h]
            o = _dot_nt(gq[rs, qk].astype(BF16), gla_states[h].astype(BF16))
            ms = jnp.mean(o * o, axis=-1, keepdims=True)
            y = o * lax.rsqrt(ms + NORM_EPS) * glag_ref[:, vv] * _silu(gg[rs, vv])
            ygla_ref[rs, vv] = y.astype(BF16)
    for h in range(GLA_HEADS):
        gstate_ref[h] = gla_states[h]

    d_model = x.shape[1]
    u_ret = _dot(yret_ref[...], wro_ref[...])
    u_gla = _dot(ygla_ref[...], wgo_ref[...])
    gate = _sigmoid(proj(wmg_ref))
    merged = gate[:, :d_model] * u_ret + gate[:, d_model:] * u_gla
    mix = _dot(merged.astype(BF16), wo_ref[...])
    out = _layernorm_rows(alpha * x + mix, ln1g_ref[...], ln1b_ref[...])
    out_ref[...] = out
    _store_token_rows(packed_ref, _pack_rows(out))


def _mix_tables(seq, rows):
    half = RET_DK // 2
    inv = ROPE_THETA ** (-np.arange(half, dtype=np.float64) / half)
    ang = np.arange(seq, dtype=np.float64)[:, None] * inv[None, :]
    cos2 = np.concatenate([np.cos(ang), np.cos(ang)], axis=1)
    sin2 = np.concatenate([-np.sin(ang), np.sin(ang)], axis=1)
    log_g = np.log1p(-np.exp2(-5.0 - np.arange(RET_HEADS, dtype=np.float64)))
    j = np.arange(rows, dtype=np.float64)
    same_or_earlier_chunk = (j[None, :] // CHUNK) <= (j[:, None] // CHUNK)
    k_scale = RET_DK ** -0.5
    dmask = np.exp(log_g[:, None, None] * np.abs(j[:, None] - j[None, :])) * same_or_earlier_chunk[None] * k_scale
    qdec = np.exp(log_g[:, None] * (j[None, :] + 1.0))
    kdec = np.exp(log_g[:, None] * (rows - 1.0 - j[None, :])) * k_scale
    qdec = np.broadcast_to(qdec[:, :, None], (RET_HEADS, rows, RET_DK))
    kdec = np.broadcast_to(kdec[:, :, None], (RET_HEADS, rows, RET_DK))
    block_decay = tuple(float(v) for v in np.exp(log_g * rows))
    tri = ((j[None, :] <= j[:, None]) & ((j[None, :] // CHUNK) == (j[:, None] // CHUNK)))
    to = lambda a, dt: jnp.asarray(np.ascontiguousarray(a), dtype=dt)
    return (to(cos2, F32), to(sin2, F32), to(dmask, F32), to(qdec, F32), to(kdec, F32), to(tri, BF16)), block_decay


def _const_spec(shape):
    nd = len(shape)
    return pl.BlockSpec(shape, lambda *_: (0,) * nd, pipeline_mode=pl.Buffered(1))


def _mix(x, w_in, ret_norm_g, gla_gate_w2, gla_gate_b, gla_norm_g, w_ret_out, w_gla_out, w_o, ln1_g, ln1_b, alpha):
    batch, seq, d_model = x.shape
    rows = MIX_ROWS
    assert seq % rows == 0 and rows % CHUNK == 0
    ret_qk, ret_v = RET_HEADS * RET_DK, RET_HEADS * RET_DV
    gla_qk, gla_v = GLA_HEADS * GLA_DK, GLA_HEADS * GLA_DV
    splits = (ret_qk, ret_qk, ret_v, ret_v, gla_qk, gla_qk, gla_v, gla_v, GLA_GATE_RANK, 2 * d_model)
    assert w_in.shape == (d_model, sum(splits))
    offs = np.cumsum((0,) + splits)
    parts = [w_in[:, offs[i]:offs[i + 1]].astype(BF16) for i in range(len(splits))]
    parts[8] = jnp.pad(parts[8], ((0, 0), (0, V7X_LANES - GLA_GATE_RANK)))
    w2 = jnp.pad(gla_gate_w2.astype(BF16), ((0, V7X_LANES - GLA_GATE_RANK), (0, 0)))
    (cos2, sin2, dmask, qdec, kdec, tri), block_decay = _mix_tables(seq, rows)
    row2 = lambda a: a.reshape(1, -1).astype(F32)
    consts = [dmask, qdec, kdec, tri, w2, row2(gla_gate_b), row2(ret_norm_g), row2(gla_norm_g),
              w_ret_out.astype(BF16), w_gla_out.astype(BF16), w_o.astype(BF16), row2(ln1_g), row2(ln1_b)]
    sub = d_model // 2 // V7X_LANES
    pos_spec = pl.BlockSpec((rows, RET_DK), lambda b, s: (s, 0))
    in_specs = ([pl.BlockSpec((None, rows, d_model), lambda b, s: (b, s, 0))]
                + [_const_spec(p.shape) for p in parts]
                + [pos_spec, pos_spec]
                + [_const_spec(c.shape) for c in consts])
    return pl.pallas_call(
        functools.partial(_mix_kernel, block_decay=block_decay, alpha=alpha),
        grid=(batch, seq // rows),
        in_specs=in_specs,
        out_specs=[pl.BlockSpec((None, rows, d_model), lambda b, s: (b, s, 0)),
                   pl.BlockSpec((None, rows * sub, V7X_LANES), lambda b, s: (b, s, 0))],
        out_shape=[jax.ShapeDtypeStruct((batch, seq, d_model), F32),
                   jax.ShapeDtypeStruct((batch, seq * sub, V7X_LANES), jnp.uint32)],
        scratch_shapes=[pltpu.VMEM((RET_HEADS, RET_DK, RET_DV), F32),
                        pltpu.VMEM((GLA_HEADS, GLA_DV, GLA_DK), F32),
                        pltpu.VMEM((rows, ret_v), BF16),
                        pltpu.VMEM((rows, gla_v), BF16)],
        compiler_params=pltpu.CompilerParams(dimension_semantics=("arbitrary", "arbitrary"),
                                             vmem_limit_bytes=V7X_VMEM_LIMIT),
        name="mix",
    )(x, *parts, cos2, sin2, *consts)


def _route_kernel(x_ref, rwt_ref, bias_ref, triu_ref, ones_ref,
                  idx_ref, w_ref, rank_ref, counts_ref, carry_ref):
    cols = x_ref.shape[0]
    n_exp = rwt_ref.shape[0]
    per_group = n_exp // N_GROUPS
    neg_inf = -jnp.inf

    @pl.when(pl.program_id(0) == 0)
    def _():
        carry_ref[...] = jnp.zeros_like(carry_ref)

    logits = _dot_nt(rwt_ref[...], x_ref[...].astype(BF16))
    scores = _sigmoid(logits)
    biased = scores + bias_ref[...]

    sub = lax.broadcasted_iota(jnp.int32, (per_group, cols), 0)
    gscore = []
    for g in range(N_GROUPS):
        blk = biased[g * per_group:(g + 1) * per_group, :]
        m1 = jnp.max(blk, axis=0, keepdims=True)
        i1 = jnp.min(jnp.where(blk == m1, sub, per_group), axis=0, keepdims=True)
        m2 = jnp.max(jnp.where(sub == i1, neg_inf, blk), axis=0, keepdims=True)
        gscore.append(m1 + m2)
    masked = []
    for g in range(N_GROUPS):
        ahead = jnp.zeros((1, cols), jnp.int32)
        for o in range(N_GROUPS):
            if o == g:
                continue
            before = (gscore[o] >= gscore[g]) if o < g else (gscore[o] > gscore[g])
            ahead = ahead + before.astype(jnp.int32)
        keep = ahead < TOPK_GROUPS
        blk = biased[g * per_group:(g + 1) * per_group, :]
        masked.append(jnp.where(keep, blk, neg_inf))
    candidates = jnp.concatenate(masked, axis=0)
    cur = candidates

    rowid = lax.broadcasted_iota(jnp.int32, (n_exp, cols), 0)
    picked = []
    weights = []
    for _ in range(TOP_K):
        m = jnp.max(cur, axis=0, keepdims=True)
        ik = jnp.min(jnp.where(cur == m, rowid, n_exp), axis=0, keepdims=True)
        sel = rowid == ik
        weights.append(jnp.sum(jnp.where(sel, scores, 0.0), axis=0, keepdims=True))
        cur = jnp.where(sel, neg_inf, cur)
        picked.append(ik)
    wsum = weights[0]
    for wk in weights[1:]:
        wsum = wsum + wk

    chosen = (cur == neg_inf) & (candidates != neg_inf)
    chosen_b = jnp.where(chosen, 1.0, 0.0).astype(BF16)
    carry = carry_ref[...]
    before = _dot(chosen_b, triu_ref[...]) + jnp.concatenate([carry] * (cols // V7X_LANES), axis=1)
    for k in range(TOP_K):
        sel = rowid == picked[k]
        rank_ref[k:k + 1, :] = jnp.sum(jnp.where(sel, before, 0.0), axis=0, keepdims=True).astype(jnp.int32)
        idx_ref[k:k + 1, :] = picked[k]
        w_ref[k:k + 1, :] = weights[k] / wsum * ROUTED_SCALE
    carry = carry + _dot(chosen_b, ones_ref[...])
    carry_ref[...] = carry
    counts_ref[...] = carry


def _route(x1, router_w, router_bias):
    tokens, d_model = x1.shape
    n_exp = router_w.shape[1]
    cols = ROUTE_COLS
    assert tokens % cols == 0 and n_exp % N_GROUPS == 0
    j = np.arange(cols)
    triu = jnp.asarray((j[:, None] < j[None, :]), dtype=BF16)
    ones = jnp.ones((cols, V7X_LANES), BF16)
    out_row = lambda dt: jax.ShapeDtypeStruct((TOP_K, tokens), dt)
    row_spec = pl.BlockSpec((TOP_K, cols), lambda i: (0, i))
    return pl.pallas_call(
        _route_kernel,
        grid=(tokens // cols,),
        in_specs=[pl.BlockSpec((cols, d_model), lambda i: (i, 0)),
                  _const_spec((n_exp, d_model)),
                  _const_spec((n_exp, 1)),
                  _const_spec((cols, cols)),
                  _const_spec((cols, V7X_LANES))],
        out_specs=[row_spec, row_spec, row_spec, pl.BlockSpec((n_exp, V7X_LANES), lambda i: (0, 0))],
        out_shape=[out_row(jnp.int32), out_row(F32), out_row(jnp.int32),
                   jax.ShapeDtypeStruct((n_exp, V7X_LANES), F32)],
        scratch_shapes=[pltpu.VMEM((n_exp, V7X_LANES), F32)],
        compiler_params=pltpu.CompilerParams(dimension_semantics=("arbitrary",), vmem_limit_bytes=V7X_VMEM_LIMIT),
        name="route",
    )(x1, router_w.T.astype(BF16), router_bias.reshape(n_exp, 1).astype(F32), triu, ones)


def _plan_kernel(idx_ref, rank_ref, pstart_ref, dest_ref):
    n_exp = pstart_ref.shape[0]
    cols = idx_ref.shape[1]
    rowid = lax.broadcasted_iota(jnp.int32, (n_exp, cols), 0)
    pstart = pstart_ref[...]
    for k in range(TOP_K):
        base = jnp.sum(jnp.where(rowid == idx_ref[k:k + 1, :], pstart, 0.0), axis=0, keepdims=True)
        dest_ref[k:k + 1, :] = base.astype(jnp.int32) + rank_ref[k:k + 1, :]


def _plan(idx, rank, pad_starts):
    tokens = idx.shape[1]
    n_exp = pad_starts.shape[0]
    cols = ROUTE_COLS
    row_spec = pl.BlockSpec((TOP_K, cols), lambda i: (0, i))
    return pl.pallas_call(
        _plan_kernel,
        grid=(tokens // cols,),
        in_specs=[row_spec, row_spec, _const_spec((n_exp, 1))],
        out_specs=row_spec,
        out_shape=jax.ShapeDtypeStruct((TOP_K, tokens), jnp.int32),
        compiler_params=pltpu.CompilerParams(dimension_semantics=("arbitrary",)),
        name="plan",
    )(idx, rank, pad_starts.reshape(n_exp, 1).astype(F32))


def _token_rows(tok, sub):
    return pl.ds(pl.multiple_of(tok * sub, sub), sub)


def _dispatch_kernel(pad_end_ref, padded_ref, dest_ref, x_ref, xs_hbm, zero_ref, sem, *, sub):
    rows = x_ref.shape[0] // sub
    n_exp = pad_end_ref.shape[0]

    def zero_copy(e):
        start = pl.multiple_of((pad_end_ref[e] - ROW_BLOCK) * sub, ROW_BLOCK * sub)
        return pltpu.make_async_copy(zero_ref, xs_hbm.at[pl.ds(start, ROW_BLOCK * sub), :], sem)

    @pl.when(pl.program_id(0) == 0)
    def _():
        zero_ref[...] = jnp.zeros_like(zero_ref)

        def issue(e, c):
            @pl.when(padded_ref[e] > 0)
            def _():
                zero_copy(e).start()
            return c
        lax.fori_loop(0, n_exp, issue, 0)

        def drain(e, c):
            @pl.when(padded_ref[e] > 0)
            def _():
                zero_copy(e).wait()
            return c
        lax.fori_loop(0, n_exp, drain, 0)

    def row_copy(k, r):
        return pltpu.make_async_copy(x_ref.at[_token_rows(r, sub), :],
                                     xs_hbm.at[_token_rows(dest_ref[r * TOP_K + k], sub), :], sem)

    def issue(g, c):
        for u in range(DMA_UNROLL):
            for k in range(TOP_K):
                row_copy(k, g * DMA_UNROLL + u).start(priority=k % 2)
        return c
    lax.fori_loop(0, rows // DMA_UNROLL, issue, 0)

    def drain(g, c):
        for u in range(DMA_UNROLL):
            for k in range(TOP_K):
                row_copy(k, g * DMA_UNROLL + u).wait()
        return c
    lax.fori_loop(0, rows // DMA_UNROLL, drain, 0)


def _dispatch(x1p, dest, pad_ends, padded, n_rows, sub):
    tokens = x1p.shape[0] // sub
    rows = DISPATCH_ROWS
    grid_spec = pltpu.PrefetchScalarGridSpec(
        num_scalar_prefetch=2,
        grid=(tokens // rows,),
        in_specs=[pl.BlockSpec((TOP_K * rows,), lambda i, pe, pd: (i,), memory_space=pltpu.SMEM),
                  pl.BlockSpec((rows * sub, V7X_LANES), lambda i, pe, pd: (i, 0))],
        out_specs=pl.BlockSpec(memory_space=pl.ANY),
        scratch_shapes=[pltpu.VMEM((ROW_BLOCK * sub, V7X_LANES), x1p.dtype), pltpu.SemaphoreType.DMA(())],
    )
    return pl.pallas_call(
        functools.partial(_dispatch_kernel, sub=sub),
        grid_spec=grid_spec,
        out_shape=jax.ShapeDtypeStruct((n_rows * sub, V7X_LANES), x1p.dtype),
        compiler_params=pltpu.CompilerParams(dimension_semantics=("arbitrary",), has_side_effects=True),
        name="dispatch",
    )(pad_ends, padded, dest, x1p)


def _experts_kernel(first_ref, nblk_ref, total_ref, xs_hbm, wg_ref, wu_ref, wd_ref, y_hbm,
                    xbuf_ref, ybuf_ref, wg_b, wu_b, wd_b, in_sem, out_sem, *, sub):
    e = pl.program_id(0)
    n = nblk_ref[e]
    first = first_ref[e]
    total = total_ref[0]
    ahead = EXPERT_SLOTS - 2
    block = ROW_BLOCK * sub

    def block_rows(g):
        return pl.ds(pl.multiple_of(g * block, block), block)

    def in_copy(g):
        slot = lax.rem(g, EXPERT_SLOTS)
        return pltpu.make_async_copy(xs_hbm.at[block_rows(g), :], xbuf_ref.at[slot], in_sem.at[slot])

    def out_copy(g):
        slot = lax.rem(g, EXPERT_SLOTS)
        return pltpu.make_async_copy(ybuf_ref.at[slot], y_hbm.at[block_rows(g), :], out_sem.at[slot])

    @pl.when(e == 0)
    def _():
        for g in range(ahead):
            @pl.when(g < total)
            def _():
                in_copy(g).start()

    @pl.when(n > 0)
    def _():
        wg_b[...] = wg_ref[...].astype(BF16)
        wu_b[...] = wu_ref[...].astype(BF16)
        wd_b[...] = wd_ref[...].astype(BF16)

        def stage(g):
            slot = lax.rem(g, EXPERT_SLOTS)
            in_copy(g).wait()

            @pl.when(g + ahead < total)
            def _():
                in_copy(g + ahead).start()

            @pl.when(g >= EXPERT_SLOTS)
            def _():
                out_copy(g - EXPERT_SLOTS).wait()

            halves = [_unpack_rows(p) for p in _load_token_rows(xbuf_ref, ROW_BLOCK, (slot,))]
            return jnp.concatenate([h for h, _ in halves] + [l for _, l in halves], axis=1).astype(BF16)

        def swiglu(xb):
            hidden = _silu(_dot(xb, wg_b[...])) * _dot(xb, wu_b[...])
            return _pack_rows(_dot(hidden.astype(BF16), wd_b[...]))

        def finish(g, packed):
            _store_token_rows(ybuf_ref, packed, (lax.rem(g, EXPERT_SLOTS),))
            out_copy(g).start()

        def pair(jj, c):
            g = first + 2 * jj
            xb0 = stage(g)
            xb1 = stage(g + 1)
            y0 = swiglu(xb0)
            y1 = swiglu(xb1)
            finish(g, y0)
            finish(g + 1, y1)
            return c
        lax.fori_loop(0, n // 2, pair, 0)

        @pl.when(lax.rem(n, 2) == 1)
        def _():
            g = first + n - 1
            finish(g, swiglu(stage(g)))

    @pl.when(e == pl.num_programs(0) - 1)
    def _():
        for d in range(EXPERT_SLOTS):
            @pl.when(total - 1 - d >= 0)
            def _():
                out_copy(total - 1 - d).wait()


def _experts(xs, first_block, n_block, w_gate, w_up, w_down, sub):
    n_exp, d_model, d_exp = w_gate.shape
    assert sub * V7X_LANES * 2 == d_model and xs.shape[0] % (ROW_BLOCK * sub) == 0
    total = jnp.sum(n_block).reshape(1).astype(jnp.int32)
    grid_spec = pltpu.PrefetchScalarGridSpec(
        num_scalar_prefetch=3,
        grid=(n_exp,),
        in_specs=[pl.BlockSpec(memory_space=pl.ANY),
                  pl.BlockSpec((None, d_model, d_exp), lambda e, fb, nb, tt: (e, 0, 0)),
                  pl.BlockSpec((None, d_model, d_exp), lambda e, fb, nb, tt: (e, 0, 0)),
                  pl.BlockSpec((None, d_exp, d_model), lambda e, fb, nb, tt: (e, 0, 0))],
        out_specs=pl.BlockSpec(memory_space=pl.ANY),
        scratch_shapes=[pltpu.VMEM((EXPERT_SLOTS, ROW_BLOCK * sub, V7X_LANES), xs.dtype),
                        pltpu.VMEM((EXPERT_SLOTS, ROW_BLOCK * sub, V7X_LANES), xs.dtype),
                        pltpu.VMEM((d_model, d_exp), BF16),
                        pltpu.VMEM((d_model, d_exp), BF16),
                        pltpu.VMEM((d_exp, d_model), BF16),
                        pltpu.SemaphoreType.DMA((EXPERT_SLOTS,)),
                        pltpu.SemaphoreType.DMA((EXPERT_SLOTS,))],
    )
    return pl.pallas_call(
        functools.partial(_experts_kernel, sub=sub),
        grid_spec=grid_spec,
        out_shape=jax.ShapeDtypeStruct(xs.shape, xs.dtype),
        compiler_params=pltpu.CompilerParams(dimension_semantics=("arbitrary",), vmem_limit_bytes=V7X_VMEM_LIMIT,
                                             has_side_effects=True),
        name="experts",
    )(first_block, n_block, total, xs, w_gate, w_up, w_down)


def _combine_kernel(dest_ref, next_ref, wt_ref, x_ref, y_hbm, sg_ref, su_ref, sd_ref, g_ref, b_ref,
                    out_ref, ybuf_a, ybuf_b, sem_a, sem_b, *, alpha, sub):
    rows = x_ref.shape[0] // 2
    step = pl.program_id(0)

    def row_copy(idx_ref, tok, k, r, buf, sem):
        return pltpu.make_async_copy(y_hbm.at[_token_rows(idx_ref[tok * TOP_K + k], sub), :],
                                     buf.at[k, _token_rows(r, sub), :], sem)

    def issue_rolled(idx_ref, first_tok, buf, sem):
        def body(g, c):
            for u in range(DMA_UNROLL):
                for k in range(TOP_K):
                    r = g * DMA_UNROLL + u
                    row_copy(idx_ref, first_tok + r, k, r, buf, sem).start(priority=k % 2)
            return c
        lax.fori_loop(0, rows // DMA_UNROLL, body, 0)

    def issue_inline(idx_ref, first_tok, buf, sem):
        for r in range(rows):
            for k in range(TOP_K):
                row_copy(idx_ref, first_tok + r, k, r, buf, sem).start(priority=k % 2)

    def drain(buf, sem):
        def body(g, c):
            for u in range(DMA_UNROLL):
                for k in range(TOP_K):
                    r = g * DMA_UNROLL + u
                    pltpu.make_async_copy(y_hbm.at[_token_rows(0, sub), :],
                                          buf.at[k, _token_rows(r, sub), :], sem).wait()
            return c
        lax.fori_loop(0, rows // DMA_UNROLL, body, 0)

    def reduce_block(first_tok, buf):
        rs = slice(first_tok, first_tok + rows)
        x = x_ref[rs, :]
        xb = x.astype(BF16)
        hidden = _silu(_dot(xb, sg_ref[...])) * _dot(xb, su_ref[...])
        acc = alpha * x + _dot(hidden.astype(BF16), sd_ref[...])
        wt = wt_ref[rs, :]
        chunks = [acc[:, c * V7X_LANES:(c + 1) * V7X_LANES] for c in range(2 * sub)]
        for k in range(TOP_K):
            wk = wt[:, k:k + 1]
            for c, p in enumerate(_load_token_rows(buf, rows, (k,))):
                hi, lo = _unpack_rows(p)
                chunks[c] = chunks[c] + hi * wk
                chunks[sub + c] = chunks[sub + c] + lo * wk
        out_ref[rs, :] = _layernorm_rows(jnp.concatenate(chunks, axis=1), g_ref[...], b_ref[...])

    @pl.when(step == 0)
    def _():
        issue_rolled(dest_ref, 0, ybuf_a, sem_a)

    drain(ybuf_a, sem_a)
    issue_inline(dest_ref, rows, ybuf_b, sem_b)
    reduce_block(0, ybuf_a)
    drain(ybuf_b, sem_b)
    issue_inline(next_ref, 0, ybuf_a, sem_a)
    reduce_block(rows, ybuf_b)

    @pl.when(step == pl.num_programs(0) - 1)
    def _():
        drain(ybuf_a, sem_a)


def _combine(x1, y, dest, w_tok, sw_gate, sw_up, sw_down, ln2_g, ln2_b, alpha, sub):
    tokens, d_model = x1.shape
    d_shared = sw_gate.shape[1]
    rows = 2 * COMBINE_ROWS
    assert tokens % rows == 0
    steps = tokens // rows
    row2 = lambda a: a.reshape(1, -1).astype(F32)
    return pl.pallas_call(
        functools.partial(_combine_kernel, alpha=alpha, sub=sub),
        grid=(steps,),
        in_specs=[pl.BlockSpec((TOP_K * rows,), lambda i: (i,), memory_space=pltpu.SMEM),
                  pl.BlockSpec((TOP_K * rows,), lambda i: (jnp.minimum(i + 1, steps - 1),), memory_space=pltpu.SMEM),
                  pl.BlockSpec((rows, TOP_K), lambda i: (i, 0)),
                  pl.BlockSpec((rows, d_model), lambda i: (i, 0)),
                  pl.BlockSpec(memory_space=pl.ANY),
                  _const_spec((d_model, d_shared)),
                  _const_spec((d_model, d_shared)),
                  _const_spec((d_shared, d_model)),
                  _const_spec((1, d_model)),
                  _const_spec((1, d_model))],
        out_specs=pl.BlockSpec((rows, d_model), lambda i: (i, 0)),
        out_shape=jax.ShapeDtypeStruct((tokens, d_model), F32),
        scratch_shapes=[pltpu.VMEM((TOP_K, COMBINE_ROWS * sub, V7X_LANES), y.dtype),
                        pltpu.VMEM((TOP_K, COMBINE_ROWS * sub, V7X_LANES), y.dtype),
                        pltpu.SemaphoreType.DMA(()), pltpu.SemaphoreType.DMA(())],
        compiler_params=pltpu.CompilerParams(dimension_semantics=("arbitrary",), vmem_limit_bytes=V7X_VMEM_LIMIT),
        name="combine",
    )(dest, dest, w_tok, x1, y, sw_gate.astype(BF16), sw_up.astype(BF16), sw_down.astype(BF16),
      row2(ln2_g), row2(ln2_b))


def _block_layout(counts, tokens):
    n_exp = counts.shape[0]
    padded = (counts + ROW_BLOCK - 1) // ROW_BLOCK * ROW_BLOCK
    pad_ends = jnp.cumsum(padded)
    pad_starts = pad_ends - padded
    n_blocks = -(-(tokens * TOP_K + n_exp * (ROW_BLOCK - 1)) // ROW_BLOCK)
    return padded, pad_ends, pad_starts, pad_starts // ROW_BLOCK, padded // ROW_BLOCK, n_blocks * ROW_BLOCK


def kernel(x, w_in, ret_norm_g, gla_gate_w2, gla_gate_b, gla_norm_g, w_ret_out, w_gla_out, w_o, ln1_g, ln1_b, router_w, router_bias, exp_w_gate, exp_w_up, exp_w_down, shared_w_gate, shared_w_up, shared_w_down, ln2_g, ln2_b):
    batch, seq, d_model = x.shape
    depth = w_in.shape[0]
    alpha = (2.0 * depth) ** 0.25
    for l in range(depth):
        x1, x1p = _mix(x, w_in[l], ret_norm_g[l], gla_gate_w2[l], gla_gate_b[l], gla_norm_g[l],
                       w_ret_out[l], w_gla_out[l], w_o[l], ln1_g[l], ln1_b[l], alpha)
        x1 = x1.reshape(batch * seq, d_model)
        sub = x1p.shape[1] // seq
        x1p = x1p.reshape(batch * seq * sub, V7X_LANES)
        idx, w_sel, rank, counts = _route(x1, router_w[l], router_bias[l])
        padded, pad_ends, pad_starts, first_block, n_block, n_rows = _block_layout(
            counts[:, 0].astype(jnp.int32), batch * seq)
        dest = _plan(idx, rank, pad_starts).T.reshape(-1)
        xs = _dispatch(x1p, dest, pad_ends, padded, n_rows, sub)
        y = _experts(xs, first_block, n_block, exp_w_gate[l], exp_w_up[l], exp_w_down[l], sub)
        out = _combine(x1, y, dest, w_sel.T, shared_w_gate[l], shared_w_up[l], shared_w_down[l],
                       ln2_g[l], ln2_b[l], alpha, sub)
        x = out.reshape(batch, seq, d_model)
    return x
```

```python
import functools

import jax
import jax.numpy as jnp
import numpy as np
from jax import lax
from jax.experimental import pallas as pl
from jax.experimental.pallas import tpu as pltpu

CHUNK = 64
RET_HEADS = 4
RET_DK = 128
RET_DV = 256
GLA_HEADS = 4
GLA_DK = 128
GLA_DV = 256
GLA_GATE_RANK = 16
GLA_GATE_TAU = 16.0
ROPE_THETA = 10000.0
N_EXPERTS = 256
TOP_K = 8
N_GROUPS = 8
TOPK_GROUPS = 4
ROUTED_SCALE = 2.5
LN_EPS = 1e-5
NORM_EPS = 1e-6

V7X_LANES = 128
V7X_VMEM_LIMIT = 60 * 1024 * 1024

MIX_ROWS = 256
ROUTE_COLS = 512
ROW_BLOCK = 256
DISPATCH_ROWS = 512
DMA_UNROLL = 2
EXPERT_GROUP = 4
EXPERT_AHEAD = 4
EXPERT_SLOTS = EXPERT_GROUP + EXPERT_AHEAD
COMBINE_ROWS = 256

F32 = jnp.float32
BF16 = jnp.bfloat16


def _dot(a, b):
    return jnp.dot(a, b, preferred_element_type=F32)


def _dot_nt(a, b):
    return lax.dot_general(a, b, (((1,), (1,)), ((), ())), preferred_element_type=F32)


def _dot_tn(a, b):
    return lax.dot_general(a, b, (((0,), (0,)), ((), ())), preferred_element_type=F32)


def _sigmoid(v):
    return 1.0 / (1.0 + jnp.exp(-v))


def _silu(v):
    return v * _sigmoid(v)


def _pack_rows(v):
    half = v.shape[1] // 2
    hi = lax.bitcast_convert_type(v[:, :half].astype(BF16).astype(F32), jnp.uint32)
    lo = lax.bitcast_convert_type(v[:, half:].astype(BF16).astype(F32), jnp.uint32)
    return hi | (lo >> 16)


def _unpack_rows(p):
    hi = lax.bitcast_convert_type(p & jnp.uint32(0xFFFF0000), F32)
    lo = lax.bitcast_convert_type(p << 16, F32)
    return hi, lo


def _store_token_rows(ref, packed, lead=()):
    m, width = packed.shape
    sub = width // V7X_LANES
    for c in range(sub):
        ref[lead + (pl.ds(c, m, stride=sub), slice(None))] = packed[:, c * V7X_LANES:(c + 1) * V7X_LANES]


def _load_token_rows(ref, m, lead=()):
    sub = ref.shape[-2] // m
    return [ref[lead + (pl.ds(c, m, stride=sub), slice(None))] for c in range(sub)]


def _layernorm_rows(v, g, b):
    mu = jnp.mean(v, axis=-1, keepdims=True)
    vc = v - mu
    var = jnp.mean(vc * vc, axis=-1, keepdims=True)
    return vc * lax.rsqrt(var + LN_EPS) * g + b


def _mix_kernel(x_ref, wrq_ref, wrk_ref, wrv_ref, wrg_ref, wgq_ref, wgk_ref, wgv_ref, wgg_ref, wga_ref, wmg_ref,
                cos_ref, sin_ref, dmask_ref, qdec_ref, kdec_ref, tri_ref, w2_ref, gb_ref, retg_ref, glag_ref,
                wro_ref, wgo_ref, wo_ref, ln1g_ref, ln1b_ref,
                out_ref, packed_ref, rstate_ref, gstate_ref, yret_ref, ygla_ref, *, block_decay, alpha):
    rows = x_ref.shape[0]

    @pl.when(pl.program_id(1) == 0)
    def _():
        rstate_ref[...] = jnp.zeros_like(rstate_ref)
        gstate_ref[...] = jnp.zeros_like(gstate_ref)

    x = x_ref[...]
    xb = x.astype(BF16)

    def proj(w_ref):
        return _dot(xb, w_ref[...])

    cos = cos_ref[...]
    sin = sin_ref[...]

    def rope(t):
        return t * cos + pltpu.roll(t, RET_DK // 2, 1) * sin

    rq = proj(wrq_ref)
    rk = proj(wrk_ref)
    rv = proj(wrv_ref)
    rg = proj(wrg_ref)
    ret_states = [rstate_ref[h] for h in range(RET_HEADS)]
    for h in range(RET_HEADS):
        qk = slice(h * RET_DK, (h + 1) * RET_DK)
        vv = slice(h * RET_DV, (h + 1) * RET_DV)
        q = rope(rq[:, qk])
        k = rope(rk[:, qk])
        v = rv[:, vv].astype(BF16)
        scores = _dot_nt(q.astype(BF16), k.astype(BF16)) * dmask_ref[h]
        o = _dot(scores.astype(BF16), v)
        state = ret_states[h]
        o = o + _dot((q * qdec_ref[h]).astype(BF16), state.astype(BF16))
        ret_states[h] = state * block_decay[h] + _dot_tn((k * kdec_ref[h]).astype(BF16), v)
        mu = jnp.mean(o, axis=-1, keepdims=True)
        oc = o - mu
        var = jnp.mean(oc * oc, axis=-1, keepdims=True)
        y = oc * lax.rsqrt(var + LN_EPS) * retg_ref[:, vv] * _silu(rg[:, vv])
        yret_ref[:, vv] = y.astype(BF16)
    for h in range(RET_HEADS):
        rstate_ref[h] = ret_states[h]

    gq = proj(wgq_ref) * (GLA_DK ** -0.5)
    gk = proj(wgk_ref)
    gv = proj(wgv_ref)
    gg = proj(wgg_ref)
    ga = proj(wga_ref)
    z = _dot(ga.astype(BF16), w2_ref[...]) + gb_ref[...]
    log_a = (jnp.minimum(z, 0.0) - jnp.log1p(jnp.exp(-jnp.abs(z)))) * (1.0 / GLA_GATE_TAU)
    la_hi = log_a.astype(BF16)
    la_lo = (log_a - la_hi.astype(F32)).astype(BF16)
    tri = tri_ref[...]
    bcum = _dot(tri, la_hi) + _dot(tri, la_lo)
    n_chunks = rows // CHUNK
    gla_states = [gstate_ref[h] for h in range(GLA_HEADS)]
    decays = []
    updates = []
    for c in range(n_chunks):
        rs = slice(c * CHUNK, (c + 1) * CHUNK)
        b_end = bcum[(c + 1) * CHUNK - 1:(c + 1) * CHUNK, :]
        kt = (gk[rs, :] * jnp.exp(b_end - bcum[rs, :])).astype(BF16)
        decays.append(jnp.exp(b_end))
        updates.append([_dot_tn(gv[rs, h * GLA_DV:(h + 1) * GLA_DV].astype(BF16), kt[:, h * GLA_DK:(h + 1) * GLA_DK])
                        for h in range(GLA_HEADS)])
    for c in range(n_chunks):
        rs = slice(c * CHUNK, (c + 1) * CHUNK)
        for h in range(GLA_HEADS):
            qk = slice(h * GLA_DK, (h + 1) * GLA_DK)
            vv = slice(h * GLA_DV, (h + 1) * GLA_DV)
            gla_states[h] = gla_states[h] * decays[c][:, qk] + updates[c][h]
            o = _dot_nt(gq[rs, qk].astype(BF16), gla_states[h].astype(BF16))
            ms = jnp.mean(o * o, axis=-1, keepdims=True)
            y = o * lax.rsqrt(ms + NORM_EPS) * glag_ref[:, vv] * _silu(gg[rs, vv])
            ygla_ref[rs, vv] = y.astype(BF16)
    for h in range(GLA_HEADS):
        gstate_ref[h] = gla_states[h]

    d_model = x.shape[1]
    u_ret = _dot(yret_ref[...], wro_ref[...])
    u_gla = _dot(ygla_ref[...], wgo_ref[...])
    gate = _sigmoid(proj(wmg_ref))
    merged = gate[:, :d_model] * u_ret + gate[:, d_model:] * u_gla
    mix = _dot(merged.astype(BF16), wo_ref[...])
    out = _layernorm_rows(alpha * x + mix, ln1g_ref[...], ln1b_ref[...])
    out_ref[...] = out
    _store_token_rows(packed_ref, _pack_rows(out))


def _mix_tables(seq, rows):
    half = RET_DK // 2
    inv = ROPE_THETA ** (-np.arange(half, dtype=np.float64) / half)
    ang = np.arange(seq, dtype=np.float64)[:, None] * inv[None, :]
    cos2 = np.concatenate([np.cos(ang), np.cos(ang)], axis=1)
    sin2 = np.concatenate([-np.sin(ang), np.sin(ang)], axis=1)
    log_g = np.log1p(-np.exp2(-5.0 - np.arange(RET_HEADS, dtype=np.float64)))
    j = np.arange(rows, dtype=np.float64)
    same_or_earlier_chunk = (j[None, :] // CHUNK) <= (j[:, None] // CHUNK)
    k_scale = RET_DK ** -0.5
    dmask = np.exp(log_g[:, None, None] * np.abs(j[:, None] - j[None, :])) * same_or_earlier_chunk[None] * k_scale
    qdec = np.exp(log_g[:, None] * (j[None, :] + 1.0))
    kdec = np.exp(log_g[:, None] * (rows - 1.0 - j[None, :])) * k_scale
    qdec = np.broadcast_to(qdec[:, :, None], (RET_HEADS, rows, RET_DK))
    kdec = np.broadcast_to(kdec[:, :, None], (RET_HEADS, rows, RET_DK))
    block_decay = tuple(float(v) for v in np.exp(log_g * rows))
    tri = ((j[None, :] <= j[:, None]) & ((j[None, :] // CHUNK) == (j[:, None] // CHUNK)))
    to = lambda a, dt: jnp.asarray(np.ascontiguousarray(a), dtype=dt)
    return (to(cos2, F32), to(sin2, F32), to(dmask, F32), to(qdec, F32), to(kdec, F32), to(tri, BF16)), block_decay


def _const_spec(shape):
    nd = len(shape)
    return pl.BlockSpec(shape, lambda *_: (0,) * nd, pipeline_mode=pl.Buffered(1))


def _mix(x, w_in, ret_norm_g, gla_gate_w2, gla_gate_b, gla_norm_g, w_ret_out, w_gla_out, w_o, ln1_g, ln1_b, alpha):
    batch, seq, d_model = x.shape
    rows = MIX_ROWS
    assert seq % rows == 0 and rows % CHUNK == 0
    ret_qk, ret_v = RET_HEADS * RET_DK, RET_HEADS * RET_DV
    gla_qk, gla_v = GLA_HEADS * GLA_DK, GLA_HEADS * GLA_DV
    splits = (ret_qk, ret_qk, ret_v, ret_v, gla_qk, gla_qk, gla_v, gla_v, GLA_GATE_RANK, 2 * d_model)
    assert w_in.shape == (d_model, sum(splits))
    offs = np.cumsum((0,) + splits)
    parts = [w_in[:, offs[i]:offs[i + 1]].astype(BF16) for i in range(len(splits))]
    parts[8] = jnp.pad(parts[8], ((0, 0), (0, V7X_LANES - GLA_GATE_RANK)))
    w2 = jnp.pad(gla_gate_w2.astype(BF16), ((0, V7X_LANES - GLA_GATE_RANK), (0, 0)))
    (cos2, sin2, dmask, qdec, kdec, tri), block_decay = _mix_tables(seq, rows)
    row2 = lambda a: a.reshape(1, -1).astype(F32)
    consts = [dmask, qdec, kdec, tri, w2, row2(gla_gate_b), row2(ret_norm_g), row2(gla_norm_g),
              w_ret_out.astype(BF16), w_gla_out.astype(BF16), w_o.astype(BF16), row2(ln1_g), row2(ln1_b)]
    sub = d_model // 2 // V7X_LANES
    pos_spec = pl.BlockSpec((rows, RET_DK), lambda b, s: (s, 0))
    in_specs = ([pl.BlockSpec((None, rows, d_model), lambda b, s: (b, s, 0))]
                + [_const_spec(p.shape) for p in parts]
                + [pos_spec, pos_spec]
                + [_const_spec(c.shape) for c in consts])
    return pl.pallas_call(
        functools.partial(_mix_kernel, block_decay=block_decay, alpha=alpha),
        grid=(batch, seq // rows),
        in_specs=in_specs,
        out_specs=[pl.BlockSpec((None, rows, d_model), lambda b, s: (b, s, 0)),
                   pl.BlockSpec((None, rows * sub, V7X_LANES), lambda b, s: (b, s, 0))],
        out_shape=[jax.ShapeDtypeStruct((batch, seq, d_model), F32),
                   jax.ShapeDtypeStruct((batch, seq * sub, V7X_LANES), jnp.uint32)],
        scratch_shapes=[pltpu.VMEM((RET_HEADS, RET_DK, RET_DV), F32),
                        pltpu.VMEM((GLA_HEADS, GLA_DV, GLA_DK), F32),
                        pltpu.VMEM((rows, ret_v), BF16),
                        pltpu.VMEM((rows, gla_v), BF16)],
        compiler_params=pltpu.CompilerParams(dimension_semantics=("arbitrary", "arbitrary"),
                                             vmem_limit_bytes=V7X_VMEM_LIMIT),
        name="mix",
    )(x, *parts, cos2, sin2, *consts)


def _route_kernel(x_ref, rwt_ref, bias_ref, triu_ref, ones_ref,
                  idx_ref, w_ref, rank_ref, counts_ref, carry_ref):
    cols = x_ref.shape[0]
    n_exp = rwt_ref.shape[0]
    per_group = n_exp // N_GROUPS
    neg_inf = -jnp.inf

    @pl.when(pl.program_id(0) == 0)
    def _():
        carry_ref[...] = jnp.zeros_like(carry_ref)

    logits = _dot_nt(rwt_ref[...], x_ref[...].astype(BF16))
    scores = _sigmoid(logits)
    biased = scores + bias_ref[...]

    sub = lax.broadcasted_iota(jnp.int32, (per_group, cols), 0)
    gscore = []
    for g in range(N_GROUPS):
        blk = biased[g * per_group:(g + 1) * per_group, :]
        m1 = jnp.max(blk, axis=0, keepdims=True)
        i1 = jnp.min(jnp.where(blk == m1, sub, per_group), axis=0, keepdims=True)
        m2 = jnp.max(jnp.where(sub == i1, neg_inf, blk), axis=0, keepdims=True)
        gscore.append(m1 + m2)
    masked = []
    for g in range(N_GROUPS):
        ahead = jnp.zeros((1, cols), jnp.int32)
        for o in range(N_GROUPS):
            if o == g:
                continue
            before = (gscore[o] >= gscore[g]) if o < g else (gscore[o] > gscore[g])
            ahead = ahead + before.astype(jnp.int32)
        keep = ahead < TOPK_GROUPS
        blk = biased[g * per_group:(g + 1) * per_group, :]
        masked.append(jnp.where(keep, blk, neg_inf))
    candidates = jnp.concatenate(masked, axis=0)
    cur = candidates

    rowid = lax.broadcasted_iota(jnp.int32, (n_exp, cols), 0)
    picked = []
    weights = []
    for _ in range(TOP_K):
        m = jnp.max(cur, axis=0, keepdims=True)
        ik = jnp.min(jnp.where(cur == m, rowid, n_exp), axis=0, keepdims=True)
        sel = rowid == ik
        weights.append(jnp.sum(jnp.where(sel, scores, 0.0), axis=0, keepdims=True))
        cur = jnp.where(sel, neg_inf, cur)
        picked.append(ik)
    wsum = weights[0]
    for wk in weights[1:]:
        wsum = wsum + wk

    chosen = (cur == neg_inf) & (candidates != neg_inf)
    chosen_b = jnp.where(chosen, 1.0, 0.0).astype(BF16)
    carry = carry_ref[...]
    before = _dot(chosen_b, triu_ref[...]) + jnp.concatenate([carry] * (cols // V7X_LANES), axis=1)
    for k in range(TOP_K):
        sel = rowid == picked[k]
        rank_ref[k:k + 1, :] = jnp.sum(jnp.where(sel, before, 0.0), axis=0, keepdims=True).astype(jnp.int32)
        idx_ref[k:k + 1, :] = picked[k]
        w_ref[k:k + 1, :] = weights[k] / wsum * ROUTED_SCALE
    carry = carry + _dot(chosen_b, ones_ref[...])
    carry_ref[...] = carry
    counts_ref[...] = carry


def _route(x1, router_w, router_bias):
    tokens, d_model = x1.shape
    n_exp = router_w.shape[1]
    cols = ROUTE_COLS
    assert tokens % cols == 0 and n_exp % N_GROUPS == 0
    j = np.arange(cols)
    triu = jnp.asarray((j[:, None] < j[None, :]), dtype=BF16)
    ones = jnp.ones((cols, V7X_LANES), BF16)
    out_row = lambda dt: jax.ShapeDtypeStruct((TOP_K, tokens), dt)
    row_spec = pl.BlockSpec((TOP_K, cols), lambda i: (0, i))
    return pl.pallas_call(
        _route_kernel,
        grid=(tokens // cols,),
        in_specs=[pl.BlockSpec((cols, d_model), lambda i: (i, 0)),
                  _const_spec((n_exp, d_model)),
                  _const_spec((n_exp, 1)),
                  _const_spec((cols, cols)),
                  _const_spec((cols, V7X_LANES))],
        out_specs=[row_spec, row_spec, row_spec, pl.BlockSpec((n_exp, V7X_LANES), lambda i: (0, 0))],
        out_shape=[out_row(jnp.int32), out_row(F32), out_row(jnp.int32),
                   jax.ShapeDtypeStruct((n_exp, V7X_LANES), F32)],
        scratch_shapes=[pltpu.VMEM((n_exp, V7X_LANES), F32)],
        compiler_params=pltpu.CompilerParams(dimension_semantics=("arbitrary",), vmem_limit_bytes=V7X_VMEM_LIMIT),
        name="route",
    )(x1, router_w.T.astype(BF16), router_bias.reshape(n_exp, 1).astype(F32), triu, ones)


def _plan_kernel(idx_ref, rank_ref, pstart_ref, dest_ref):
    n_exp = pstart_ref.shape[0]
    cols = idx_ref.shape[1]
    rowid = lax.broadcasted_iota(jnp.int32, (n_exp, cols), 0)
    pstart = pstart_ref[...]
    for k in range(TOP_K):
        base = jnp.sum(jnp.where(rowid == idx_ref[k:k + 1, :], pstart, 0.0), axis=0, keepdims=True)
        dest_ref[k:k + 1, :] = base.astype(jnp.int32) + rank_ref[k:k + 1, :]


def _plan(idx, rank, pad_starts):
    tokens = idx.shape[1]
    n_exp = pad_starts.shape[0]
    cols = ROUTE_COLS
    row_spec = pl.BlockSpec((TOP_K, cols), lambda i: (0, i))
    return pl.pallas_call(
        _plan_kernel,
        grid=(tokens // cols,),
        in_specs=[row_spec, row_spec, _const_spec((n_exp, 1))],
        out_specs=row_spec,
        out_shape=jax.ShapeDtypeStruct((TOP_K, tokens), jnp.int32),
        compiler_params=pltpu.CompilerParams(dimension_semantics=("arbitrary",)),
        name="plan",
    )(idx, rank, pad_starts.reshape(n_exp, 1).astype(F32))


def _token_rows(tok, sub):
    return pl.ds(pl.multiple_of(tok * sub, sub), sub)


def _dispatch_kernel(pad_end_ref, padded_ref, dest_ref, x_ref, xs_hbm, zero_ref, sem, *, sub):
    rows = x_ref.shape[0] // sub
    n_exp = pad_end_ref.shape[0]

    def zero_copy(e):
        start = pl.multiple_of((pad_end_ref[e] - ROW_BLOCK) * sub, ROW_BLOCK * sub)
        return pltpu.make_async_copy(zero_ref, xs_hbm.at[pl.ds(start, ROW_BLOCK * sub), :], sem)

    @pl.when(pl.program_id(0) == 0)
    def _():
        zero_ref[...] = jnp.zeros_like(zero_ref)

        def issue(e, c):
            @pl.when(padded_ref[e] > 0)
            def _():
                zero_copy(e).start()
            return c
        lax.fori_loop(0, n_exp, issue, 0)

        def drain(e, c):
            @pl.when(padded_ref[e] > 0)
            def _():
                zero_copy(e).wait()
            return c
        lax.fori_loop(0, n_exp, drain, 0)

    def row_copy(k, r):
        return pltpu.make_async_copy(x_ref.at[_token_rows(r, sub), :],
                                     xs_hbm.at[_token_rows(dest_ref[r * TOP_K + k], sub), :], sem)

    def issue(g, c):
        for u in range(DMA_UNROLL):
            for k in range(TOP_K):
                row_copy(k, g * DMA_UNROLL + u).start(priority=k % 2)
        return c
    lax.fori_loop(0, rows // DMA_UNROLL, issue, 0)

    def drain(g, c):
        for u in range(DMA_UNROLL):
            for k in range(TOP_K):
                row_copy(k, g * DMA_UNROLL + u).wait()
        return c
    lax.fori_loop(0, rows // DMA_UNROLL, drain, 0)


def _dispatch(x1p, dest, pad_ends, padded, n_rows, sub):
    tokens = x1p.shape[0] // sub
    rows = DISPATCH_ROWS
    grid_spec = pltpu.PrefetchScalarGridSpec(
        num_scalar_prefetch=2,
        grid=(tokens // rows,),
        in_specs=[pl.BlockSpec((TOP_K * rows,), lambda i, pe, pd: (i,), memory_space=pltpu.SMEM),
                  pl.BlockSpec((rows * sub, V7X_LANES), lambda i, pe, pd: (i, 0))],
        out_specs=pl.BlockSpec(memory_space=pl.ANY),
        scratch_shapes=[pltpu.VMEM((ROW_BLOCK * sub, V7X_LANES), x1p.dtype), pltpu.SemaphoreType.DMA(())],
    )
    return pl.pallas_call(
        functools.partial(_dispatch_kernel, sub=sub),
        grid_spec=grid_spec,
        out_shape=jax.ShapeDtypeStruct((n_rows * sub, V7X_LANES), x1p.dtype),
        compiler_params=pltpu.CompilerParams(dimension_semantics=("arbitrary",), has_side_effects=True),
        name="dispatch",
    )(pad_ends, padded, dest, x1p)


def _experts_kernel(first_ref, nblk_ref, total_ref, xs_hbm, wg_ref, wu_ref, wd_ref, y_hbm,
                    xbuf_ref, ybuf_ref, wgu_b, wd_b, in_sem, out_sem, *, sub):
    e = pl.program_id(0)
    n = nblk_ref[e]
    first = first_ref[e]
    total = total_ref[0]
    ahead = EXPERT_AHEAD
    block = ROW_BLOCK * sub

    def block_rows(g):
        return pl.ds(pl.multiple_of(g * block, block), block)

    def in_copy(g):
        slot = lax.rem(g, EXPERT_SLOTS)
        return pltpu.make_async_copy(xs_hbm.at[block_rows(g), :], xbuf_ref.at[slot], in_sem.at[slot])

    def out_copy(g):
        slot = lax.rem(g, EXPERT_SLOTS)
        return pltpu.make_async_copy(ybuf_ref.at[slot], y_hbm.at[block_rows(g), :], out_sem.at[slot])

    @pl.when(e == 0)
    def _():
        for g in range(ahead):
            @pl.when(g < total)
            def _():
                in_copy(g).start()

    @pl.when(n > 0)
    def _():
        d_exp = wg_ref.shape[1]
        wgu_b[:, :d_exp] = wg_ref[...].astype(BF16)
        wgu_b[:, d_exp:] = wu_ref[...].astype(BF16)
        wd_b[...] = wd_ref[...].astype(BF16)

        def stage(g):
            slot = lax.rem(g, EXPERT_SLOTS)
            in_copy(g).wait()

            @pl.when(g + ahead < total)
            def _():
                in_copy(g + ahead).start()

            @pl.when(g >= EXPERT_SLOTS)
            def _():
                out_copy(g - EXPERT_SLOTS).wait()

            halves = [_unpack_rows(p) for p in _load_token_rows(xbuf_ref, ROW_BLOCK, (slot,))]
            return jnp.concatenate([h for h, _ in halves] + [l for _, l in halves], axis=1).astype(BF16)

        def swiglu(xb):
            gate_up = _dot(xb, wgu_b[...])
            hidden = _silu(gate_up[:, :d_exp]) * gate_up[:, d_exp:]
            return _pack_rows(_dot(hidden.astype(BF16), wd_b[...]))

        def finish(g, packed):
            _store_token_rows(ybuf_ref, packed, (lax.rem(g, EXPERT_SLOTS),))
            out_copy(g).start()

        def group(g, count):
            xbs = [stage(g + i) for i in range(count)]
            y = swiglu(xbs[0] if count == 1 else jnp.concatenate(xbs, axis=0))
            for i in range(count):
                finish(g + i, y[i * ROW_BLOCK:(i + 1) * ROW_BLOCK])

        def full_groups(q, c):
            group(first + EXPERT_GROUP * q, EXPERT_GROUP)
            return c
        lax.fori_loop(0, n // EXPERT_GROUP, full_groups, 0)

        done = n - lax.rem(n, EXPERT_GROUP)
        size = EXPERT_GROUP // 2
        while size >= 1:
            take = lax.rem(n, 2 * size) >= size

            @pl.when(take)
            def _(done=done, size=size):
                group(first + done, size)
            done = done + jnp.where(take, size, 0)
            size //= 2

    @pl.when(e == pl.num_programs(0) - 1)
    def _():
        for d in range(EXPERT_SLOTS):
            @pl.when(total - 1 - d >= 0)
            def _():
                out_copy(total - 1 - d).wait()


def _experts(xs, first_block, n_block, w_gate, w_up, w_down, sub):
    n_exp, d_model, d_exp = w_gate.shape
    assert sub * V7X_LANES * 2 == d_model and xs.shape[0] % (ROW_BLOCK * sub) == 0
    total = jnp.sum(n_block).reshape(1).astype(jnp.int32)
    grid_spec = pltpu.PrefetchScalarGridSpec(
        num_scalar_prefetch=3,
        grid=(n_exp,),
        in_specs=[pl.BlockSpec(memory_space=pl.ANY),
                  pl.BlockSpec((None, d_model, d_exp), lambda e, fb, nb, tt: (e, 0, 0)),
                  pl.BlockSpec((None, d_model, d_exp), lambda e, fb, nb, tt: (e, 0, 0)),
                  pl.BlockSpec((None, d_exp, d_model), lambda e, fb, nb, tt: (e, 0, 0))],
        out_specs=pl.BlockSpec(memory_space=pl.ANY),
        scratch_shapes=[pltpu.VMEM((EXPERT_SLOTS, ROW_BLOCK * sub, V7X_LANES), xs.dtype),
                        pltpu.VMEM((EXPERT_SLOTS, ROW_BLOCK * sub, V7X_LANES), xs.dtype),
                        pltpu.VMEM((d_model, 2 * d_exp), BF16),
                        pltpu.VMEM((d_exp, d_model), BF16),
                        pltpu.SemaphoreType.DMA((EXPERT_SLOTS,)),
                        pltpu.SemaphoreType.DMA((EXPERT_SLOTS,))],
    )
    return pl.pallas_call(
        functools.partial(_experts_kernel, sub=sub),
        grid_spec=grid_spec,
        out_shape=jax.ShapeDtypeStruct(xs.shape, xs.dtype),
        compiler_params=pltpu.CompilerParams(dimension_semantics=("arbitrary",), vmem_limit_bytes=V7X_VMEM_LIMIT,
                                             has_side_effects=True),
        name="experts",
    )(first_block, n_block, total, xs, w_gate, w_up, w_down)


def _combine_kernel(dest_ref, next_ref, wt_ref, x_ref, y_hbm, sg_ref, su_ref, sd_ref, g_ref, b_ref,
                    out_ref, ybuf_a, ybuf_b, sem_a, sem_b, *, alpha, sub):
    rows = x_ref.shape[0] // 2
    step = pl.program_id(0)

    def row_copy(idx_ref, tok, k, r, buf, sem):
        return pltpu.make_async_copy(y_hbm.at[_token_rows(idx_ref[tok * TOP_K + k], sub), :],
                                     buf.at[k, _token_rows(r, sub), :], sem)

    def issue_rolled(idx_ref, first_tok, buf, sem):
        def body(g, c):
            for u in range(DMA_UNROLL):
                for k in range(TOP_K):
                    r = g * DMA_UNROLL + u
                    row_copy(idx_ref, first_tok + r, k, r, buf, sem).start(priority=k % 2)
            return c
        lax.fori_loop(0, rows // DMA_UNROLL, body, 0)

    def issue_inline(idx_ref, first_tok, buf, sem):
        for r in range(rows):
            for k in range(TOP_K):
                row_copy(idx_ref, first_tok + r, k, r, buf, sem).start(priority=k % 2)

    def drain(buf, sem):
        def body(g, c):
            for u in range(DMA_UNROLL):
                for k in range(TOP_K):
                    r = g * DMA_UNROLL + u
                    pltpu.make_async_copy(y_hbm.at[_token_rows(0, sub), :],
                                          buf.at[k, _token_rows(r, sub), :], sem).wait()
            return c
        lax.fori_loop(0, rows // DMA_UNROLL, body, 0)

    def reduce_block(first_tok, buf):
        rs = slice(first_tok, first_tok + rows)
        x = x_ref[rs, :]
        xb = x.astype(BF16)
        hidden = _silu(_dot(xb, sg_ref[...])) * _dot(xb, su_ref[...])
        acc = alpha * x + _dot(hidden.astype(BF16), sd_ref[...])
        wt = wt_ref[rs, :]
        chunks = [acc[:, c * V7X_LANES:(c + 1) * V7X_LANES] for c in range(2 * sub)]
        for k in range(TOP_K):
            wk = wt[:, k:k + 1]
            for c, p in enumerate(_load_token_rows(buf, rows, (k,))):
                hi, lo = _unpack_rows(p)
                chunks[c] = chunks[c] + hi * wk
                chunks[sub + c] = chunks[sub + c] + lo * wk
        out_ref[rs, :] = _layernorm_rows(jnp.concatenate(chunks, axis=1), g_ref[...], b_ref[...])

    @pl.when(step == 0)
    def _():
        issue_rolled(dest_ref, 0, ybuf_a, sem_a)

    drain(ybuf_a, sem_a)
    issue_inline(dest_ref, rows, ybuf_b, sem_b)
    reduce_block(0, ybuf_a)
    drain(ybuf_b, sem_b)
    issue_inline(next_ref, 0, ybuf_a, sem_a)
    reduce_block(rows, ybuf_b)

    @pl.when(step == pl.num_programs(0) - 1)
    def _():
        drain(ybuf_a, sem_a)


def _combine(x1, y, dest, w_tok, sw_gate, sw_up, sw_down, ln2_g, ln2_b, alpha, sub):
    tokens, d_model = x1.shape
    d_shared = sw_gate.shape[1]
    rows = 2 * COMBINE_ROWS
    assert tokens % rows == 0
    steps = tokens // rows
    row2 = lambda a: a.reshape(1, -1).astype(F32)
    return pl.pallas_call(
        functools.partial(_combine_kernel, alpha=alpha, sub=sub),
        grid=(steps,),
        in_specs=[pl.BlockSpec((TOP_K * rows,), lambda i: (i,), memory_space=pltpu.SMEM),
                  pl.BlockSpec((TOP_K * rows,), lambda i: (jnp.minimum(i + 1, steps - 1),), memory_space=pltpu.SMEM),
                  pl.BlockSpec((rows, TOP_K), lambda i: (i, 0)),
                  pl.BlockSpec((rows, d_model), lambda i: (i, 0)),
                  pl.BlockSpec(memory_space=pl.ANY),
                  _const_spec((d_model, d_shared)),
                  _const_spec((d_model, d_shared)),
                  _const_spec((d_shared, d_model)),
                  _const_spec((1, d_model)),
                  _const_spec((1, d_model))],
        out_specs=pl.BlockSpec((rows, d_model), lambda i: (i, 0)),
        out_shape=jax.ShapeDtypeStruct((tokens, d_model), F32),
        scratch_shapes=[pltpu.VMEM((TOP_K, COMBINE_ROWS * sub, V7X_LANES), y.dtype),
                        pltpu.VMEM((TOP_K, COMBINE_ROWS * sub, V7X_LANES), y.dtype),
                        pltpu.SemaphoreType.DMA(()), pltpu.SemaphoreType.DMA(())],
        compiler_params=pltpu.CompilerParams(dimension_semantics=("arbitrary",), vmem_limit_bytes=V7X_VMEM_LIMIT),
        name="combine",
    )(dest, dest, w_tok, x1, y, sw_gate.astype(BF16), sw_up.astype(BF16), sw_down.astype(BF16),
      row2(ln2_g), row2(ln2_b))


def _block_layout(counts, tokens):
    n_exp = counts.shape[0]
    padded = (counts + ROW_BLOCK - 1) // ROW_BLOCK * ROW_BLOCK
    pad_ends = jnp.cumsum(padded)
    pad_starts = pad_ends - padded
    n_blocks = -(-(tokens * TOP_K + n_exp * (ROW_BLOCK - 1)) // ROW_BLOCK)
    return padded, pad_ends, pad_starts, pad_starts // ROW_BLOCK, padded // ROW_BLOCK, n_blocks * ROW_BLOCK


def kernel(x, w_in, ret_norm_g, gla_gate_w2, gla_gate_b, gla_norm_g, w_ret_out, w_gla_out, w_o, ln1_g, ln1_b, router_w, router_bias, exp_w_gate, exp_w_up, exp_w_down, shared_w_gate, shared_w_up, shared_w_down, ln2_g, ln2_b):
    batch, seq, d_model = x.shape
    depth = w_in.shape[0]
    alpha = (2.0 * depth) ** 0.25
    for l in range(depth):
        x1, x1p = _mix(x, w_in[l], ret_norm_g[l], gla_gate_w2[l], gla_gate_b[l], gla_norm_g[l],
                       w_ret_out[l], w_gla_out[l], w_o[l], ln1_g[l], ln1_b[l], alpha)
        x1 = x1.reshape(batch * seq, d_model)
        sub = x1p.shape[1] // seq
        x1p = x1p.reshape(batch * seq * sub, V7X_LANES)
        idx, w_sel, rank, counts = _route(x1, router_w[l], router_bias[l])
        padded, pad_ends, pad_starts, first_block, n_block, n_rows = _block_layout(
            counts[:, 0].astype(jnp.int32), batch * seq)
        dest = _plan(idx, rank, pad_starts).T.reshape(-1)
        xs = _dispatch(x1p, dest, pad_ends, padded, n_rows, sub)
        y = _experts(xs, first_block, n_block, exp_w_gate[l], exp_w_up[l], exp_w_down[l], sub)
        out = _combine(x1, y, dest, w_sel.T, shared_w_gate[l], shared_w_up[l], shared_w_down[l],
                       ln2_g[l], ln2_b[l], alpha, sub)
        x = out.reshape(batch, seq, d_model)
    return x
```

```python
import functools

import jax
import jax.numpy as jnp
import numpy as np
from jax import lax
from jax.experimental import pallas as pl
from jax.experimental.pallas import tpu as pltpu

CHUNK = 64
RET_HEADS = 4
RET_DK = 128
RET_DV = 256
GLA_HEADS = 4
GLA_DK = 128
GLA_DV = 256
GLA_GATE_RANK = 16
GLA_GATE_TAU = 16.0
ROPE_THETA = 10000.0
N_EXPERTS = 256
TOP_K = 8
N_GROUPS = 8
TOPK_GROUPS = 4
ROUTED_SCALE = 2.5
LN_EPS = 1e-5
NORM_EPS = 1e-6

V7X_LANES = 128
V7X_VMEM_LIMIT = 60 * 1024 * 1024

MIX_ROWS = 256
MIX_SEQS = 2
ROUTE_COLS = 512
ROW_BLOCK = 256
DISPATCH_ROWS = 512
DMA_UNROLL = 2
EXPERT_GROUP = 4
EXPERT_AHEAD = 4
EXPERT_SLOTS = EXPERT_GROUP + EXPERT_AHEAD
COMBINE_ROWS = 256

F32 = jnp.float32
BF16 = jnp.bfloat16


def _dot(a, b):
    return jnp.dot(a, b, preferred_element_type=F32)


def _dot_nt(a, b):
    return lax.dot_general(a, b, (((1,), (1,)), ((), ())), preferred_element_type=F32)


def _dot_tn(a, b):
    return lax.dot_general(a, b, (((0,), (0,)), ((), ())), preferred_element_type=F32)


def _sigmoid(v):
    return 1.0 / (1.0 + jnp.exp(-v))


def _silu(v):
    return v * _sigmoid(v)


def _pack_rows(v):
    half = v.shape[1] // 2
    hi = lax.bitcast_convert_type(v[:, :half].astype(BF16).astype(F32), jnp.uint32)
    lo = lax.bitcast_convert_type(v[:, half:].astype(BF16).astype(F32), jnp.uint32)
    return hi | (lo >> 16)


def _unpack_rows(p):
    hi = lax.bitcast_convert_type(p & jnp.uint32(0xFFFF0000), F32)
    lo = lax.bitcast_convert_type(p << 16, F32)
    return hi, lo


def _store_token_rows(ref, packed, lead=()):
    m, width = packed.shape
    sub = width // V7X_LANES
    for c in range(sub):
        ref[lead + (pl.ds(c, m, stride=sub), slice(None))] = packed[:, c * V7X_LANES:(c + 1) * V7X_LANES]


def _load_token_rows(ref, m, lead=()):
    sub = ref.shape[-2] // m
    return [ref[lead + (pl.ds(c, m, stride=sub), slice(None))] for c in range(sub)]


def _layernorm_rows(v, g, b):
    mu = jnp.mean(v, axis=-1, keepdims=True)
    vc = v - mu
    var = jnp.mean(vc * vc, axis=-1, keepdims=True)
    return vc * lax.rsqrt(var + LN_EPS) * g + b


def _mix_kernel(x_ref, wrq_ref, wrk_ref, wrv_ref, wrg_ref, wgq_ref, wgk_ref, wgv_ref, wgg_ref, wga_ref, wmg_ref,
                cos_ref, sin_ref, dmask_ref, qdec_ref, kdec_ref, tri_ref, w2_ref, gb_ref, retg_ref, glag_ref,
                wro_ref, wgo_ref, wo_ref, ln1g_ref, ln1b_ref,
                out_ref, packed_ref, rstate_ref, gstate_ref, yret_ref, ygla_ref, *, block_decay, alpha):
    n_seq, rows, d_model = x_ref.shape

    @pl.when(pl.program_id(1) == 0)
    def _():
        rstate_ref[...] = jnp.zeros_like(rstate_ref)
        gstate_ref[...] = jnp.zeros_like(gstate_ref)

    x = x_ref[...].reshape(n_seq * rows, d_model)
    xb = x.astype(BF16)

    def proj(w_ref):
        return _dot(xb, w_ref[...])

    cos = cos_ref[...]
    sin = sin_ref[...]

    def rope(t):
        return t * cos + pltpu.roll(t, RET_DK // 2, 1) * sin

    rq = proj(wrq_ref)
    rk = proj(wrk_ref)
    rv = proj(wrv_ref)
    rg = proj(wrg_ref)
    seq_heads = [(s, h) for s in range(n_seq) for h in range(RET_HEADS)]
    ret_states = [rstate_ref[s * RET_HEADS + h] for s, h in seq_heads]
    for i, (s, h) in enumerate(seq_heads):
        sr = slice(s * rows, (s + 1) * rows)
        qk = slice(h * RET_DK, (h + 1) * RET_DK)
        vv = slice(h * RET_DV, (h + 1) * RET_DV)
        q = rope(rq[sr, qk])
        k = rope(rk[sr, qk])
        v = rv[sr, vv].astype(BF16)
        scores = _dot_nt(q.astype(BF16), k.astype(BF16)) * dmask_ref[h]
        o = _dot(scores.astype(BF16), v)
        state = ret_states[i]
        o = o + _dot((q * qdec_ref[h]).astype(BF16), state.astype(BF16))
        ret_states[i] = state * block_decay[h] + _dot_tn((k * kdec_ref[h]).astype(BF16), v)
        mu = jnp.mean(o, axis=-1, keepdims=True)
        oc = o - mu
        var = jnp.mean(oc * oc, axis=-1, keepdims=True)
        y = oc * lax.rsqrt(var + LN_EPS) * retg_ref[:, vv] * _silu(rg[sr, vv])
        yret_ref[sr, vv] = y.astype(BF16)
    for i, (s, h) in enumerate(seq_heads):
        rstate_ref[s * RET_HEADS + h] = ret_states[i]

    gq = proj(wgq_ref) * (GLA_DK ** -0.5)
    gk = proj(wgk_ref)
    gv = proj(wgv_ref)
    gg = proj(wgg_ref)
    ga = proj(wga_ref)
    z = _dot(ga.astype(BF16), w2_ref[...]) + gb_ref[...]
    log_a = (jnp.minimum(z, 0.0) - jnp.log1p(jnp.exp(-jnp.abs(z)))) * (1.0 / GLA_GATE_TAU)
    la_hi = log_a.astype(BF16)
    la_lo = (log_a - la_hi.astype(F32)).astype(BF16)
    tri = tri_ref[...]
    bcum = jnp.concatenate([_dot(tri, la_hi[s * rows:(s + 1) * rows]) + _dot(tri, la_lo[s * rows:(s + 1) * rows])
                            for s in range(n_seq)], axis=0)
    n_chunks = rows // CHUNK
    seq_heads = [(s, h) for s in range(n_seq) for h in range(GLA_HEADS)]
    gla_states = [gstate_ref[s * GLA_HEADS + h] for s, h in seq_heads]
    decays = []
    updates = []
    for c in range(n_seq * n_chunks):
        rs = slice(c * CHUNK, (c + 1) * CHUNK)
        b_end = bcum[(c + 1) * CHUNK - 1:(c + 1) * CHUNK, :]
        kt = (gk[rs, :] * jnp.exp(b_end - bcum[rs, :])).astype(BF16)
        decays.append(jnp.exp(b_end))
        updates.append([_dot_tn(gv[rs, h * GLA_DV:(h + 1) * GLA_DV].astype(BF16), kt[:, h * GLA_DK:(h + 1) * GLA_DK])
                        for h in range(GLA_HEADS)])
    for c in range(n_seq * n_chunks):
        rs = slice(c * CHUNK, (c + 1) * CHUNK)
        for h in range(GLA_HEADS):
            i = (c // n_chunks) * GLA_HEADS + h
            qk = slice(h * GLA_DK, (h + 1) * GLA_DK)
            vv = slice(h * GLA_DV, (h + 1) * GLA_DV)
            gla_states[i] = gla_states[i] * decays[c][:, qk] + updates[c][h]
            o = _dot_nt(gq[rs, qk].astype(BF16), gla_states[i].astype(BF16))
            ms = jnp.mean(o * o, axis=-1, keepdims=True)
            y = o * lax.rsqrt(ms + NORM_EPS) * glag_ref[:, vv] * _silu(gg[rs, vv])
            ygla_ref[rs, vv] = y.astype(BF16)
    for i, (s, h) in enumerate(seq_heads):
        gstate_ref[s * GLA_HEADS + h] = gla_states[i]

    u_ret = _dot(yret_ref[...], wro_ref[...])
    u_gla = _dot(ygla_ref[...], wgo_ref[...])
    gate = _sigmoid(proj(wmg_ref))
    merged = gate[:, :d_model] * u_ret + gate[:, d_model:] * u_gla
    mix = _dot(merged.astype(BF16), wo_ref[...])
    out = _layernorm_rows(alpha * x + mix, ln1g_ref[...], ln1b_ref[...])
    out_ref[...] = out.reshape(n_seq, rows, d_model)
    packed = _pack_rows(out)
    for s in range(n_seq):
        _store_token_rows(packed_ref, packed[s * rows:(s + 1) * rows], (s,))


def _mix_tables(seq, rows):
    half = RET_DK // 2
    inv = ROPE_THETA ** (-np.arange(half, dtype=np.float64) / half)
    ang = np.arange(seq, dtype=np.float64)[:, None] * inv[None, :]
    cos2 = np.concatenate([np.cos(ang), np.cos(ang)], axis=1)
    sin2 = np.concatenate([-np.sin(ang), np.sin(ang)], axis=1)
    log_g = np.log1p(-np.exp2(-5.0 - np.arange(RET_HEADS, dtype=np.float64)))
    j = np.arange(rows, dtype=np.float64)
    same_or_earlier_chunk = (j[None, :] // CHUNK) <= (j[:, None] // CHUNK)
    k_scale = RET_DK ** -0.5
    dmask = np.exp(log_g[:, None, None] * np.abs(j[:, None] - j[None, :])) * same_or_earlier_chunk[None] * k_scale
    qdec = np.exp(log_g[:, None] * (j[None, :] + 1.0))
    kdec = np.exp(log_g[:, None] * (rows - 1.0 - j[None, :])) * k_scale
    qdec = np.broadcast_to(qdec[:, :, None], (RET_HEADS, rows, RET_DK))
    kdec = np.broadcast_to(kdec[:, :, None], (RET_HEADS, rows, RET_DK))
    block_decay = tuple(float(v) for v in np.exp(log_g * rows))
    tri = ((j[None, :] <= j[:, None]) & ((j[None, :] // CHUNK) == (j[:, None] // CHUNK)))
    to = lambda a, dt: jnp.asarray(np.ascontiguousarray(a), dtype=dt)
    return (to(cos2, F32), to(sin2, F32), to(dmask, F32), to(qdec, F32), to(kdec, F32), to(tri, BF16)), block_decay


def _const_spec(shape):
    nd = len(shape)
    return pl.BlockSpec(shape, lambda *_: (0,) * nd, pipeline_mode=pl.Buffered(1))


def _mix(x, w_in, ret_norm_g, gla_gate_w2, gla_gate_b, gla_norm_g, w_ret_out, w_gla_out, w_o, ln1_g, ln1_b, alpha):
    batch, seq, d_model = x.shape
    rows = MIX_ROWS
    assert seq % rows == 0 and rows % CHUNK == 0
    ret_qk, ret_v = RET_HEADS * RET_DK, RET_HEADS * RET_DV
    gla_qk, gla_v = GLA_HEADS * GLA_DK, GLA_HEADS * GLA_DV
    splits = (ret_qk, ret_qk, ret_v, ret_v, gla_qk, gla_qk, gla_v, gla_v, GLA_GATE_RANK, 2 * d_model)
    assert w_in.shape == (d_model, sum(splits))
    offs = np.cumsum((0,) + splits)
    parts = [w_in[:, offs[i]:offs[i + 1]].astype(BF16) for i in range(len(splits))]
    parts[8] = jnp.pad(parts[8], ((0, 0), (0, V7X_LANES - GLA_GATE_RANK)))
    w2 = jnp.pad(gla_gate_w2.astype(BF16), ((0, V7X_LANES - GLA_GATE_RANK), (0, 0)))
    (cos2, sin2, dmask, qdec, kdec, tri), block_decay = _mix_tables(seq, rows)
    row2 = lambda a: a.reshape(1, -1).astype(F32)
    consts = [dmask, qdec, kdec, tri, w2, row2(gla_gate_b), row2(ret_norm_g), row2(gla_norm_g),
              w_ret_out.astype(BF16), w_gla_out.astype(BF16), w_o.astype(BF16), row2(ln1_g), row2(ln1_b)]
    sub = d_model // 2 // V7X_LANES
    pos_spec = pl.BlockSpec((rows, RET_DK), lambda b, s: (s, 0))
    n_seq = MIX_SEQS
    assert batch % n_seq == 0
    in_specs = ([pl.BlockSpec((n_seq, rows, d_model), lambda b, s: (b, s, 0))]
                + [_const_spec(p.shape) for p in parts]
                + [pos_spec, pos_spec]
                + [_const_spec(c.shape) for c in consts])
    return pl.pallas_call(
        functools.partial(_mix_kernel, block_decay=block_decay, alpha=alpha),
        grid=(batch // n_seq, seq // rows),
        in_specs=in_specs,
        out_specs=[pl.BlockSpec((n_seq, rows, d_model), lambda b, s: (b, s, 0)),
                   pl.BlockSpec((n_seq, rows * sub, V7X_LANES), lambda b, s: (b, s, 0))],
        out_shape=[jax.ShapeDtypeStruct((batch, seq, d_model), F32),
                   jax.ShapeDtypeStruct((batch, seq * sub, V7X_LANES), jnp.uint32)],
        scratch_shapes=[pltpu.VMEM((n_seq * RET_HEADS, RET_DK, RET_DV), F32),
                        pltpu.VMEM((n_seq * GLA_HEADS, GLA_DV, GLA_DK), F32),
                        pltpu.VMEM((n_seq * rows, ret_v), BF16),
                        pltpu.VMEM((n_seq * rows, gla_v), BF16)],
        compiler_params=pltpu.CompilerParams(dimension_semantics=("arbitrary", "arbitrary"),
                                             vmem_limit_bytes=V7X_VMEM_LIMIT),
        name="mix",
    )(x, *parts, cos2, sin2, *consts)


def _route_kernel(x_ref, rwt_ref, bias_ref, triu_ref, ones_ref,
                  idx_ref, w_ref, rank_ref, counts_ref, carry_ref):
    cols = x_ref.shape[0]
    n_exp = rwt_ref.shape[0]
    per_group = n_exp // N_GROUPS
    neg_inf = -jnp.inf

    @pl.when(pl.program_id(0) == 0)
    def _():
        carry_ref[...] = jnp.zeros_like(carry_ref)

    logits = _dot_nt(rwt_ref[...], x_ref[...].astype(BF16))
    scores = _sigmoid(logits)
    biased = scores + bias_ref[...]

    sub = lax.broadcasted_iota(jnp.int32, (per_group, cols), 0)
    gscore = []
    for g in range(N_GROUPS):
        blk = biased[g * per_group:(g + 1) * per_group, :]
        m1 = jnp.max(blk, axis=0, keepdims=True)
        i1 = jnp.min(jnp.where(blk == m1, sub, per_group), axis=0, keepdims=True)
        m2 = jnp.max(jnp.where(sub == i1, neg_inf, blk), axis=0, keepdims=True)
        gscore.append(m1 + m2)
    masked = []
    for g in range(N_GROUPS):
        ahead = jnp.zeros((1, cols), jnp.int32)
        for o in range(N_GROUPS):
            if o == g:
                continue
            before = (gscore[o] >= gscore[g]) if o < g else (gscore[o] > gscore[g])
            ahead = ahead + before.astype(jnp.int32)
        keep = ahead < TOPK_GROUPS
        blk = biased[g * per_group:(g + 1) * per_group, :]
        masked.append(jnp.where(keep, blk, neg_inf))
    candidates = jnp.concatenate(masked, axis=0)
    cur = candidates

    rowid = lax.broadcasted_iota(jnp.int32, (n_exp, cols), 0)
    picked = []
    weights = []
    for _ in range(TOP_K):
        m = jnp.max(cur, axis=0, keepdims=True)
        ik = jnp.min(jnp.where(cur == m, rowid, n_exp), axis=0, keepdims=True)
        sel = rowid == ik
        weights.append(jnp.sum(jnp.where(sel, scores, 0.0), axis=0, keepdims=True))
        cur = jnp.where(sel, neg_inf, cur)
        picked.append(ik)
    wsum = weights[0]
    for wk in weights[1:]:
        wsum = wsum + wk

    chosen = (cur == neg_inf) & (candidates != neg_inf)
    chosen_b = jnp.where(chosen, 1.0, 0.0).astype(BF16)
    carry = carry_ref[...]
    before = _dot(chosen_b, triu_ref[...]) + jnp.concatenate([carry] * (cols // V7X_LANES), axis=1)
    for k in range(TOP_K):
        sel = rowid == picked[k]
        rank_ref[k:k + 1, :] = jnp.sum(jnp.where(sel, before, 0.0), axis=0, keepdims=True).astype(jnp.int32)
        idx_ref[k:k + 1, :] = picked[k]
        w_ref[k:k + 1, :] = weights[k] / wsum * ROUTED_SCALE
    carry = carry + _dot(chosen_b, ones_ref[...])
    carry_ref[...] = carry
    counts_ref[...] = carry


def _route(x1, router_w, router_bias):
    tokens, d_model = x1.shape
    n_exp = router_w.shape[1]
    cols = ROUTE_COLS
    assert tokens % cols == 0 and n_exp % N_GROUPS == 0
    j = np.arange(cols)
    triu = jnp.asarray((j[:, None] < j[None, :]), dtype=BF16)
    ones = jnp.ones((cols, V7X_LANES), BF16)
    out_row = lambda dt: jax.ShapeDtypeStruct((TOP_K, tokens), dt)
    row_spec = pl.BlockSpec((TOP_K, cols), lambda i: (0, i))
    return pl.pallas_call(
        _route_kernel,
        grid=(tokens // cols,),
        in_specs=[pl.BlockSpec((cols, d_model), lambda i: (i, 0)),
                  _const_spec((n_exp, d_model)),
                  _const_spec((n_exp, 1)),
                  _const_spec((cols, cols)),
                  _const_spec((cols, V7X_LANES))],
        out_specs=[row_spec, row_spec, row_spec, pl.BlockSpec((n_exp, V7X_LANES), lambda i: (0, 0))],
        out_shape=[out_row(jnp.int32), out_row(F32), out_row(jnp.int32),
                   jax.ShapeDtypeStruct((n_exp, V7X_LANES), F32)],
        scratch_shapes=[pltpu.VMEM((n_exp, V7X_LANES), F32)],
        compiler_params=pltpu.CompilerParams(dimension_semantics=("arbitrary",), vmem_limit_bytes=V7X_VMEM_LIMIT),
        name="route",
    )(x1, router_w.T.astype(BF16), router_bias.reshape(n_exp, 1).astype(F32), triu, ones)


def _plan_kernel(idx_ref, rank_ref, pstart_ref, dest_ref):
    n_exp = pstart_ref.shape[0]
    cols = idx_ref.shape[1]
    rowid = lax.broadcasted_iota(jnp.int32, (n_exp, cols), 0)
    pstart = pstart_ref[...]
    for k in range(TOP_K):
        base = jnp.sum(jnp.where(rowid == idx_ref[k:k + 1, :], pstart, 0.0), axis=0, keepdims=True)
        dest_ref[k:k + 1, :] = base.astype(jnp.int32) + rank_ref[k:k + 1, :]


def _plan(idx, rank, pad_starts):
    tokens = idx.shape[1]
    n_exp = pad_starts.shape[0]
    cols = ROUTE_COLS
    row_spec = pl.BlockSpec((TOP_K, cols), lambda i: (0, i))
    return pl.pallas_call(
        _plan_kernel,
        grid=(tokens // cols,),
        in_specs=[row_spec, row_spec, _const_spec((n_exp, 1))],
        out_specs=row_spec,
        out_shape=jax.ShapeDtypeStruct((TOP_K, tokens), jnp.int32),
        compiler_params=pltpu.CompilerParams(dimension_semantics=("arbitrary",)),
        name="plan",
    )(idx, rank, pad_starts.reshape(n_exp, 1).astype(F32))


def _token_rows(tok, sub):
    return pl.ds(pl.multiple_of(tok * sub, sub), sub)


def _dispatch_kernel(pad_end_ref, padded_ref, dest_ref, x_ref, xs_hbm, zero_ref, sem, *, sub):
    rows = x_ref.shape[0] // sub
    n_exp = pad_end_ref.shape[0]

    def zero_copy(e):
        start = pl.multiple_of((pad_end_ref[e] - ROW_BLOCK) * sub, ROW_BLOCK * sub)
        return pltpu.make_async_copy(zero_ref, xs_hbm.at[pl.ds(start, ROW_BLOCK * sub), :], sem)

    @pl.when(pl.program_id(0) == 0)
    def _():
        zero_ref[...] = jnp.zeros_like(zero_ref)

        def issue(e, c):
            @pl.when(padded_ref[e] > 0)
            def _():
                zero_copy(e).start()
            return c
        lax.fori_loop(0, n_exp, issue, 0)

        def drain(e, c):
            @pl.when(padded_ref[e] > 0)
            def _():
                zero_copy(e).wait()
            return c
        lax.fori_loop(0, n_exp, drain, 0)

    def row_copy(k, r):
        return pltpu.make_async_copy(x_ref.at[_token_rows(r, sub), :],
                                     xs_hbm.at[_token_rows(dest_ref[r * TOP_K + k], sub), :], sem)

    def issue(g, c):
        for u in range(DMA_UNROLL):
            for k in range(TOP_K):
                row_copy(k, g * DMA_UNROLL + u).start(priority=k % 2)
        return c
    lax.fori_loop(0, rows // DMA_UNROLL, issue, 0)

    def drain(g, c):
        for u in range(DMA_UNROLL):
            for k in range(TOP_K):
                row_copy(k, g * DMA_UNROLL + u).wait()
        return c
    lax.fori_loop(0, rows // DMA_UNROLL, drain, 0)


def _dispatch(x1p, dest, pad_ends, padded, n_rows, sub):
    tokens = x1p.shape[0] // sub
    rows = DISPATCH_ROWS
    grid_spec = pltpu.PrefetchScalarGridSpec(
        num_scalar_prefetch=2,
        grid=(tokens // rows,),
        in_specs=[pl.BlockSpec((TOP_K * rows,), lambda i, pe, pd: (i,), memory_space=pltpu.SMEM),
                  pl.BlockSpec((rows * sub, V7X_LANES), lambda i, pe, pd: (i, 0))],
        out_specs=pl.BlockSpec(memory_space=pl.ANY),
        scratch_shapes=[pltpu.VMEM((ROW_BLOCK * sub, V7X_LANES), x1p.dtype), pltpu.SemaphoreType.DMA(())],
    )
    return pl.pallas_call(
        functools.partial(_dispatch_kernel, sub=sub),
        grid_spec=grid_spec,
        out_shape=jax.ShapeDtypeStruct((n_rows * sub, V7X_LANES), x1p.dtype),
        compiler_params=pltpu.CompilerParams(dimension_semantics=("arbitrary",), has_side_effects=True),
        name="dispatch",
    )(pad_ends, padded, dest, x1p)


def _experts_kernel(first_ref, nblk_ref, total_ref, xs_hbm, wg_ref, wu_ref, wd_ref, y_hbm,
                    xbuf_ref, ybuf_ref, wgu_b, wd_b, in_sem, out_sem, *, sub):
    e = pl.program_id(0)
    n = nblk_ref[e]
    first = first_ref[e]
    total = total_ref[0]
    ahead = EXPERT_AHEAD
    block = ROW_BLOCK * sub

    def block_rows(g):
        return pl.ds(pl.multiple_of(g * block, block), block)

    def in_copy(g):
        slot = lax.rem(g, EXPERT_SLOTS)
        return pltpu.make_async_copy(xs_hbm.at[block_rows(g), :], xbuf_ref.at[slot], in_sem.at[slot])

    def out_copy(g):
        slot = lax.rem(g, EXPERT_SLOTS)
        return pltpu.make_async_copy(ybuf_ref.at[slot], y_hbm.at[block_rows(g), :], out_sem.at[slot])

    @pl.when(e == 0)
    def _():
        for g in range(ahead):
            @pl.when(g < total)
            def _():
                in_copy(g).start()

    @pl.when(n > 0)
    def _():
        d_exp = wg_ref.shape[1]
        wgu_b[:, :d_exp] = wg_ref[...].astype(BF16)
        wgu_b[:, d_exp:] = wu_ref[...].astype(BF16)
        wd_b[...] = wd_ref[...].astype(BF16)

        def stage(g):
            slot = lax.rem(g, EXPERT_SLOTS)
            in_copy(g).wait()

            @pl.when(g + ahead < total)
            def _():
                in_copy(g + ahead).start()

            @pl.when(g >= EXPERT_SLOTS)
            def _():
                out_copy(g - EXPERT_SLOTS).wait()

            halves = [_unpack_rows(p) for p in _load_token_rows(xbuf_ref, ROW_BLOCK, (slot,))]
            return jnp.concatenate([h for h, _ in halves] + [l for _, l in halves], axis=1).astype(BF16)

        def swiglu(xb):
            gate_up = _dot(xb, wgu_b[...])
            hidden = _silu(gate_up[:, :d_exp]) * gate_up[:, d_exp:]
            return _pack_rows(_dot(hidden.astype(BF16), wd_b[...]))

        def finish(g, packed):
            _store_token_rows(ybuf_ref, packed, (lax.rem(g, EXPERT_SLOTS),))
            out_copy(g).start()

        def group(g, count):
            xbs = [stage(g + i) for i in range(count)]
            y = swiglu(xbs[0] if count == 1 else jnp.concatenate(xbs, axis=0))
            for i in range(count):
                finish(g + i, y[i * ROW_BLOCK:(i + 1) * ROW_BLOCK])

        def full_groups(q, c):
            group(first + EXPERT_GROUP * q, EXPERT_GROUP)
            return c
        lax.fori_loop(0, n // EXPERT_GROUP, full_groups, 0)

        done = n - lax.rem(n, EXPERT_GROUP)
        size = EXPERT_GROUP // 2
        while size >= 1:
            take = lax.rem(n, 2 * size) >= size

            @pl.when(take)
            def _(done=done, size=size):
                group(first + done, size)
            done = done + jnp.where(take, size, 0)
            size //= 2

    @pl.when(e == pl.num_programs(0) - 1)
    def _():
        for d in range(EXPERT_SLOTS):
            @pl.when(total - 1 - d >= 0)
            def _():
                out_copy(total - 1 - d).wait()


def _experts(xs, first_block, n_block, w_gate, w_up, w_down, sub):
    n_exp, d_model, d_exp = w_gate.shape
    assert sub * V7X_LANES * 2 == d_model and xs.shape[0] % (ROW_BLOCK * sub) == 0
    total = jnp.sum(n_block).reshape(1).astype(jnp.int32)
    grid_spec = pltpu.PrefetchScalarGridSpec(
        num_scalar_prefetch=3,
        grid=(n_exp,),
        in_specs=[pl.BlockSpec(memory_space=pl.ANY),
                  pl.BlockSpec((None, d_model, d_exp), lambda e, fb, nb, tt: (e, 0, 0)),
                  pl.BlockSpec((None, d_model, d_exp), lambda e, fb, nb, tt: (e, 0, 0)),
                  pl.BlockSpec((None, d_exp, d_model), lambda e, fb, nb, tt: (e, 0, 0))],
        out_specs=pl.BlockSpec(memory_space=pl.ANY),
        scratch_shapes=[pltpu.VMEM((EXPERT_SLOTS, ROW_BLOCK * sub, V7X_LANES), xs.dtype),
                        pltpu.VMEM((EXPERT_SLOTS, ROW_BLOCK * sub, V7X_LANES), xs.dtype),
                        pltpu.VMEM((d_model, 2 * d_exp), BF16),
                        pltpu.VMEM((d_exp, d_model), BF16),
                        pltpu.SemaphoreType.DMA((EXPERT_SLOTS,)),
                        pltpu.SemaphoreType.DMA((EXPERT_SLOTS,))],
    )
    return pl.pallas_call(
        functools.partial(_experts_kernel, sub=sub),
        grid_spec=grid_spec,
        out_shape=jax.ShapeDtypeStruct(xs.shape, xs.dtype),
        compiler_params=pltpu.CompilerParams(dimension_semantics=("arbitrary",), vmem_limit_bytes=V7X_VMEM_LIMIT,
                                             has_side_effects=True),
        name="experts",
    )(first_block, n_block, total, xs, w_gate, w_up, w_down)


def _combine_kernel(dest_ref, next_ref, wt_ref, x_ref, y_hbm, sg_ref, su_ref, sd_ref, g_ref, b_ref,
                    out_ref, ybuf_a, ybuf_b, sem_a, sem_b, *, alpha, sub):
    rows = x_ref.shape[0] // 2
    step = pl.program_id(0)

    def row_copy(idx_ref, tok, k, r, buf, sem):
        return pltpu.make_async_copy(y_hbm.at[_token_rows(idx_ref[tok * TOP_K + k], sub), :],
                                     buf.at[k, _token_rows(r, sub), :], sem)

    def issue_rolled(idx_ref, first_tok, buf, sem):
        def body(g, c):
            for u in range(DMA_UNROLL):
                for k in range(TOP_K):
                    r = g * DMA_UNROLL + u
                    row_copy(idx_ref, first_tok + r, k, r, buf, sem).start(priority=k % 2)
            return c
        lax.fori_loop(0, rows // DMA_UNROLL, body, 0)

    def issue_inline(idx_ref, first_tok, buf, sem):
        for r in range(rows):
            for k in range(TOP_K):
                row_copy(idx_ref, first_tok + r, k, r, buf, sem).start(priority=k % 2)

    def drain(buf, sem):
        def body(g, c):
            for u in range(DMA_UNROLL):
                for k in range(TOP_K):
                    r = g * DMA_UNROLL + u
                    pltpu.make_async_copy(y_hbm.at[_token_rows(0, sub), :],
                                          buf.at[k, _token_rows(r, sub), :], sem).wait()
            return c
        lax.fori_loop(0, rows // DMA_UNROLL, body, 0)

    def reduce_block(first_tok, buf):
        rs = slice(first_tok, first_tok + rows)
        x = x_ref[rs, :]
        xb = x.astype(BF16)
        hidden = _silu(_dot(xb, sg_ref[...])) * _dot(xb, su_ref[...])
        acc = alpha * x + _dot(hidden.astype(BF16), sd_ref[...])
        wt = wt_ref[rs, :]
        chunks = [acc[:, c * V7X_LANES:(c + 1) * V7X_LANES] for c in range(2 * sub)]
        for k in range(TOP_K):
            wk = wt[:, k:k + 1]
            for c, p in enumerate(_load_token_rows(buf, rows, (k,))):
                hi, lo = _unpack_rows(p)
                chunks[c] = chunks[c] + hi * wk
                chunks[sub + c] = chunks[sub + c] + lo * wk
        out_ref[rs, :] = _layernorm_rows(jnp.concatenate(chunks, axis=1), g_ref[...], b_ref[...])

    @pl.when(step == 0)
    def _():
        issue_rolled(dest_ref, 0, ybuf_a, sem_a)

    drain(ybuf_a, sem_a)
    issue_inline(dest_ref, rows, ybuf_b, sem_b)
    reduce_block(0, ybuf_a)
    drain(ybuf_b, sem_b)
    issue_inline(next_ref, 0, ybuf_a, sem_a)
    reduce_block(rows, ybuf_b)

    @pl.when(step == pl.num_programs(0) - 1)
    def _():
        drain(ybuf_a, sem_a)


def _combine(x1, y, dest, w_tok, sw_gate, sw_up, sw_down, ln2_g, ln2_b, alpha, sub):
    tokens, d_model = x1.shape
    d_shared = sw_gate.shape[1]
    rows = 2 * COMBINE_ROWS
    assert tokens % rows == 0
    steps = tokens // rows
    row2 = lambda a: a.reshape(1, -1).astype(F32)
    return pl.pallas_call(
        functools.partial(_combine_kernel, alpha=alpha, sub=sub),
        grid=(steps,),
        in_specs=[pl.BlockSpec((TOP_K * rows,), lambda i: (i,), memory_space=pltpu.SMEM),
                  pl.BlockSpec((TOP_K * rows,), lambda i: (jnp.minimum(i + 1, steps - 1),), memory_space=pltpu.SMEM),
                  pl.BlockSpec((rows, TOP_K), lambda i: (i, 0)),
                  pl.BlockSpec((rows, d_model), lambda i: (i, 0)),
                  pl.BlockSpec(memory_space=pl.ANY),
                  _const_spec((d_model, d_shared)),
                  _const_spec((d_model, d_shared)),
                  _const_spec((d_shared, d_model)),
                  _const_spec((1, d_model)),
                  _const_spec((1, d_model))],
        out_specs=pl.BlockSpec((rows, d_model), lambda i: (i, 0)),
        out_shape=jax.ShapeDtypeStruct((tokens, d_model), F32),
        scratch_shapes=[pltpu.VMEM((TOP_K, COMBINE_ROWS * sub, V7X_LANES), y.dtype),
                        pltpu.VMEM((TOP_K, COMBINE_ROWS * sub, V7X_LANES), y.dtype),
                        pltpu.SemaphoreType.DMA(()), pltpu.SemaphoreType.DMA(())],
        compiler_params=pltpu.CompilerParams(dimension_semantics=("arbitrary",), vmem_limit_bytes=V7X_VMEM_LIMIT),
        name="combine",
    )(dest, dest, w_tok, x1, y, sw_gate.astype(BF16), sw_up.astype(BF16), sw_down.astype(BF16),
      row2(ln2_g), row2(ln2_b))


def _block_layout(counts, tokens):
    n_exp = counts.shape[0]
    padded = (counts + ROW_BLOCK - 1) // ROW_BLOCK * ROW_BLOCK
    pad_ends = jnp.cumsum(padded)
    pad_starts = pad_ends - padded
    n_blocks = -(-(tokens * TOP_K + n_exp * (ROW_BLOCK - 1)) // ROW_BLOCK)
    return padded, pad_ends, pad_starts, pad_starts // ROW_BLOCK, padded // ROW_BLOCK, n_blocks * ROW_BLOCK


def kernel(x, w_in, ret_norm_g, gla_gate_w2, gla_gate_b, gla_norm_g, w_ret_out, w_gla_out, w_o, ln1_g, ln1_b, router_w, router_bias, exp_w_gate, exp_w_up, exp_w_down, shared_w_gate, shared_w_up, shared_w_down, ln2_g, ln2_b):
    batch, seq, d_model = x.shape
    depth = w_in.shape[0]
    alpha = (2.0 * depth) ** 0.25
    for l in range(depth):
        x1, x1p = _mix(x, w_in[l], ret_norm_g[l], gla_gate_w2[l], gla_gate_b[l], gla_norm_g[l],
                       w_ret_out[l], w_gla_out[l], w_o[l], ln1_g[l], ln1_b[l], alpha)
        x1 = x1.reshape(batch * seq, d_model)
        sub = x1p.shape[1] // seq
        x1p = x1p.reshape(batch * seq * sub, V7X_LANES)
        idx, w_sel, rank, counts = _route(x1, router_w[l], router_bias[l])
        padded, pad_ends, pad_starts, first_block, n_block, n_rows = _block_layout(
            counts[:, 0].astype(jnp.int32), batch * seq)
        dest = _plan(idx, rank, pad_starts).T.reshape(-1)
        xs = _dispatch(x1p, dest, pad_ends, padded, n_rows, sub)
        y = _experts(xs, first_block, n_block, exp_w_gate[l], exp_w_up[l], exp_w_down[l], sub)
        out = _combine(x1, y, dest, w_sel.T, shared_w_gate[l], shared_w_up[l], shared_w_down[l],
                       ln2_g[l], ln2_b[l], alpha, sub)
        x = out.reshape(batch, seq, d_model)
    return x
```

```python
import functools

import jax
import jax.numpy as jnp
import numpy as np
from jax import lax
from jax.experimental import pallas as pl
from jax.experimental.pallas import tpu as pltpu

CHUNK = 64
RET_HEADS = 4
RET_DK = 128
RET_DV = 256
GLA_HEADS = 4
GLA_DK = 128
GLA_DV = 256
GLA_GATE_RANK = 16
GLA_GATE_TAU = 16.0
ROPE_THETA = 10000.0
N_EXPERTS = 256
TOP_K = 8
N_GROUPS = 8
TOPK_GROUPS = 4
ROUTED_SCALE = 2.5
LN_EPS = 1e-5
NORM_EPS = 1e-6

V7X_LANES = 128
V7X_VMEM_LIMIT = 60 * 1024 * 1024

MIX_ROWS = 256
MIX_SEQS = 2
ROUTE_COLS = 512
ROW_BLOCK = 128
DISPATCH_ROWS = 512
DMA_UNROLL = 2
EXPERT_GROUP = 8
EXPERT_AHEAD = 8
EXPERT_SLOTS = EXPERT_GROUP + EXPERT_AHEAD
COMBINE_ROWS = 256

F32 = jnp.float32
BF16 = jnp.bfloat16


def _dot(a, b):
    return jnp.dot(a, b, preferred_element_type=F32)


def _dot_nt(a, b):
    return lax.dot_general(a, b, (((1,), (1,)), ((), ())), preferred_element_type=F32)


def _dot_tn(a, b):
    return lax.dot_general(a, b, (((0,), (0,)), ((), ())), preferred_element_type=F32)


def _sigmoid(v):
    return 1.0 / (1.0 + jnp.exp(-v))


def _silu(v):
    return v * _sigmoid(v)


def _pack_rows(v):
    half = v.shape[1] // 2
    hi = lax.bitcast_convert_type(v[:, :half].astype(BF16).astype(F32), jnp.uint32)
    lo = lax.bitcast_convert_type(v[:, half:].astype(BF16).astype(F32), jnp.uint32)
    return hi | (lo >> 16)


def _unpack_rows(p):
    hi = lax.bitcast_convert_type(p & jnp.uint32(0xFFFF0000), F32)
    lo = lax.bitcast_convert_type(p << 16, F32)
    return hi, lo


def _store_token_rows(ref, packed, lead=()):
    m, width = packed.shape
    sub = width // V7X_LANES
    for c in range(sub):
        ref[lead + (pl.ds(c, m, stride=sub), slice(None))] = packed[:, c * V7X_LANES:(c + 1) * V7X_LANES]


def _load_token_rows(ref, m, lead=()):
    sub = ref.shape[-2] // m
    return [ref[lead + (pl.ds(c, m, stride=sub), slice(None))] for c in range(sub)]


def _layernorm_rows(v, g, b):
    mu = jnp.mean(v, axis=-1, keepdims=True)
    vc = v - mu
    var = jnp.mean(vc * vc, axis=-1, keepdims=True)
    return vc * lax.rsqrt(var + LN_EPS) * g + b


def _mix_kernel(x_ref, wrq_ref, wrk_ref, wrv_ref, wrg_ref, wgq_ref, wgk_ref, wgv_ref, wgg_ref, wga_ref, wmg_ref,
                cos_ref, sin_ref, dmask_ref, qdec_ref, kdec_ref, tri_ref, w2_ref, gb_ref, retg_ref, glag_ref,
                wro_ref, wgo_ref, wo_ref, ln1g_ref, ln1b_ref,
                out_ref, packed_ref, rstate_ref, gstate_ref, yret_ref, ygla_ref, *, block_decay, alpha):
    n_seq, rows, d_model = x_ref.shape

    @pl.when(pl.program_id(1) == 0)
    def _():
        rstate_ref[...] = jnp.zeros_like(rstate_ref)
        gstate_ref[...] = jnp.zeros_like(gstate_ref)

    x = x_ref[...].reshape(n_seq * rows, d_model)
    xb = x.astype(BF16)

    def proj(w_ref):
        return _dot(xb, w_ref[...])

    cos = cos_ref[...]
    sin = sin_ref[...]

    def rope(t):
        return t * cos + pltpu.roll(t, RET_DK // 2, 1) * sin

    rq = proj(wrq_ref)
    rk = proj(wrk_ref)
    rv = proj(wrv_ref)
    rg = proj(wrg_ref)
    seq_heads = [(s, h) for s in range(n_seq) for h in range(RET_HEADS)]
    ret_states = [rstate_ref[s * RET_HEADS + h] for s, h in seq_heads]
    for i, (s, h) in enumerate(seq_heads):
        sr = slice(s * rows, (s + 1) * rows)
        qk = slice(h * RET_DK, (h + 1) * RET_DK)
        vv = slice(h * RET_DV, (h + 1) * RET_DV)
        q = rope(rq[sr, qk])
        k = rope(rk[sr, qk])
        v = rv[sr, vv].astype(BF16)
        scores = _dot_nt(q.astype(BF16), k.astype(BF16)) * dmask_ref[h]
        o = _dot(scores.astype(BF16), v)
        state = ret_states[i]
        o = o + _dot((q * qdec_ref[h]).astype(BF16), state.astype(BF16))
        ret_states[i] = state * block_decay[h] + _dot_tn((k * kdec_ref[h]).astype(BF16), v)
        mu = jnp.mean(o, axis=-1, keepdims=True)
        oc = o - mu
        var = jnp.mean(oc * oc, axis=-1, keepdims=True)
        y = oc * lax.rsqrt(var + LN_EPS) * retg_ref[:, vv] * _silu(rg[sr, vv])
        yret_ref[sr, vv] = y.astype(BF16)
    for i, (s, h) in enumerate(seq_heads):
        rstate_ref[s * RET_HEADS + h] = ret_states[i]

    gq = proj(wgq_ref) * (GLA_DK ** -0.5)
    gk = proj(wgk_ref)
    gv = proj(wgv_ref)
    gg = proj(wgg_ref)
    ga = proj(wga_ref)
    z = _dot(ga.astype(BF16), w2_ref[...]) + gb_ref[...]
    log_a = (jnp.minimum(z, 0.0) - jnp.log1p(jnp.exp(-jnp.abs(z)))) * (1.0 / GLA_GATE_TAU)
    la_hi = log_a.astype(BF16)
    la_lo = (log_a - la_hi.astype(F32)).astype(BF16)
    tri = tri_ref[...]
    bcum = jnp.concatenate([_dot(tri, la_hi[s * rows:(s + 1) * rows]) + _dot(tri, la_lo[s * rows:(s + 1) * rows])
                            for s in range(n_seq)], axis=0)
    n_chunks = rows // CHUNK
    seq_heads = [(s, h) for s in range(n_seq) for h in range(GLA_HEADS)]
    gla_states = [gstate_ref[s * GLA_HEADS + h] for s, h in seq_heads]
    decays = []
    updates = []
    for c in range(n_seq * n_chunks):
        rs = slice(c * CHUNK, (c + 1) * CHUNK)
        b_end = bcum[(c + 1) * CHUNK - 1:(c + 1) * CHUNK, :]
        kt = (gk[rs, :] * jnp.exp(b_end - bcum[rs, :])).astype(BF16)
        decays.append(jnp.exp(b_end))
        updates.append([_dot_tn(gv[rs, h * GLA_DV:(h + 1) * GLA_DV].astype(BF16), kt[:, h * GLA_DK:(h + 1) * GLA_DK])
                        for h in range(GLA_HEADS)])
    for c in range(n_seq * n_chunks):
        rs = slice(c * CHUNK, (c + 1) * CHUNK)
        for h in range(GLA_HEADS):
            i = (c // n_chunks) * GLA_HEADS + h
            qk = slice(h * GLA_DK, (h + 1) * GLA_DK)
            vv = slice(h * GLA_DV, (h + 1) * GLA_DV)
            gla_states[i] = gla_states[i] * decays[c][:, qk] + updates[c][h]
            o = _dot_nt(gq[rs, qk].astype(BF16), gla_states[i].astype(BF16))
            ms = jnp.mean(o * o, axis=-1, keepdims=True)
            y = o * lax.rsqrt(ms + NORM_EPS) * glag_ref[:, vv] * _silu(gg[rs, vv])
            ygla_ref[rs, vv] = y.astype(BF16)
    for i, (s, h) in enumerate(seq_heads):
        gstate_ref[s * GLA_HEADS + h] = gla_states[i]

    u_ret = _dot(yret_ref[...], wro_ref[...])
    u_gla = _dot(ygla_ref[...], wgo_ref[...])
    gate = _sigmoid(proj(wmg_ref))
    merged = gate[:, :d_model] * u_ret + gate[:, d_model:] * u_gla
    mix = _dot(merged.astype(BF16), wo_ref[...])
    out = _layernorm_rows(alpha * x + mix, ln1g_ref[...], ln1b_ref[...])
    out_ref[...] = out.reshape(n_seq, rows, d_model)
    packed = _pack_rows(out)
    for s in range(n_seq):
        _store_token_rows(packed_ref, packed[s * rows:(s + 1) * rows], (s,))


def _mix_tables(seq, rows):
    half = RET_DK // 2
    inv = ROPE_THETA ** (-np.arange(half, dtype=np.float64) / half)
    ang = np.arange(seq, dtype=np.float64)[:, None] * inv[None, :]
    cos2 = np.concatenate([np.cos(ang), np.cos(ang)], axis=1)
    sin2 = np.concatenate([-np.sin(ang), np.sin(ang)], axis=1)
    log_g = np.log1p(-np.exp2(-5.0 - np.arange(RET_HEADS, dtype=np.float64)))
    j = np.arange(rows, dtype=np.float64)
    same_or_earlier_chunk = (j[None, :] // CHUNK) <= (j[:, None] // CHUNK)
    k_scale = RET_DK ** -0.5
    dmask = np.exp(log_g[:, None, None] * np.abs(j[:, None] - j[None, :])) * same_or_earlier_chunk[None] * k_scale
    qdec = np.exp(log_g[:, None] * (j[None, :] + 1.0))
    kdec = np.exp(log_g[:, None] * (rows - 1.0 - j[None, :])) * k_scale
    qdec = np.broadcast_to(qdec[:, :, None], (RET_HEADS, rows, RET_DK))
    kdec = np.broadcast_to(kdec[:, :, None], (RET_HEADS, rows, RET_DK))
    block_decay = tuple(float(v) for v in np.exp(log_g * rows))
    tri = ((j[None, :] <= j[:, None]) & ((j[None, :] // CHUNK) == (j[:, None] // CHUNK)))
    to = lambda a, dt: jnp.asarray(np.ascontiguousarray(a), dtype=dt)
    return (to(cos2, F32), to(sin2, F32), to(dmask, F32), to(qdec, F32), to(kdec, F32), to(tri, BF16)), block_decay


def _const_spec(shape):
    nd = len(shape)
    return pl.BlockSpec(shape, lambda *_: (0,) * nd, pipeline_mode=pl.Buffered(1))


def _mix(x, w_in, ret_norm_g, gla_gate_w2, gla_gate_b, gla_norm_g, w_ret_out, w_gla_out, w_o, ln1_g, ln1_b, alpha):
    batch, seq, d_model = x.shape
    rows = MIX_ROWS
    assert seq % rows == 0 and rows % CHUNK == 0
    ret_qk, ret_v = RET_HEADS * RET_DK, RET_HEADS * RET_DV
    gla_qk, gla_v = GLA_HEADS * GLA_DK, GLA_HEADS * GLA_DV
    splits = (ret_qk, ret_qk, ret_v, ret_v, gla_qk, gla_qk, gla_v, gla_v, GLA_GATE_RANK, 2 * d_model)
    assert w_in.shape == (d_model, sum(splits))
    offs = np.cumsum((0,) + splits)
    parts = [w_in[:, offs[i]:offs[i + 1]].astype(BF16) for i in range(len(splits))]
    parts[8] = jnp.pad(parts[8], ((0, 0), (0, V7X_LANES - GLA_GATE_RANK)))
    w2 = jnp.pad(gla_gate_w2.astype(BF16), ((0, V7X_LANES - GLA_GATE_RANK), (0, 0)))
    (cos2, sin2, dmask, qdec, kdec, tri), block_decay = _mix_tables(seq, rows)
    row2 = lambda a: a.reshape(1, -1).astype(F32)
    consts = [dmask, qdec, kdec, tri, w2, row2(gla_gate_b), row2(ret_norm_g), row2(gla_norm_g),
              w_ret_out.astype(BF16), w_gla_out.astype(BF16), w_o.astype(BF16), row2(ln1_g), row2(ln1_b)]
    sub = d_model // 2 // V7X_LANES
    pos_spec = pl.BlockSpec((rows, RET_DK), lambda b, s: (s, 0))
    n_seq = MIX_SEQS
    assert batch % n_seq == 0
    in_specs = ([pl.BlockSpec((n_seq, rows, d_model), lambda b, s: (b, s, 0))]
                + [_const_spec(p.shape) for p in parts]
                + [pos_spec, pos_spec]
                + [_const_spec(c.shape) for c in consts])
    return pl.pallas_call(
        functools.partial(_mix_kernel, block_decay=block_decay, alpha=alpha),
        grid=(batch // n_seq, seq // rows),
        in_specs=in_specs,
        out_specs=[pl.BlockSpec((n_seq, rows, d_model), lambda b, s: (b, s, 0)),
                   pl.BlockSpec((n_seq, rows * sub, V7X_LANES), lambda b, s: (b, s, 0))],
        out_shape=[jax.ShapeDtypeStruct((batch, seq, d_model), F32),
                   jax.ShapeDtypeStruct((batch, seq * sub, V7X_LANES), jnp.uint32)],
        scratch_shapes=[pltpu.VMEM((n_seq * RET_HEADS, RET_DK, RET_DV), F32),
                        pltpu.VMEM((n_seq * GLA_HEADS, GLA_DV, GLA_DK), F32),
                        pltpu.VMEM((n_seq * rows, ret_v), BF16),
                        pltpu.VMEM((n_seq * rows, gla_v), BF16)],
        compiler_params=pltpu.CompilerParams(dimension_semantics=("arbitrary", "arbitrary"),
                                             vmem_limit_bytes=V7X_VMEM_LIMIT),
        name="mix",
    )(x, *parts, cos2, sin2, *consts)


def _route_kernel(x_ref, rwt_ref, bias_ref, triu_ref, ones_ref,
                  idx_ref, w_ref, rank_ref, counts_ref, carry_ref):
    cols = x_ref.shape[0]
    n_exp = rwt_ref.shape[0]
    per_group = n_exp // N_GROUPS
    neg_inf = -jnp.inf

    @pl.when(pl.program_id(0) == 0)
    def _():
        carry_ref[...] = jnp.zeros_like(carry_ref)

    logits = _dot_nt(rwt_ref[...], x_ref[...].astype(BF16))
    scores = _sigmoid(logits)
    biased = scores + bias_ref[...]

    sub = lax.broadcasted_iota(jnp.int32, (per_group, cols), 0)
    gscore = []
    for g in range(N_GROUPS):
        blk = biased[g * per_group:(g + 1) * per_group, :]
        m1 = jnp.max(blk, axis=0, keepdims=True)
        i1 = jnp.min(jnp.where(blk == m1, sub, per_group), axis=0, keepdims=True)
        m2 = jnp.max(jnp.where(sub == i1, neg_inf, blk), axis=0, keepdims=True)
        gscore.append(m1 + m2)
    masked = []
    for g in range(N_GROUPS):
        ahead = jnp.zeros((1, cols), jnp.int32)
        for o in range(N_GROUPS):
            if o == g:
                continue
            before = (gscore[o] >= gscore[g]) if o < g else (gscore[o] > gscore[g])
            ahead = ahead + before.astype(jnp.int32)
        keep = ahead < TOPK_GROUPS
        blk = biased[g * per_group:(g + 1) * per_group, :]
        masked.append(jnp.where(keep, blk, neg_inf))
    candidates = jnp.concatenate(masked, axis=0)
    cur = candidates

    rowid = lax.broadcasted_iota(jnp.int32, (n_exp, cols), 0)
    picked = []
    weights = []
    for _ in range(TOP_K):
        m = jnp.max(cur, axis=0, keepdims=True)
        ik = jnp.min(jnp.where(cur == m, rowid, n_exp), axis=0, keepdims=True)
        sel = rowid == ik
        weights.append(jnp.sum(jnp.where(sel, scores, 0.0), axis=0, keepdims=True))
        cur = jnp.where(sel, neg_inf, cur)
        picked.append(ik)
    wsum = weights[0]
    for wk in weights[1:]:
        wsum = wsum + wk

    chosen = (cur == neg_inf) & (candidates != neg_inf)
    chosen_b = jnp.where(chosen, 1.0, 0.0).astype(BF16)
    carry = carry_ref[...]
    before = _dot(chosen_b, triu_ref[...]) + jnp.concatenate([carry] * (cols // V7X_LANES), axis=1)
    for k in range(TOP_K):
        sel = rowid == picked[k]
        rank_ref[k:k + 1, :] = jnp.sum(jnp.where(sel, before, 0.0), axis=0, keepdims=True).astype(jnp.int32)
        idx_ref[k:k + 1, :] = picked[k]
        w_ref[k:k + 1, :] = weights[k] / wsum * ROUTED_SCALE
    carry = carry + _dot(chosen_b, ones_ref[...])
    carry_ref[...] = carry
    counts_ref[...] = carry


def _route(x1, router_w, router_bias):
    tokens, d_model = x1.shape
    n_exp = router_w.shape[1]
    cols = ROUTE_COLS
    assert tokens % cols == 0 and n_exp % N_GROUPS == 0
    j = np.arange(cols)
    triu = jnp.asarray((j[:, None] < j[None, :]), dtype=BF16)
    ones = jnp.ones((cols, V7X_LANES), BF16)
    out_row = lambda dt: jax.ShapeDtypeStruct((TOP_K, tokens), dt)
    row_spec = pl.BlockSpec((TOP_K, cols), lambda i: (0, i))
    return pl.pallas_call(
        _route_kernel,
        grid=(tokens // cols,),
        in_specs=[pl.BlockSpec((cols, d_model), lambda i: (i, 0)),
                  _const_spec((n_exp, d_model)),
                  _const_spec((n_exp, 1)),
                  _const_spec((cols, cols)),
                  _const_spec((cols, V7X_LANES))],
        out_specs=[row_spec, row_spec, row_spec, pl.BlockSpec((n_exp, V7X_LANES), lambda i: (0, 0))],
        out_shape=[out_row(jnp.int32), out_row(F32), out_row(jnp.int32),
                   jax.ShapeDtypeStruct((n_exp, V7X_LANES), F32)],
        scratch_shapes=[pltpu.VMEM((n_exp, V7X_LANES), F32)],
        compiler_params=pltpu.CompilerParams(dimension_semantics=("arbitrary",), vmem_limit_bytes=V7X_VMEM_LIMIT),
        name="route",
    )(x1, router_w.T.astype(BF16), router_bias.reshape(n_exp, 1).astype(F32), triu, ones)


def _plan_kernel(idx_ref, rank_ref, pstart_ref, dest_ref):
    n_exp = pstart_ref.shape[0]
    cols = idx_ref.shape[1]
    rowid = lax.broadcasted_iota(jnp.int32, (n_exp, cols), 0)
    pstart = pstart_ref[...]
    for k in range(TOP_K):
        base = jnp.sum(jnp.where(rowid == idx_ref[k:k + 1, :], pstart, 0.0), axis=0, keepdims=True)
        dest_ref[k:k + 1, :] = base.astype(jnp.int32) + rank_ref[k:k + 1, :]


def _plan(idx, rank, pad_starts):
    tokens = idx.shape[1]
    n_exp = pad_starts.shape[0]
    cols = ROUTE_COLS
    row_spec = pl.BlockSpec((TOP_K, cols), lambda i: (0, i))
    return pl.pallas_call(
        _plan_kernel,
        grid=(tokens // cols,),
        in_specs=[row_spec, row_spec, _const_spec((n_exp, 1))],
        out_specs=row_spec,
        out_shape=jax.ShapeDtypeStruct((TOP_K, tokens), jnp.int32),
        compiler_params=pltpu.CompilerParams(dimension_semantics=("arbitrary",)),
        name="plan",
    )(idx, rank, pad_starts.reshape(n_exp, 1).astype(F32))


def _token_rows(tok, sub):
    return pl.ds(pl.multiple_of(tok * sub, sub), sub)


def _dispatch_kernel(pad_end_ref, padded_ref, dest_ref, x_ref, wg_ref, wu_ref, wd_ref,
                     xs_hbm, wgu_ref, wdb_ref, zero_ref, sem, *, sub):
    rows = x_ref.shape[0] // sub
    n_exp = pad_end_ref.shape[0]

    def zero_copy(e):
        start = pl.multiple_of((pad_end_ref[e] - ROW_BLOCK) * sub, ROW_BLOCK * sub)
        return pltpu.make_async_copy(zero_ref, xs_hbm.at[pl.ds(start, ROW_BLOCK * sub), :], sem)

    @pl.when(pl.program_id(0) == 0)
    def _():
        zero_ref[...] = jnp.zeros_like(zero_ref)

        def issue(e, c):
            @pl.when(padded_ref[e] > 0)
            def _():
                zero_copy(e).start()
            return c
        lax.fori_loop(0, n_exp, issue, 0)

        def drain(e, c):
            @pl.when(padded_ref[e] > 0)
            def _():
                zero_copy(e).wait()
            return c
        lax.fori_loop(0, n_exp, drain, 0)

    def row_copy(k, r):
        return pltpu.make_async_copy(x_ref.at[_token_rows(r, sub), :],
                                     xs_hbm.at[_token_rows(dest_ref[r * TOP_K + k], sub), :], sem)

    def issue(g, c):
        for u in range(DMA_UNROLL):
            for k in range(TOP_K):
                row_copy(k, g * DMA_UNROLL + u).start(priority=k % 2)
        return c
    lax.fori_loop(0, rows // DMA_UNROLL, issue, 0)

    d_exp = wg_ref.shape[2]
    for j in range(wg_ref.shape[0]):
        wgu_ref[j, :, :d_exp] = wg_ref[j].astype(BF16)
        wgu_ref[j, :, d_exp:] = wu_ref[j].astype(BF16)
        wdb_ref[j] = wd_ref[j].astype(BF16)

    def drain(g, c):
        for u in range(DMA_UNROLL):
            for k in range(TOP_K):
                row_copy(k, g * DMA_UNROLL + u).wait()
        return c
    lax.fori_loop(0, rows // DMA_UNROLL, drain, 0)


def _dispatch(x1p, dest, pad_ends, padded, n_rows, sub, w_gate, w_up, w_down):
    tokens = x1p.shape[0] // sub
    rows = DISPATCH_ROWS
    steps = tokens // rows
    n_exp, d_model, d_exp = w_gate.shape
    assert n_exp % steps == 0
    per_step = n_exp // steps
    w_spec = lambda shape: pl.BlockSpec((per_step,) + shape, lambda i, pe, pd: (i, 0, 0))
    grid_spec = pltpu.PrefetchScalarGridSpec(
        num_scalar_prefetch=2,
        grid=(steps,),
        in_specs=[pl.BlockSpec((TOP_K * rows,), lambda i, pe, pd: (i,), memory_space=pltpu.SMEM),
                  pl.BlockSpec((rows * sub, V7X_LANES), lambda i, pe, pd: (i, 0)),
                  w_spec((d_model, d_exp)), w_spec((d_model, d_exp)), w_spec((d_exp, d_model))],
        out_specs=[pl.BlockSpec(memory_space=pl.ANY), w_spec((d_model, 2 * d_exp)), w_spec((d_exp, d_model))],
        scratch_shapes=[pltpu.VMEM((ROW_BLOCK * sub, V7X_LANES), x1p.dtype), pltpu.SemaphoreType.DMA(())],
    )
    return pl.pallas_call(
        functools.partial(_dispatch_kernel, sub=sub),
        grid_spec=grid_spec,
        out_shape=[jax.ShapeDtypeStruct((n_rows * sub, V7X_LANES), x1p.dtype),
                   jax.ShapeDtypeStruct((n_exp, d_model, 2 * d_exp), BF16),
                   jax.ShapeDtypeStruct((n_exp, d_exp, d_model), BF16)],
        compiler_params=pltpu.CompilerParams(dimension_semantics=("arbitrary",), has_side_effects=True,
                                             vmem_limit_bytes=V7X_VMEM_LIMIT),
        name="dispatch",
    )(pad_ends, padded, dest, x1p, w_gate, w_up, w_down)


def _experts_kernel(first_ref, nblk_ref, total_ref, xs_hbm, wgu_b, wd_b, y_hbm,
                    xbuf_ref, ybuf_ref, in_sem, out_sem, *, sub):
    e = pl.program_id(0)
    n = nblk_ref[e]
    first = first_ref[e]
    total = total_ref[0]
    ahead = EXPERT_AHEAD
    block = ROW_BLOCK * sub

    def block_rows(g):
        return pl.ds(pl.multiple_of(g * block, block), block)

    def in_copy(g):
        slot = lax.rem(g, EXPERT_SLOTS)
        return pltpu.make_async_copy(xs_hbm.at[block_rows(g), :], xbuf_ref.at[slot], in_sem.at[slot])

    def out_copy(g):
        slot = lax.rem(g, EXPERT_SLOTS)
        return pltpu.make_async_copy(ybuf_ref.at[slot], y_hbm.at[block_rows(g), :], out_sem.at[slot])

    @pl.when(e == 0)
    def _():
        for g in range(ahead):
            @pl.when(g < total)
            def _():
                in_copy(g).start()

    @pl.when(n > 0)
    def _():
        d_exp = wd_b.shape[0]

        def stage(g):
            slot = lax.rem(g, EXPERT_SLOTS)
            in_copy(g).wait()

            @pl.when(g + ahead < total)
            def _():
                in_copy(g + ahead).start()

            @pl.when(g >= EXPERT_SLOTS)
            def _():
                out_copy(g - EXPERT_SLOTS).wait()

            halves = [_unpack_rows(p) for p in _load_token_rows(xbuf_ref, ROW_BLOCK, (slot,))]
            return jnp.concatenate([h for h, _ in halves] + [l for _, l in halves], axis=1).astype(BF16)

        def swiglu(xb):
            gate_up = _dot(xb, wgu_b[...])
            hidden = _silu(gate_up[:, :d_exp]) * gate_up[:, d_exp:]
            return _pack_rows(_dot(hidden.astype(BF16), wd_b[...]))

        def finish(g, packed):
            _store_token_rows(ybuf_ref, packed, (lax.rem(g, EXPERT_SLOTS),))
            out_copy(g).start()

        def group(g, count):
            xbs = [stage(g + i) for i in range(count)]
            y = swiglu(xbs[0] if count == 1 else jnp.concatenate(xbs, axis=0))
            for i in range(count):
                finish(g + i, y[i * ROW_BLOCK:(i + 1) * ROW_BLOCK])

        def full_groups(q, c):
            group(first + EXPERT_GROUP * q, EXPERT_GROUP)
            return c
        lax.fori_loop(0, n // EXPERT_GROUP, full_groups, 0)

        done = n - lax.rem(n, EXPERT_GROUP)
        size = EXPERT_GROUP // 2
        while size >= 1:
            take = lax.rem(n, 2 * size) >= size

            @pl.when(take)
            def _(done=done, size=size):
                group(first + done, size)
            done = done + jnp.where(take, size, 0)
            size //= 2

    @pl.when(e == pl.num_programs(0) - 1)
    def _():
        for d in range(EXPERT_SLOTS):
            @pl.when(total - 1 - d >= 0)
            def _():
                out_copy(total - 1 - d).wait()


def _experts(xs, first_block, n_block, w_gate_up, w_down, sub):
    n_exp, d_exp, d_model = w_down.shape
    assert sub * V7X_LANES * 2 == d_model and xs.shape[0] % (ROW_BLOCK * sub) == 0
    total = jnp.sum(n_block).reshape(1).astype(jnp.int32)
    grid_spec = pltpu.PrefetchScalarGridSpec(
        num_scalar_prefetch=3,
        grid=(n_exp,),
        in_specs=[pl.BlockSpec(memory_space=pl.ANY),
                  pl.BlockSpec((None, d_model, 2 * d_exp), lambda e, fb, nb, tt: (e, 0, 0)),
                  pl.BlockSpec((None, d_exp, d_model), lambda e, fb, nb, tt: (e, 0, 0))],
        out_specs=pl.BlockSpec(memory_space=pl.ANY),
        scratch_shapes=[pltpu.VMEM((EXPERT_SLOTS, ROW_BLOCK * sub, V7X_LANES), xs.dtype),
                        pltpu.VMEM((EXPERT_SLOTS, ROW_BLOCK * sub, V7X_LANES), xs.dtype),
                        pltpu.SemaphoreType.DMA((EXPERT_SLOTS,)),
                        pltpu.SemaphoreType.DMA((EXPERT_SLOTS,))],
    )
    return pl.pallas_call(
        functools.partial(_experts_kernel, sub=sub),
        grid_spec=grid_spec,
        out_shape=jax.ShapeDtypeStruct(xs.shape, xs.dtype),
        compiler_params=pltpu.CompilerParams(dimension_semantics=("arbitrary",), vmem_limit_bytes=V7X_VMEM_LIMIT,
                                             has_side_effects=True),
        name="experts",
    )(first_block, n_block, total, xs, w_gate_up, w_down)


def _combine_kernel(dest_ref, next_ref, wt_ref, x_ref, y_hbm, sg_ref, su_ref, sd_ref, g_ref, b_ref,
                    out_ref, ybuf_a, ybuf_b, sem_a, sem_b, *, alpha, sub):
    rows = x_ref.shape[0] // 2
    step = pl.program_id(0)

    def row_copy(idx_ref, tok, k, r, buf, sem):
        return pltpu.make_async_copy(y_hbm.at[_token_rows(idx_ref[tok * TOP_K + k], sub), :],
                                     buf.at[k, _token_rows(r, sub), :], sem)

    def issue_rolled(idx_ref, first_tok, buf, sem):
        def body(g, c):
            for u in range(DMA_UNROLL):
                for k in range(TOP_K):
                    r = g * DMA_UNROLL + u
                    row_copy(idx_ref, first_tok + r, k, r, buf, sem).start(priority=k % 2)
            return c
        lax.fori_loop(0, rows // DMA_UNROLL, body, 0)

    def issue_inline(idx_ref, first_tok, buf, sem):
        for r in range(rows):
            for k in range(TOP_K):
                row_copy(idx_ref, first_tok + r, k, r, buf, sem).start(priority=k % 2)

    def drain(buf, sem):
        def body(g, c):
            for u in range(DMA_UNROLL):
                for k in range(TOP_K):
                    r = g * DMA_UNROLL + u
                    pltpu.make_async_copy(y_hbm.at[_token_rows(0, sub), :],
                                          buf.at[k, _token_rows(r, sub), :], sem).wait()
            return c
        lax.fori_loop(0, rows // DMA_UNROLL, body, 0)

    def reduce_block(first_tok, buf):
        rs = slice(first_tok, first_tok + rows)
        x = x_ref[rs, :]
        xb = x.astype(BF16)
        hidden = _silu(_dot(xb, sg_ref[...])) * _dot(xb, su_ref[...])
        acc = alpha * x + _dot(hidden.astype(BF16), sd_ref[...])
        wt = wt_ref[rs, :]
        chunks = [acc[:, c * V7X_LANES:(c + 1) * V7X_LANES] for c in range(2 * sub)]
        for k in range(TOP_K):
            wk = wt[:, k:k + 1]
            for c, p in enumerate(_load_token_rows(buf, rows, (k,))):
                hi, lo = _unpack_rows(p)
                chunks[c] = chunks[c] + hi * wk
                chunks[sub + c] = chunks[sub + c] + lo * wk
        out_ref[rs, :] = _layernorm_rows(jnp.concatenate(chunks, axis=1), g_ref[...], b_ref[...])

    @pl.when(step == 0)
    def _():
        issue_rolled(dest_ref, 0, ybuf_a, sem_a)

    drain(ybuf_a, sem_a)
    issue_inline(dest_ref, rows, ybuf_b, sem_b)
    reduce_block(0, ybuf_a)
    drain(ybuf_b, sem_b)
    issue_inline(next_ref, 0, ybuf_a, sem_a)
    reduce_block(rows, ybuf_b)

    @pl.when(step == pl.num_programs(0) - 1)
    def _():
        drain(ybuf_a, sem_a)


def _combine(x1, y, dest, w_tok, sw_gate, sw_up, sw_down, ln2_g, ln2_b, alpha, sub):
    tokens, d_model = x1.shape
    d_shared = sw_gate.shape[1]
    rows = 2 * COMBINE_ROWS
    assert tokens % rows == 0
    steps = tokens // rows
    row2 = lambda a: a.reshape(1, -1).astype(F32)
    return pl.pallas_call(
        functools.partial(_combine_kernel, alpha=alpha, sub=sub),
        grid=(steps,),
        in_specs=[pl.BlockSpec((TOP_K * rows,), lambda i: (i,), memory_space=pltpu.SMEM),
                  pl.BlockSpec((TOP_K * rows,), lambda i: (jnp.minimum(i + 1, steps - 1),), memory_space=pltpu.SMEM),
                  pl.BlockSpec((rows, TOP_K), lambda i: (i, 0)),
                  pl.BlockSpec((rows, d_model), lambda i: (i, 0)),
                  pl.BlockSpec(memory_space=pl.ANY),
                  _const_spec((d_model, d_shared)),
                  _const_spec((d_model, d_shared)),
                  _const_spec((d_shared, d_model)),
                  _const_spec((1, d_model)),
                  _const_spec((1, d_model))],
        out_specs=pl.BlockSpec((rows, d_model), lambda i: (i, 0)),
        out_shape=jax.ShapeDtypeStruct((tokens, d_model), F32),
        scratch_shapes=[pltpu.VMEM((TOP_K, COMBINE_ROWS * sub, V7X_LANES), y.dtype),
                        pltpu.VMEM((TOP_K, COMBINE_ROWS * sub, V7X_LANES), y.dtype),
                        pltpu.SemaphoreType.DMA(()), pltpu.SemaphoreType.DMA(())],
        compiler_params=pltpu.CompilerParams(dimension_semantics=("arbitrary",), vmem_limit_bytes=V7X_VMEM_LIMIT),
        name="combine",
    )(dest, dest, w_tok, x1, y, sw_gate.astype(BF16), sw_up.astype(BF16), sw_down.astype(BF16),
      row2(ln2_g), row2(ln2_b))


def _block_layout(counts, tokens):
    n_exp = counts.shape[0]
    padded = (counts + ROW_BLOCK - 1) // ROW_BLOCK * ROW_BLOCK
    pad_ends = jnp.cumsum(padded)
    pad_starts = pad_ends - padded
    n_blocks = -(-(tokens * TOP_K + n_exp * (ROW_BLOCK - 1)) // ROW_BLOCK)
    return padded, pad_ends, pad_starts, pad_starts // ROW_BLOCK, padded // ROW_BLOCK, n_blocks * ROW_BLOCK


def kernel(x, w_in, ret_norm_g, gla_gate_w2, gla_gate_b, gla_norm_g, w_ret_out, w_gla_out, w_o, ln1_g, ln1_b, router_w, router_bias, exp_w_gate, exp_w_up, exp_w_down, shared_w_gate, shared_w_up, shared_w_down, ln2_g, ln2_b):
    batch, seq, d_model = x.shape
    depth = w_in.shape[0]
    alpha = (2.0 * depth) ** 0.25
    for l in range(depth):
        x1, x1p = _mix(x, w_in[l], ret_norm_g[l], gla_gate_w2[l], gla_gate_b[l], gla_norm_g[l],
                       w_ret_out[l], w_gla_out[l], w_o[l], ln1_g[l], ln1_b[l], alpha)
        x1 = x1.reshape(batch * seq, d_model)
        sub = x1p.shape[1] // seq
        x1p = x1p.reshape(batch * seq * sub, V7X_LANES)
        idx, w_sel, rank, counts = _route(x1, router_w[l], router_bias[l])
        padded, pad_ends, pad_starts, first_block, n_block, n_rows = _block_layout(
            counts[:, 0].astype(jnp.int32), batch * seq)
        dest = _plan(idx, rank, pad_starts).T.reshape(-1)
        xs, w_gate_up, w_down = _dispatch(x1p, dest, pad_ends, padded, n_rows, sub,
                                          exp_w_gate[l], exp_w_up[l], exp_w_down[l])
        y = _experts(xs, first_block, n_block, w_gate_up, w_down, sub)
        out = _combine(x1, y, dest, w_sel.T, shared_w_gate[l], shared_w_up[l], shared_w_down[l],
                       ln2_g[l], ln2_b[l], alpha, sub)
        x = out.reshape(batch, seq, d_model)
    return x
```

```python
import functools

import jax
import jax.numpy as jnp
import numpy as np
from jax import lax
from jax.experimental import pallas as pl
from jax.experimental.pallas import tpu as pltpu

CHUNK = 64
RET_HEADS = 4
RET_DK = 128
RET_DV = 256
GLA_HEADS = 4
GLA_DK = 128
GLA_DV = 256
GLA_GATE_RANK = 16
GLA_GATE_TAU = 16.0
ROPE_THETA = 10000.0
N_EXPERTS = 256
TOP_K = 8
N_GROUPS = 8
TOPK_GROUPS = 4
ROUTED_SCALE = 2.5
LN_EPS = 1e-5
NORM_EPS = 1e-6

V7X_LANES = 128
V7X_VMEM_LIMIT = 60 * 1024 * 1024

MIX_ROWS = 256
MIX_SEQS = 2
ROUTE_COLS = 512
ROW_BLOCK = 128
DISPATCH_ROWS = 512
DMA_UNROLL = 2
EXPERTS_PER_STEP = 4
EXPERT_GROUP = 8
EXPERT_AHEAD = 8
EXPERT_SLOTS = EXPERT_GROUP + EXPERT_AHEAD
COMBINE_ROWS = 256

F32 = jnp.float32
BF16 = jnp.bfloat16


def _dot(a, b):
    return jnp.dot(a, b, preferred_element_type=F32)


def _dot_nt(a, b):
    return lax.dot_general(a, b, (((1,), (1,)), ((), ())), preferred_element_type=F32)


def _dot_tn(a, b):
    return lax.dot_general(a, b, (((0,), (0,)), ((), ())), preferred_element_type=F32)


def _sigmoid(v):
    return 1.0 / (1.0 + jnp.exp(-v))


def _silu(v):
    return v * _sigmoid(v)


def _pack_rows(v):
    half = v.shape[1] // 2
    hi = lax.bitcast_convert_type(v[:, :half].astype(BF16).astype(F32), jnp.uint32)
    lo = lax.bitcast_convert_type(v[:, half:].astype(BF16).astype(F32), jnp.uint32)
    return hi | (lo >> 16)


def _unpack_rows(p):
    hi = lax.bitcast_convert_type(p & jnp.uint32(0xFFFF0000), F32)
    lo = lax.bitcast_convert_type(p << 16, F32)
    return hi, lo


def _store_token_rows(ref, packed, lead=()):
    m, width = packed.shape
    sub = width // V7X_LANES
    for c in range(sub):
        ref[lead + (pl.ds(c, m, stride=sub), slice(None))] = packed[:, c * V7X_LANES:(c + 1) * V7X_LANES]


def _load_token_rows(ref, m, lead=()):
    sub = ref.shape[-2] // m
    return [ref[lead + (pl.ds(c, m, stride=sub), slice(None))] for c in range(sub)]


def _layernorm_rows(v, g, b):
    mu = jnp.mean(v, axis=-1, keepdims=True)
    vc = v - mu
    var = jnp.mean(vc * vc, axis=-1, keepdims=True)
    return vc * lax.rsqrt(var + LN_EPS) * g + b


def _mix_kernel(x_ref, wrq_ref, wrk_ref, wrv_ref, wrg_ref, wgq_ref, wgk_ref, wgv_ref, wgg_ref, wga_ref, wmg_ref,
                cos_ref, sin_ref, dmask_ref, qdec_ref, kdec_ref, tri_ref, w2_ref, gb_ref, retg_ref, glag_ref,
                wro_ref, wgo_ref, wo_ref, ln1g_ref, ln1b_ref,
                out_ref, packed_ref, rstate_ref, gstate_ref, yret_ref, ygla_ref, *, block_decay, alpha):
    n_seq, rows, d_model = x_ref.shape

    @pl.when(pl.program_id(1) == 0)
    def _():
        rstate_ref[...] = jnp.zeros_like(rstate_ref)
        gstate_ref[...] = jnp.zeros_like(gstate_ref)

    x = x_ref[...].reshape(n_seq * rows, d_model)
    xb = x.astype(BF16)

    def proj(w_ref):
        return _dot(xb, w_ref[...])

    cos = cos_ref[...]
    sin = sin_ref[...]

    def rope(t):
        return t * cos + pltpu.roll(t, RET_DK // 2, 1) * sin

    rq = proj(wrq_ref)
    rk = proj(wrk_ref)
    rv = proj(wrv_ref)
    rg = proj(wrg_ref)
    seq_heads = [(s, h) for s in range(n_seq) for h in range(RET_HEADS)]
    ret_states = [rstate_ref[s * RET_HEADS + h] for s, h in seq_heads]
    for i, (s, h) in enumerate(seq_heads):
        sr = slice(s * rows, (s + 1) * rows)
        qk = slice(h * RET_DK, (h + 1) * RET_DK)
        vv = slice(h * RET_DV, (h + 1) * RET_DV)
        q = rope(rq[sr, qk])
        k = rope(rk[sr, qk])
        v = rv[sr, vv].astype(BF16)
        scores = _dot_nt(q.astype(BF16), k.astype(BF16)) * dmask_ref[h]
        o = _dot(scores.astype(BF16), v)
        state = ret_states[i]
        o = o + _dot((q * qdec_ref[h]).astype(BF16), state.astype(BF16))
        ret_states[i] = state * block_decay[h] + _dot_tn((k * kdec_ref[h]).astype(BF16), v)
        mu = jnp.mean(o, axis=-1, keepdims=True)
        oc = o - mu
        var = jnp.mean(oc * oc, axis=-1, keepdims=True)
        y = oc * lax.rsqrt(var + LN_EPS) * retg_ref[:, vv] * _silu(rg[sr, vv])
        yret_ref[sr, vv] = y.astype(BF16)
    for i, (s, h) in enumerate(seq_heads):
        rstate_ref[s * RET_HEADS + h] = ret_states[i]

    gq = proj(wgq_ref) * (GLA_DK ** -0.5)
    gk = proj(wgk_ref)
    gv = proj(wgv_ref)
    gg = proj(wgg_ref)
    ga = proj(wga_ref)
    z = _dot(ga.astype(BF16), w2_ref[...]) + gb_ref[...]
    log_a = (jnp.minimum(z, 0.0) - jnp.log1p(jnp.exp(-jnp.abs(z)))) * (1.0 / GLA_GATE_TAU)
    la_hi = log_a.astype(BF16)
    la_lo = (log_a - la_hi.astype(F32)).astype(BF16)
    tri = tri_ref[...]
    bcum = jnp.concatenate([_dot(tri, la_hi[s * rows:(s + 1) * rows]) + _dot(tri, la_lo[s * rows:(s + 1) * rows])
                            for s in range(n_seq)], axis=0)
    n_chunks = rows // CHUNK
    seq_heads = [(s, h) for s in range(n_seq) for h in range(GLA_HEADS)]
    gla_states = [gstate_ref[s * GLA_HEADS + h] for s, h in seq_heads]
    decays = []
    updates = []
    for c in range(n_seq * n_chunks):
        rs = slice(c * CHUNK, (c + 1) * CHUNK)
        b_end = bcum[(c + 1) * CHUNK - 1:(c + 1) * CHUNK, :]
        kt = (gk[rs, :] * jnp.exp(b_end - bcum[rs, :])).astype(BF16)
        decays.append(jnp.exp(b_end))
        updates.append([_dot_tn(gv[rs, h * GLA_DV:(h + 1) * GLA_DV].astype(BF16), kt[:, h * GLA_DK:(h + 1) * GLA_DK])
                        for h in range(GLA_HEADS)])
    for c in range(n_seq * n_chunks):
        rs = slice(c * CHUNK, (c + 1) * CHUNK)
        for h in range(GLA_HEADS):
            i = (c // n_chunks) * GLA_HEADS + h
            qk = slice(h * GLA_DK, (h + 1) * GLA_DK)
            vv = slice(h * GLA_DV, (h + 1) * GLA_DV)
            gla_states[i] = gla_states[i] * decays[c][:, qk] + updates[c][h]
            o = _dot_nt(gq[rs, qk].astype(BF16), gla_states[i].astype(BF16))
            ms = jnp.mean(o * o, axis=-1, keepdims=True)
            y = o * lax.rsqrt(ms + NORM_EPS) * glag_ref[:, vv] * _silu(gg[rs, vv])
            ygla_ref[rs, vv] = y.astype(BF16)
    for i, (s, h) in enumerate(seq_heads):
        gstate_ref[s * GLA_HEADS + h] = gla_states[i]

    u_ret = _dot(yret_ref[...], wro_ref[...])
    u_gla = _dot(ygla_ref[...], wgo_ref[...])
    gate = _sigmoid(proj(wmg_ref))
    merged = gate[:, :d_model] * u_ret + gate[:, d_model:] * u_gla
    mix = _dot(merged.astype(BF16), wo_ref[...])
    out = _layernorm_rows(alpha * x + mix, ln1g_ref[...], ln1b_ref[...])
    out_ref[...] = out.reshape(n_seq, rows, d_model)
    packed = _pack_rows(out)
    for s in range(n_seq):
        _store_token_rows(packed_ref, packed[s * rows:(s + 1) * rows], (s,))


def _mix_tables(seq, rows):
    half = RET_DK // 2
    inv = ROPE_THETA ** (-np.arange(half, dtype=np.float64) / half)
    ang = np.arange(seq, dtype=np.float64)[:, None] * inv[None, :]
    cos2 = np.concatenate([np.cos(ang), np.cos(ang)], axis=1)
    sin2 = np.concatenate([-np.sin(ang), np.sin(ang)], axis=1)
    log_g = np.log1p(-np.exp2(-5.0 - np.arange(RET_HEADS, dtype=np.float64)))
    j = np.arange(rows, dtype=np.float64)
    same_or_earlier_chunk = (j[None, :] // CHUNK) <= (j[:, None] // CHUNK)
    k_scale = RET_DK ** -0.5
    dmask = np.exp(log_g[:, None, None] * np.abs(j[:, None] - j[None, :])) * same_or_earlier_chunk[None] * k_scale
    qdec = np.exp(log_g[:, None] * (j[None, :] + 1.0))
    kdec = np.exp(log_g[:, None] * (rows - 1.0 - j[None, :])) * k_scale
    qdec = np.broadcast_to(qdec[:, :, None], (RET_HEADS, rows, RET_DK))
    kdec = np.broadcast_to(kdec[:, :, None], (RET_HEADS, rows, RET_DK))
    block_decay = tuple(float(v) for v in np.exp(log_g * rows))
    tri = ((j[None, :] <= j[:, None]) & ((j[None, :] // CHUNK) == (j[:, None] // CHUNK)))
    to = lambda a, dt: jnp.asarray(np.ascontiguousarray(a), dtype=dt)
    return (to(cos2, F32), to(sin2, F32), to(dmask, F32), to(qdec, F32), to(kdec, F32), to(tri, BF16)), block_decay


def _const_spec(shape):
    nd = len(shape)
    return pl.BlockSpec(shape, lambda *_: (0,) * nd, pipeline_mode=pl.Buffered(1))


def _mix(x, w_in, ret_norm_g, gla_gate_w2, gla_gate_b, gla_norm_g, w_ret_out, w_gla_out, w_o, ln1_g, ln1_b, alpha):
    batch, seq, d_model = x.shape
    rows = MIX_ROWS
    assert seq % rows == 0 and rows % CHUNK == 0
    ret_qk, ret_v = RET_HEADS * RET_DK, RET_HEADS * RET_DV
    gla_qk, gla_v = GLA_HEADS * GLA_DK, GLA_HEADS * GLA_DV
    splits = (ret_qk, ret_qk, ret_v, ret_v, gla_qk, gla_qk, gla_v, gla_v, GLA_GATE_RANK, 2 * d_model)
    assert w_in.shape == (d_model, sum(splits))
    offs = np.cumsum((0,) + splits)
    parts = [w_in[:, offs[i]:offs[i + 1]].astype(BF16) for i in range(len(splits))]
    parts[8] = jnp.pad(parts[8], ((0, 0), (0, V7X_LANES - GLA_GATE_RANK)))
    w2 = jnp.pad(gla_gate_w2.astype(BF16), ((0, V7X_LANES - GLA_GATE_RANK), (0, 0)))
    (cos2, sin2, dmask, qdec, kdec, tri), block_decay = _mix_tables(seq, rows)
    row2 = lambda a: a.reshape(1, -1).astype(F32)
    consts = [dmask, qdec, kdec, tri, w2, row2(gla_gate_b), row2(ret_norm_g), row2(gla_norm_g),
              w_ret_out.astype(BF16), w_gla_out.astype(BF16), w_o.astype(BF16), row2(ln1_g), row2(ln1_b)]
    sub = d_model // 2 // V7X_LANES
    pos_spec = pl.BlockSpec((rows, RET_DK), lambda b, s: (s, 0))
    n_seq = MIX_SEQS
    assert batch % n_seq == 0
    in_specs = ([pl.BlockSpec((n_seq, rows, d_model), lambda b, s: (b, s, 0))]
                + [_const_spec(p.shape) for p in parts]
                + [pos_spec, pos_spec]
                + [_const_spec(c.shape) for c in consts])
    return pl.pallas_call(
        functools.partial(_mix_kernel, block_decay=block_decay, alpha=alpha),
        grid=(batch // n_seq, seq // rows),
        in_specs=in_specs,
        out_specs=[pl.BlockSpec((n_seq, rows, d_model), lambda b, s: (b, s, 0)),
                   pl.BlockSpec((n_seq, rows * sub, V7X_LANES), lambda b, s: (b, s, 0))],
        out_shape=[jax.ShapeDtypeStruct((batch, seq, d_model), F32),
                   jax.ShapeDtypeStruct((batch, seq * sub, V7X_LANES), jnp.uint32)],
        scratch_shapes=[pltpu.VMEM((n_seq * RET_HEADS, RET_DK, RET_DV), F32),
                        pltpu.VMEM((n_seq * GLA_HEADS, GLA_DV, GLA_DK), F32),
                        pltpu.VMEM((n_seq * rows, ret_v), BF16),
                        pltpu.VMEM((n_seq * rows, gla_v), BF16)],
        compiler_params=pltpu.CompilerParams(dimension_semantics=("arbitrary", "arbitrary"),
                                             vmem_limit_bytes=V7X_VMEM_LIMIT),
        name="mix",
    )(x, *parts, cos2, sin2, *consts)


def _route_kernel(x_ref, rwt_ref, bias_ref, triu_ref, ones_ref,
                  idx_ref, w_ref, rank_ref, counts_ref, carry_ref):
    cols = x_ref.shape[0]
    n_exp = rwt_ref.shape[0]
    per_group = n_exp // N_GROUPS
    neg_inf = -jnp.inf

    @pl.when(pl.program_id(0) == 0)
    def _():
        carry_ref[...] = jnp.zeros_like(carry_ref)

    logits = _dot_nt(rwt_ref[...], x_ref[...].astype(BF16))
    scores = _sigmoid(logits)
    biased = scores + bias_ref[...]

    sub = lax.broadcasted_iota(jnp.int32, (per_group, cols), 0)
    gscore = []
    for g in range(N_GROUPS):
        blk = biased[g * per_group:(g + 1) * per_group, :]
        m1 = jnp.max(blk, axis=0, keepdims=True)
        i1 = jnp.min(jnp.where(blk == m1, sub, per_group), axis=0, keepdims=True)
        m2 = jnp.max(jnp.where(sub == i1, neg_inf, blk), axis=0, keepdims=True)
        gscore.append(m1 + m2)
    masked = []
    for g in range(N_GROUPS):
        ahead = jnp.zeros((1, cols), jnp.int32)
        for o in range(N_GROUPS):
            if o == g:
                continue
            before = (gscore[o] >= gscore[g]) if o < g else (gscore[o] > gscore[g])
            ahead = ahead + before.astype(jnp.int32)
        keep = ahead < TOPK_GROUPS
        blk = biased[g * per_group:(g + 1) * per_group, :]
        masked.append(jnp.where(keep, blk, neg_inf))
    candidates = jnp.concatenate(masked, axis=0)
    cur = candidates

    rowid = lax.broadcasted_iota(jnp.int32, (n_exp, cols), 0)
    picked = []
    weights = []
    for _ in range(TOP_K):
        m = jnp.max(cur, axis=0, keepdims=True)
        ik = jnp.min(jnp.where(cur == m, rowid, n_exp), axis=0, keepdims=True)
        sel = rowid == ik
        weights.append(jnp.sum(jnp.where(sel, scores, 0.0), axis=0, keepdims=True))
        cur = jnp.where(sel, neg_inf, cur)
        picked.append(ik)
    wsum = weights[0]
    for wk in weights[1:]:
        wsum = wsum + wk

    chosen = (cur == neg_inf) & (candidates != neg_inf)
    chosen_b = jnp.where(chosen, 1.0, 0.0).astype(BF16)
    carry = carry_ref[...]
    before = _dot(chosen_b, triu_ref[...]) + jnp.concatenate([carry] * (cols // V7X_LANES), axis=1)
    for k in range(TOP_K):
        sel = rowid == picked[k]
        rank_ref[k:k + 1, :] = jnp.sum(jnp.where(sel, before, 0.0), axis=0, keepdims=True).astype(jnp.int32)
        idx_ref[k:k + 1, :] = picked[k]
        w_ref[k:k + 1, :] = weights[k] / wsum * ROUTED_SCALE
    carry = carry + _dot(chosen_b, ones_ref[...])
    carry_ref[...] = carry
    counts_ref[...] = carry


def _route(x1, router_w, router_bias):
    tokens, d_model = x1.shape
    n_exp = router_w.shape[1]
    cols = ROUTE_COLS
    assert tokens % cols == 0 and n_exp % N_GROUPS == 0
    j = np.arange(cols)
    triu = jnp.asarray((j[:, None] < j[None, :]), dtype=BF16)
    ones = jnp.ones((cols, V7X_LANES), BF16)
    out_row = lambda dt: jax.ShapeDtypeStruct((TOP_K, tokens), dt)
    row_spec = pl.BlockSpec((TOP_K, cols), lambda i: (0, i))
    return pl.pallas_call(
        _route_kernel,
        grid=(tokens // cols,),
        in_specs=[pl.BlockSpec((cols, d_model), lambda i: (i, 0)),
                  _const_spec((n_exp, d_model)),
                  _const_spec((n_exp, 1)),
                  _const_spec((cols, cols)),
                  _const_spec((cols, V7X_LANES))],
        out_specs=[row_spec, row_spec, row_spec, pl.BlockSpec((n_exp, V7X_LANES), lambda i: (0, 0))],
        out_shape=[out_row(jnp.int32), out_row(F32), out_row(jnp.int32),
                   jax.ShapeDtypeStruct((n_exp, V7X_LANES), F32)],
        scratch_shapes=[pltpu.VMEM((n_exp, V7X_LANES), F32)],
        compiler_params=pltpu.CompilerParams(dimension_semantics=("arbitrary",), vmem_limit_bytes=V7X_VMEM_LIMIT),
        name="route",
    )(x1, router_w.T.astype(BF16), router_bias.reshape(n_exp, 1).astype(F32), triu, ones)


def _plan_kernel(idx_ref, rank_ref, pstart_ref, dest_ref):
    n_exp = pstart_ref.shape[0]
    cols = idx_ref.shape[1]
    rowid = lax.broadcasted_iota(jnp.int32, (n_exp, cols), 0)
    pstart = pstart_ref[...]
    for k in range(TOP_K):
        base = jnp.sum(jnp.where(rowid == idx_ref[k:k + 1, :], pstart, 0.0), axis=0, keepdims=True)
        dest_ref[k:k + 1, :] = base.astype(jnp.int32) + rank_ref[k:k + 1, :]


def _plan(idx, rank, pad_starts):
    tokens = idx.shape[1]
    n_exp = pad_starts.shape[0]
    cols = ROUTE_COLS
    row_spec = pl.BlockSpec((TOP_K, cols), lambda i: (0, i))
    return pl.pallas_call(
        _plan_kernel,
        grid=(tokens // cols,),
        in_specs=[row_spec, row_spec, _const_spec((n_exp, 1))],
        out_specs=row_spec,
        out_shape=jax.ShapeDtypeStruct((TOP_K, tokens), jnp.int32),
        compiler_params=pltpu.CompilerParams(dimension_semantics=("arbitrary",)),
        name="plan",
    )(idx, rank, pad_starts.reshape(n_exp, 1).astype(F32))


def _token_rows(tok, sub):
    return pl.ds(pl.multiple_of(tok * sub, sub), sub)


def _dispatch_kernel(pad_end_ref, padded_ref, dest_ref, x_ref, xs_hbm, zero_ref, sem, *, sub):
    rows = x_ref.shape[0] // sub
    n_exp = pad_end_ref.shape[0]

    def zero_copy(e):
        start = pl.multiple_of((pad_end_ref[e] - ROW_BLOCK) * sub, ROW_BLOCK * sub)
        return pltpu.make_async_copy(zero_ref, xs_hbm.at[pl.ds(start, ROW_BLOCK * sub), :], sem)

    @pl.when(pl.program_id(0) == 0)
    def _():
        zero_ref[...] = jnp.zeros_like(zero_ref)

        def issue(e, c):
            @pl.when(padded_ref[e] > 0)
            def _():
                zero_copy(e).start()
            return c
        lax.fori_loop(0, n_exp, issue, 0)

        def drain(e, c):
            @pl.when(padded_ref[e] > 0)
            def _():
                zero_copy(e).wait()
            return c
        lax.fori_loop(0, n_exp, drain, 0)

    def row_copy(k, r):
        return pltpu.make_async_copy(x_ref.at[_token_rows(r, sub), :],
                                     xs_hbm.at[_token_rows(dest_ref[r * TOP_K + k], sub), :], sem)

    def issue(g, c):
        for u in range(DMA_UNROLL):
            for k in range(TOP_K):
                row_copy(k, g * DMA_UNROLL + u).start(priority=k % 2)
        return c
    lax.fori_loop(0, rows // DMA_UNROLL, issue, 0)

    def drain(g, c):
        for u in range(DMA_UNROLL):
            for k in range(TOP_K):
                row_copy(k, g * DMA_UNROLL + u).wait()
        return c
    lax.fori_loop(0, rows // DMA_UNROLL, drain, 0)


def _dispatch(x1p, dest, pad_ends, padded, n_rows, sub):
    tokens = x1p.shape[0] // sub
    rows = DISPATCH_ROWS
    grid_spec = pltpu.PrefetchScalarGridSpec(
        num_scalar_prefetch=2,
        grid=(tokens // rows,),
        in_specs=[pl.BlockSpec((TOP_K * rows,), lambda i, pe, pd: (i,), memory_space=pltpu.SMEM),
                  pl.BlockSpec((rows * sub, V7X_LANES), lambda i, pe, pd: (i, 0))],
        out_specs=pl.BlockSpec(memory_space=pl.ANY),
        scratch_shapes=[pltpu.VMEM((ROW_BLOCK * sub, V7X_LANES), x1p.dtype), pltpu.SemaphoreType.DMA(())],
    )
    return pl.pallas_call(
        functools.partial(_dispatch_kernel, sub=sub),
        grid_spec=grid_spec,
        out_shape=jax.ShapeDtypeStruct((n_rows * sub, V7X_LANES), x1p.dtype),
        compiler_params=pltpu.CompilerParams(dimension_semantics=("arbitrary",), has_side_effects=True),
        name="dispatch",
    )(pad_ends, padded, dest, x1p)


def _experts_kernel(first_ref, nblk_ref, total_ref, xs_hbm, wg_ref, wu_ref, wd_ref, y_hbm,
                    xbuf_ref, ybuf_ref, wgu_b, wd_b, in_sem, out_sem, *, sub):
    step = pl.program_id(0)
    per_step = wg_ref.shape[0]
    total = total_ref[0]
    ahead = EXPERT_AHEAD
    block = ROW_BLOCK * sub

    def block_rows(g):
        return pl.ds(pl.multiple_of(g * block, block), block)

    def in_copy(g):
        slot = lax.rem(g, EXPERT_SLOTS)
        return pltpu.make_async_copy(xs_hbm.at[block_rows(g), :], xbuf_ref.at[slot], in_sem.at[slot])

    def out_copy(g):
        slot = lax.rem(g, EXPERT_SLOTS)
        return pltpu.make_async_copy(ybuf_ref.at[slot], y_hbm.at[block_rows(g), :], out_sem.at[slot])

    @pl.when(step == 0)
    def _():
        for g in range(ahead):
            @pl.when(g < total)
            def _():
                in_copy(g).start()

    def one_expert(j, carry):
        e = step * per_step + j
        n = nblk_ref[e]

        @pl.when(n > 0)
        def _():
            expert_blocks(j, first_ref[e], n)
        return carry

    def expert_blocks(j, first, n):
        d_exp = wg_ref.shape[2]
        wgu_b[:, :d_exp] = wg_ref[j].astype(BF16)
        wgu_b[:, d_exp:] = wu_ref[j].astype(BF16)
        wd_b[...] = wd_ref[j].astype(BF16)

        def stage(g):
            slot = lax.rem(g, EXPERT_SLOTS)
            in_copy(g).wait()

            @pl.when(g + ahead < total)
            def _():
                in_copy(g + ahead).start()

            @pl.when(g >= EXPERT_SLOTS)
            def _():
                out_copy(g - EXPERT_SLOTS).wait()

            halves = [_unpack_rows(p) for p in _load_token_rows(xbuf_ref, ROW_BLOCK, (slot,))]
            return jnp.concatenate([h for h, _ in halves] + [l for _, l in halves], axis=1).astype(BF16)

        def swiglu(xb):
            gate_up = _dot(xb, wgu_b[...])
            hidden = _silu(gate_up[:, :d_exp]) * gate_up[:, d_exp:]
            return _pack_rows(_dot(hidden.astype(BF16), wd_b[...]))

        def finish(g, packed):
            _store_token_rows(ybuf_ref, packed, (lax.rem(g, EXPERT_SLOTS),))
            out_copy(g).start()

        def group(g, count):
            xbs = [stage(g + i) for i in range(count)]
            y = swiglu(xbs[0] if count == 1 else jnp.concatenate(xbs, axis=0))
            for i in range(count):
                finish(g + i, y[i * ROW_BLOCK:(i + 1) * ROW_BLOCK])

        def full_groups(q, c):
            group(first + EXPERT_GROUP * q, EXPERT_GROUP)
            return c
        lax.fori_loop(0, n // EXPERT_GROUP, full_groups, 0)

        done = n - lax.rem(n, EXPERT_GROUP)
        size = EXPERT_GROUP // 2
        while size >= 1:
            take = lax.rem(n, 2 * size) >= size

            @pl.when(take)
            def _(done=done, size=size):
                group(first + done, size)
            done = done + jnp.where(take, size, 0)
            size //= 2

    lax.fori_loop(0, per_step, one_expert, 0)

    @pl.when(step == pl.num_programs(0) - 1)
    def _():
        for d in range(EXPERT_SLOTS):
            @pl.when(total - 1 - d >= 0)
            def _():
                out_copy(total - 1 - d).wait()


def _experts(xs, first_block, n_block, w_gate, w_up, w_down, sub):
    n_exp, d_model, d_exp = w_gate.shape
    assert sub * V7X_LANES * 2 == d_model and xs.shape[0] % (ROW_BLOCK * sub) == 0
    assert n_exp % EXPERTS_PER_STEP == 0
    total = jnp.sum(n_block).reshape(1).astype(jnp.int32)
    grid_spec = pltpu.PrefetchScalarGridSpec(
        num_scalar_prefetch=3,
        grid=(n_exp // EXPERTS_PER_STEP,),
        in_specs=[pl.BlockSpec(memory_space=pl.ANY),
                  pl.BlockSpec((EXPERTS_PER_STEP, d_model, d_exp), lambda e, fb, nb, tt: (e, 0, 0)),
                  pl.BlockSpec((EXPERTS_PER_STEP, d_model, d_exp), lambda e, fb, nb, tt: (e, 0, 0)),
                  pl.BlockSpec((EXPERTS_PER_STEP, d_exp, d_model), lambda e, fb, nb, tt: (e, 0, 0))],
        out_specs=pl.BlockSpec(memory_space=pl.ANY),
        scratch_shapes=[pltpu.VMEM((EXPERT_SLOTS, ROW_BLOCK * sub, V7X_LANES), xs.dtype),
                        pltpu.VMEM((EXPERT_SLOTS, ROW_BLOCK * sub, V7X_LANES), xs.dtype),
                        pltpu.VMEM((d_model, 2 * d_exp), BF16),
                        pltpu.VMEM((d_exp, d_model), BF16),
                        pltpu.SemaphoreType.DMA((EXPERT_SLOTS,)),
                        pltpu.SemaphoreType.DMA((EXPERT_SLOTS,))],
    )
    return pl.pallas_call(
        functools.partial(_experts_kernel, sub=sub),
        grid_spec=grid_spec,
        out_shape=jax.ShapeDtypeStruct(xs.shape, xs.dtype),
        compiler_params=pltpu.CompilerParams(dimension_semantics=("arbitrary",), vmem_limit_bytes=V7X_VMEM_LIMIT,
                                             has_side_effects=True),
        name="experts",
    )(first_block, n_block, total, xs, w_gate, w_up, w_down)


def _combine_kernel(dest_ref, next_ref, wt_ref, x_ref, y_hbm, sg_ref, su_ref, sd_ref, g_ref, b_ref,
                    out_ref, ybuf_a, ybuf_b, sem_a, sem_b, *, alpha, sub):
    rows = x_ref.shape[0] // 2
    step = pl.program_id(0)

    def row_copy(idx_ref, tok, k, r, buf, sem):
        return pltpu.make_async_copy(y_hbm.at[_token_rows(idx_ref[tok * TOP_K + k], sub), :],
                                     buf.at[k, _token_rows(r, sub), :], sem)

    def issue_rolled(idx_ref, first_tok, buf, sem):
        def body(g, c):
            for u in range(DMA_UNROLL):
                for k in range(TOP_K):
                    r = g * DMA_UNROLL + u
                    row_copy(idx_ref, first_tok + r, k, r, buf, sem).start(priority=k % 2)
            return c
        lax.fori_loop(0, rows // DMA_UNROLL, body, 0)

    def issue_inline(idx_ref, first_tok, buf, sem):
        for r in range(rows):
            for k in range(TOP_K):
                row_copy(idx_ref, first_tok + r, k, r, buf, sem).start(priority=k % 2)

    def drain(buf, sem):
        def body(g, c):
            for u in range(DMA_UNROLL):
                for k in range(TOP_K):
                    r = g * DMA_UNROLL + u
                    pltpu.make_async_copy(y_hbm.at[_token_rows(0, sub), :],
                                          buf.at[k, _token_rows(r, sub), :], sem).wait()
            return c
        lax.fori_loop(0, rows // DMA_UNROLL, body, 0)

    def reduce_block(first_tok, buf):
        rs = slice(first_tok, first_tok + rows)
        x = x_ref[rs, :]
        xb = x.astype(BF16)
        hidden = _silu(_dot(xb, sg_ref[...])) * _dot(xb, su_ref[...])
        acc = alpha * x + _dot(hidden.astype(BF16), sd_ref[...])
        wt = wt_ref[rs, :]
        chunks = [acc[:, c * V7X_LANES:(c + 1) * V7X_LANES] for c in range(2 * sub)]
        for k in range(TOP_K):
            wk = wt[:, k:k + 1]
            for c, p in enumerate(_load_token_rows(buf, rows, (k,))):
                hi, lo = _unpack_rows(p)
                chunks[c] = chunks[c] + hi * wk
                chunks[sub + c] = chunks[sub + c] + lo * wk
        out_ref[rs, :] = _layernorm_rows(jnp.concatenate(chunks, axis=1), g_ref[...], b_ref[...])

    @pl.when(step == 0)
    def _():
        issue_rolled(dest_ref, 0, ybuf_a, sem_a)

    drain(ybuf_a, sem_a)
    issue_inline(dest_ref, rows, ybuf_b, sem_b)
    reduce_block(0, ybuf_a)
    drain(ybuf_b, sem_b)
    issue_inline(next_ref, 0, ybuf_a, sem_a)
    reduce_block(rows, ybuf_b)

    @pl.when(step == pl.num_programs(0) - 1)
    def _():
        drain(ybuf_a, sem_a)


def _combine(x1, y, dest, w_tok, sw_gate, sw_up, sw_down, ln2_g, ln2_b, alpha, sub):
    tokens, d_model = x1.shape
    d_shared = sw_gate.shape[1]
    rows = 2 * COMBINE_ROWS
    assert tokens % rows == 0
    steps = tokens // rows
    row2 = lambda a: a.reshape(1, -1).astype(F32)
    return pl.pallas_call(
        functools.partial(_combine_kernel, alpha=alpha, sub=sub),
        grid=(steps,),
        in_specs=[pl.BlockSpec((TOP_K * rows,), lambda i: (i,), memory_space=pltpu.SMEM),
                  pl.BlockSpec((TOP_K * rows,), lambda i: (jnp.minimum(i + 1, steps - 1),), memory_space=pltpu.SMEM),
                  pl.BlockSpec((rows, TOP_K), lambda i: (i, 0)),
                  pl.BlockSpec((rows, d_model), lambda i: (i, 0)),
                  pl.BlockSpec(memory_space=pl.ANY),
                  _const_spec((d_model, d_shared)),
                  _const_spec((d_model, d_shared)),
                  _const_spec((d_shared, d_model)),
                  _const_spec((1, d_model)),
                  _const_spec((1, d_model))],
        out_specs=pl.BlockSpec((rows, d_model), lambda i: (i, 0)),
        out_shape=jax.ShapeDtypeStruct((tokens, d_model), F32),
        scratch_shapes=[pltpu.VMEM((TOP_K, COMBINE_ROWS * sub, V7X_LANES), y.dtype),
                        pltpu.VMEM((TOP_K, COMBINE_ROWS * sub, V7X_LANES), y.dtype),
                        pltpu.SemaphoreType.DMA(()), pltpu.SemaphoreType.DMA(())],
        compiler_params=pltpu.CompilerParams(dimension_semantics=("arbitrary",), vmem_limit_bytes=V7X_VMEM_LIMIT),
        name="combine",
    )(dest, dest, w_tok, x1, y, sw_gate.astype(BF16), sw_up.astype(BF16), sw_down.astype(BF16),
      row2(ln2_g), row2(ln2_b))


def _block_layout(counts, tokens):
    n_exp = counts.shape[0]
    padded = (counts + ROW_BLOCK - 1) // ROW_BLOCK * ROW_BLOCK
    pad_ends = jnp.cumsum(padded)
    pad_starts = pad_ends - padded
    n_blocks = -(-(tokens * TOP_K + n_exp * (ROW_BLOCK - 1)) // ROW_BLOCK)
    return padded, pad_ends, pad_starts, pad_starts // ROW_BLOCK, padded // ROW_BLOCK, n_blocks * ROW_BLOCK


def kernel(x, w_in, ret_norm_g, gla_gate_w2, gla_gate_b, gla_norm_g, w_ret_out, w_gla_out, w_o, ln1_g, ln1_b, router_w, router_bias, exp_w_gate, exp_w_up, exp_w_down, shared_w_gate, shared_w_up, shared_w_down, ln2_g, ln2_b):
    batch, seq, d_model = x.shape
    depth = w_in.shape[0]
    alpha = (2.0 * depth) ** 0.25
    for l in range(depth):
        x1, x1p = _mix(x, w_in[l], ret_norm_g[l], gla_gate_w2[l], gla_gate_b[l], gla_norm_g[l],
                       w_ret_out[l], w_gla_out[l], w_o[l], ln1_g[l], ln1_b[l], alpha)
        x1 = x1.reshape(batch * seq, d_model)
        sub = x1p.shape[1] // seq
        x1p = x1p.reshape(batch * seq * sub, V7X_LANES)
        idx, w_sel, rank, counts = _route(x1, router_w[l], router_bias[l])
        padded, pad_ends, pad_starts, first_block, n_block, n_rows = _block_layout(
            counts[:, 0].astype(jnp.int32), batch * seq)
        dest = _plan(idx, rank, pad_starts).T.reshape(-1)
        xs = _dispatch(x1p, dest, pad_ends, padded, n_rows, sub)
        y = _experts(xs, first_block, n_block, exp_w_gate[l], exp_w_up[l], exp_w_down[l], sub)
        out = _combine(x1, y, dest, w_sel.T, shared_w_gate[l], shared_w_up[l], shared_w_down[l],
                       ln2_g[l], ln2_b[l], alpha, sub)
        x = out.reshape(batch, seq, d_model)
    return x
```

```python
import functools

import jax
import jax.numpy as jnp
import numpy as np
from jax import lax
from jax.experimental import pallas as pl
from jax.experimental.pallas import tpu as pltpu

CHUNK = 64
RET_HEADS = 4
RET_DK = 128
RET_DV = 256
GLA_HEADS = 4
GLA_DK = 128
GLA_DV = 256
GLA_GATE_RANK = 16
GLA_GATE_TAU = 16.0
ROPE_THETA = 10000.0
N_EXPERTS = 256
TOP_K = 8
N_GROUPS = 8
TOPK_GROUPS = 4
ROUTED_SCALE = 2.5
LN_EPS = 1e-5
NORM_EPS = 1e-6

V7X_LANES = 128
V7X_VMEM_LIMIT = 60 * 1024 * 1024

MIX_ROWS = 256
MIX_SEQS = 2
ROUTE_COLS = 1024
ROW_BLOCK = 128
DISPATCH_ROWS = 512
DMA_UNROLL = 2
EXPERT_GROUP = 8
EXPERT_AHEAD = 8
EXPERT_SLOTS = EXPERT_GROUP + EXPERT_AHEAD
COMBINE_ROWS = 256

F32 = jnp.float32
BF16 = jnp.bfloat16


def _dot(a, b):
    return jnp.dot(a, b, preferred_element_type=F32)


def _dot_nt(a, b):
    return lax.dot_general(a, b, (((1,), (1,)), ((), ())), preferred_element_type=F32)


def _dot_tn(a, b):
    return lax.dot_general(a, b, (((0,), (0,)), ((), ())), preferred_element_type=F32)


def _sigmoid(v):
    return 1.0 / (1.0 + jnp.exp(-v))


def _silu(v):
    return v * _sigmoid(v)


def _pack_rows(v):
    half = v.shape[1] // 2
    hi = lax.bitcast_convert_type(v[:, :half].astype(BF16).astype(F32), jnp.uint32)
    lo = lax.bitcast_convert_type(v[:, half:].astype(BF16).astype(F32), jnp.uint32)
    return hi | (lo >> 16)


def _unpack_rows(p):
    hi = lax.bitcast_convert_type(p & jnp.uint32(0xFFFF0000), F32)
    lo = lax.bitcast_convert_type(p << 16, F32)
    return hi, lo


def _store_token_rows(ref, packed, lead=()):
    m, width = packed.shape
    sub = width // V7X_LANES
    for c in range(sub):
        ref[lead + (pl.ds(c, m, stride=sub), slice(None))] = packed[:, c * V7X_LANES:(c + 1) * V7X_LANES]


def _load_token_rows(ref, m, lead=()):
    sub = ref.shape[-2] // m
    return [ref[lead + (pl.ds(c, m, stride=sub), slice(None))] for c in range(sub)]


def _layernorm_rows(v, g, b):
    mu = jnp.mean(v, axis=-1, keepdims=True)
    vc = v - mu
    var = jnp.mean(vc * vc, axis=-1, keepdims=True)
    return vc * lax.rsqrt(var + LN_EPS) * g + b


def _mix_kernel(x_ref, wrq_ref, wrk_ref, wrv_ref, wrg_ref, wgq_ref, wgk_ref, wgv_ref, wgg_ref, wga_ref, wmg_ref,
                cos_ref, sin_ref, dmask_ref, qdec_ref, kdec_ref, tri_ref, w2_ref, gb_ref, retg_ref, glag_ref,
                wro_ref, wgo_ref, wo_ref, ln1g_ref, ln1b_ref,
                out_ref, packed_ref, rstate_ref, gstate_ref, yret_ref, ygla_ref, *, block_decay, alpha):
    n_seq, rows, d_model = x_ref.shape

    @pl.when(pl.program_id(1) == 0)
    def _():
        rstate_ref[...] = jnp.zeros_like(rstate_ref)
        gstate_ref[...] = jnp.zeros_like(gstate_ref)

    x = x_ref[...].reshape(n_seq * rows, d_model)
    xb = x.astype(BF16)

    def proj(w_ref):
        return _dot(xb, w_ref[...])

    cos = cos_ref[...]
    sin = sin_ref[...]

    def rope(t):
        return t * cos + pltpu.roll(t, RET_DK // 2, 1) * sin

    rq = proj(wrq_ref)
    rk = proj(wrk_ref)
    rv = proj(wrv_ref)
    rg = proj(wrg_ref)
    seq_heads = [(s, h) for s in range(n_seq) for h in range(RET_HEADS)]
    ret_states = [rstate_ref[s * RET_HEADS + h] for s, h in seq_heads]
    for i, (s, h) in enumerate(seq_heads):
        sr = slice(s * rows, (s + 1) * rows)
        qk = slice(h * RET_DK, (h + 1) * RET_DK)
        vv = slice(h * RET_DV, (h + 1) * RET_DV)
        q = rope(rq[sr, qk])
        k = rope(rk[sr, qk])
        v = rv[sr, vv].astype(BF16)
        scores = _dot_nt(q.astype(BF16), k.astype(BF16)) * dmask_ref[h]
        o = _dot(scores.astype(BF16), v)
        state = ret_states[i]
        o = o + _dot((q * qdec_ref[h]).astype(BF16), state.astype(BF16))
        ret_states[i] = state * block_decay[h] + _dot_tn((k * kdec_ref[h]).astype(BF16), v)
        mu = jnp.mean(o, axis=-1, keepdims=True)
        oc = o - mu
        var = jnp.mean(oc * oc, axis=-1, keepdims=True)
        y = oc * lax.rsqrt(var + LN_EPS) * retg_ref[:, vv] * _silu(rg[sr, vv])
        yret_ref[sr, vv] = y.astype(BF16)
    for i, (s, h) in enumerate(seq_heads):
        rstate_ref[s * RET_HEADS + h] = ret_states[i]

    gq = proj(wgq_ref) * (GLA_DK ** -0.5)
    gk = proj(wgk_ref)
    gv = proj(wgv_ref)
    gg = proj(wgg_ref)
    ga = proj(wga_ref)
    z = _dot(ga.astype(BF16), w2_ref[...]) + gb_ref[...]
    log_a = (jnp.minimum(z, 0.0) - jnp.log1p(jnp.exp(-jnp.abs(z)))) * (1.0 / GLA_GATE_TAU)
    la_hi = log_a.astype(BF16)
    la_lo = (log_a - la_hi.astype(F32)).astype(BF16)
    tri = tri_ref[...]
    bcum = jnp.concatenate([_dot(tri, la_hi[s * rows:(s + 1) * rows]) + _dot(tri, la_lo[s * rows:(s + 1) * rows])
                            for s in range(n_seq)], axis=0)
    n_chunks = rows // CHUNK
    seq_heads = [(s, h) for s in range(n_seq) for h in range(GLA_HEADS)]
    gla_states = [gstate_ref[s * GLA_HEADS + h] for s, h in seq_heads]
    decays = []
    updates = []
    for c in range(n_seq * n_chunks):
        rs = slice(c * CHUNK, (c + 1) * CHUNK)
        b_end = bcum[(c + 1) * CHUNK - 1:(c + 1) * CHUNK, :]
        kt = (gk[rs, :] * jnp.exp(b_end - bcum[rs, :])).astype(BF16)
        decays.append(jnp.exp(b_end))
        updates.append([_dot_tn(gv[rs, h * GLA_DV:(h + 1) * GLA_DV].astype(BF16), kt[:, h * GLA_DK:(h + 1) * GLA_DK])
                        for h in range(GLA_HEADS)])
    for c in range(n_seq * n_chunks):
        rs = slice(c * CHUNK, (c + 1) * CHUNK)
        for h in range(GLA_HEADS):
            i = (c // n_chunks) * GLA_HEADS + h
            qk = slice(h * GLA_DK, (h + 1) * GLA_DK)
            vv = slice(h * GLA_DV, (h + 1) * GLA_DV)
            gla_states[i] = gla_states[i] * decays[c][:, qk] + updates[c][h]
            o = _dot_nt(gq[rs, qk].astype(BF16), gla_states[i].astype(BF16))
            ms = jnp.mean(o * o, axis=-1, keepdims=True)
            y = o * lax.rsqrt(ms + NORM_EPS) * glag_ref[:, vv] * _silu(gg[rs, vv])
            ygla_ref[rs, vv] = y.astype(BF16)
    for i, (s, h) in enumerate(seq_heads):
        gstate_ref[s * GLA_HEADS + h] = gla_states[i]

    u_ret = _dot(yret_ref[...], wro_ref[...])
    u_gla = _dot(ygla_ref[...], wgo_ref[...])
    gate = _sigmoid(proj(wmg_ref))
    merged = gate[:, :d_model] * u_ret + gate[:, d_model:] * u_gla
    mix = _dot(merged.astype(BF16), wo_ref[...])
    out = _layernorm_rows(alpha * x + mix, ln1g_ref[...], ln1b_ref[...])
    out_ref[...] = out.reshape(n_seq, rows, d_model)
    packed = _pack_rows(out)
    for s in range(n_seq):
        _store_token_rows(packed_ref, packed[s * rows:(s + 1) * rows], (s,))


def _mix_tables(seq, rows):
    half = RET_DK // 2
    inv = ROPE_THETA ** (-np.arange(half, dtype=np.float64) / half)
    ang = np.arange(seq, dtype=np.float64)[:, None] * inv[None, :]
    cos2 = np.concatenate([np.cos(ang), np.cos(ang)], axis=1)
    sin2 = np.concatenate([-np.sin(ang), np.sin(ang)], axis=1)
    log_g = np.log1p(-np.exp2(-5.0 - np.arange(RET_HEADS, dtype=np.float64)))
    j = np.arange(rows, dtype=np.float64)
    same_or_earlier_chunk = (j[None, :] // CHUNK) <= (j[:, None] // CHUNK)
    k_scale = RET_DK ** -0.5
    dmask = np.exp(log_g[:, None, None] * np.abs(j[:, None] - j[None, :])) * same_or_earlier_chunk[None] * k_scale
    qdec = np.exp(log_g[:, None] * (j[None, :] + 1.0))
    kdec = np.exp(log_g[:, None] * (rows - 1.0 - j[None, :])) * k_scale
    qdec = np.broadcast_to(qdec[:, :, None], (RET_HEADS, rows, RET_DK))
    kdec = np.broadcast_to(kdec[:, :, None], (RET_HEADS, rows, RET_DK))
    block_decay = tuple(float(v) for v in np.exp(log_g * rows))
    tri = ((j[None, :] <= j[:, None]) & ((j[None, :] // CHUNK) == (j[:, None] // CHUNK)))
    to = lambda a, dt: jnp.asarray(np.ascontiguousarray(a), dtype=dt)
    return (to(cos2, F32), to(sin2, F32), to(dmask, F32), to(qdec, F32), to(kdec, F32), to(tri, BF16)), block_decay


def _const_spec(shape):
    nd = len(shape)
    return pl.BlockSpec(shape, lambda *_: (0,) * nd, pipeline_mode=pl.Buffered(1))


def _mix(x, w_in, ret_norm_g, gla_gate_w2, gla_gate_b, gla_norm_g, w_ret_out, w_gla_out, w_o, ln1_g, ln1_b, alpha):
    batch, seq, d_model = x.shape
    rows = MIX_ROWS
    assert seq % rows == 0 and rows % CHUNK == 0
    ret_qk, ret_v = RET_HEADS * RET_DK, RET_HEADS * RET_DV
    gla_qk, gla_v = GLA_HEADS * GLA_DK, GLA_HEADS * GLA_DV
    splits = (ret_qk, ret_qk, ret_v, ret_v, gla_qk, gla_qk, gla_v, gla_v, GLA_GATE_RANK, 2 * d_model)
    assert w_in.shape == (d_model, sum(splits))
    offs = np.cumsum((0,) + splits)
    parts = [w_in[:, offs[i]:offs[i + 1]].astype(BF16) for i in range(len(splits))]
    parts[8] = jnp.pad(parts[8], ((0, 0), (0, V7X_LANES - GLA_GATE_RANK)))
    w2 = jnp.pad(gla_gate_w2.astype(BF16), ((0, V7X_LANES - GLA_GATE_RANK), (0, 0)))
    (cos2, sin2, dmask, qdec, kdec, tri), block_decay = _mix_tables(seq, rows)
    row2 = lambda a: a.reshape(1, -1).astype(F32)
    consts = [dmask, qdec, kdec, tri, w2, row2(gla_gate_b), row2(ret_norm_g), row2(gla_norm_g),
              w_ret_out.astype(BF16), w_gla_out.astype(BF16), w_o.astype(BF16), row2(ln1_g), row2(ln1_b)]
    sub = d_model // 2 // V7X_LANES
    pos_spec = pl.BlockSpec((rows, RET_DK), lambda b, s: (s, 0))
    n_seq = MIX_SEQS
    assert batch % n_seq == 0
    in_specs = ([pl.BlockSpec((n_seq, rows, d_model), lambda b, s: (b, s, 0))]
                + [_const_spec(p.shape) for p in parts]
                + [pos_spec, pos_spec]
                + [_const_spec(c.shape) for c in consts])
    return pl.pallas_call(
        functools.partial(_mix_kernel, block_decay=block_decay, alpha=alpha),
        grid=(batch // n_seq, seq // rows),
        in_specs=in_specs,
        out_specs=[pl.BlockSpec((n_seq, rows, d_model), lambda b, s: (b, s, 0)),
                   pl.BlockSpec((n_seq, rows * sub, V7X_LANES), lambda b, s: (b, s, 0))],
        out_shape=[jax.ShapeDtypeStruct((batch, seq, d_model), F32),
                   jax.ShapeDtypeStruct((batch, seq * sub, V7X_LANES), jnp.uint32)],
        scratch_shapes=[pltpu.VMEM((n_seq * RET_HEADS, RET_DK, RET_DV), F32),
                        pltpu.VMEM((n_seq * GLA_HEADS, GLA_DV, GLA_DK), F32),
                        pltpu.VMEM((n_seq * rows, ret_v), BF16),
                        pltpu.VMEM((n_seq * rows, gla_v), BF16)],
        compiler_params=pltpu.CompilerParams(dimension_semantics=("arbitrary", "arbitrary"),
                                             vmem_limit_bytes=V7X_VMEM_LIMIT),
        name="mix",
    )(x, *parts, cos2, sin2, *consts)


def _route_kernel(x_ref, rwt_ref, bias_ref, triu_ref, ones_ref,
                  idx_ref, w_ref, rank_ref, counts_ref, carry_ref):
    cols = x_ref.shape[0]
    n_exp = rwt_ref.shape[0]
    per_group = n_exp // N_GROUPS
    neg_inf = -jnp.inf

    @pl.when(pl.program_id(0) == 0)
    def _():
        carry_ref[...] = jnp.zeros_like(carry_ref)

    logits = _dot_nt(rwt_ref[...], x_ref[...].astype(BF16))
    scores = _sigmoid(logits)
    biased = scores + bias_ref[...]

    sub = lax.broadcasted_iota(jnp.int32, (per_group, cols), 0)
    gscore = []
    for g in range(N_GROUPS):
        blk = biased[g * per_group:(g + 1) * per_group, :]
        m1 = jnp.max(blk, axis=0, keepdims=True)
        i1 = jnp.min(jnp.where(blk == m1, sub, per_group), axis=0, keepdims=True)
        m2 = jnp.max(jnp.where(sub == i1, neg_inf, blk), axis=0, keepdims=True)
        gscore.append(m1 + m2)
    masked = []
    for g in range(N_GROUPS):
        ahead = jnp.zeros((1, cols), jnp.int32)
        for o in range(N_GROUPS):
            if o == g:
                continue
            before = (gscore[o] >= gscore[g]) if o < g else (gscore[o] > gscore[g])
            ahead = ahead + before.astype(jnp.int32)
        keep = ahead < TOPK_GROUPS
        blk = biased[g * per_group:(g + 1) * per_group, :]
        masked.append(jnp.where(keep, blk, neg_inf))
    candidates = jnp.concatenate(masked, axis=0)
    cur = candidates

    rowid = lax.broadcasted_iota(jnp.int32, (n_exp, cols), 0)
    picked = []
    weights = []
    for _ in range(TOP_K):
        m = jnp.max(cur, axis=0, keepdims=True)
        ik = jnp.min(jnp.where(cur == m, rowid, n_exp), axis=0, keepdims=True)
        sel = rowid == ik
        weights.append(jnp.sum(jnp.where(sel, scores, 0.0), axis=0, keepdims=True))
        cur = jnp.where(sel, neg_inf, cur)
        picked.append(ik)
    wsum = weights[0]
    for wk in weights[1:]:
        wsum = wsum + wk

    chosen = (cur == neg_inf) & (candidates != neg_inf)
    chosen_b = jnp.where(chosen, 1.0, 0.0).astype(BF16)
    carry = carry_ref[...]
    before = _dot(chosen_b, triu_ref[...]) + jnp.concatenate([carry] * (cols // V7X_LANES), axis=1)
    for k in range(TOP_K):
        sel = rowid == picked[k]
        rank_ref[k:k + 1, :] = jnp.sum(jnp.where(sel, before, 0.0), axis=0, keepdims=True).astype(jnp.int32)
        idx_ref[k:k + 1, :] = picked[k]
        w_ref[k:k + 1, :] = weights[k] / wsum * ROUTED_SCALE
    carry = carry + _dot(chosen_b, ones_ref[...])
    carry_ref[...] = carry
    counts_ref[...] = carry


def _route(x1, router_w, router_bias):
    tokens, d_model = x1.shape
    n_exp = router_w.shape[1]
    cols = ROUTE_COLS
    assert tokens % cols == 0 and n_exp % N_GROUPS == 0
    j = np.arange(cols)
    triu = jnp.asarray((j[:, None] < j[None, :]), dtype=BF16)
    ones = jnp.ones((cols, V7X_LANES), BF16)
    out_row = lambda dt: jax.ShapeDtypeStruct((TOP_K, tokens), dt)
    row_spec = pl.BlockSpec((TOP_K, cols), lambda i: (0, i))
    return pl.pallas_call(
        _route_kernel,
        grid=(tokens // cols,),
        in_specs=[pl.BlockSpec((cols, d_model), lambda i: (i, 0)),
                  _const_spec((n_exp, d_model)),
                  _const_spec((n_exp, 1)),
                  _const_spec((cols, cols)),
                  _const_spec((cols, V7X_LANES))],
        out_specs=[row_spec, row_spec, row_spec, pl.BlockSpec((n_exp, V7X_LANES), lambda i: (0, 0))],
        out_shape=[out_row(jnp.int32), out_row(F32), out_row(jnp.int32),
                   jax.ShapeDtypeStruct((n_exp, V7X_LANES), F32)],
        scratch_shapes=[pltpu.VMEM((n_exp, V7X_LANES), F32)],
        compiler_params=pltpu.CompilerParams(dimension_semantics=("arbitrary",), vmem_limit_bytes=V7X_VMEM_LIMIT),
        name="route",
    )(x1, router_w.T.astype(BF16), router_bias.reshape(n_exp, 1).astype(F32), triu, ones)


def _plan_kernel(idx_ref, rank_ref, pstart_ref, dest_ref):
    n_exp = pstart_ref.shape[0]
    cols = idx_ref.shape[1]
    rowid = lax.broadcasted_iota(jnp.int32, (n_exp, cols), 0)
    pstart = pstart_ref[...]
    for k in range(TOP_K):
        base = jnp.sum(jnp.where(rowid == idx_ref[k:k + 1, :], pstart, 0.0), axis=0, keepdims=True)
        dest_ref[k:k + 1, :] = base.astype(jnp.int32) + rank_ref[k:k + 1, :]


def _plan(idx, rank, pad_starts):
    tokens = idx.shape[1]
    n_exp = pad_starts.shape[0]
    cols = ROUTE_COLS
    row_spec = pl.BlockSpec((TOP_K, cols), lambda i: (0, i))
    return pl.pallas_call(
        _plan_kernel,
        grid=(tokens // cols,),
        in_specs=[row_spec, row_spec, _const_spec((n_exp, 1))],
        out_specs=row_spec,
        out_shape=jax.ShapeDtypeStruct((TOP_K, tokens), jnp.int32),
        compiler_params=pltpu.CompilerParams(dimension_semantics=("arbitrary",)),
        name="plan",
    )(idx, rank, pad_starts.reshape(n_exp, 1).astype(F32))


def _token_rows(tok, sub):
    return pl.ds(pl.multiple_of(tok * sub, sub), sub)


def _dispatch_kernel(pad_end_ref, padded_ref, dest_ref, x_ref, xs_hbm, zero_ref, sem, *, sub):
    rows = x_ref.shape[0] // sub
    n_exp = pad_end_ref.shape[0]

    def zero_copy(e):
        start = pl.multiple_of((pad_end_ref[e] - ROW_BLOCK) * sub, ROW_BLOCK * sub)
        return pltpu.make_async_copy(zero_ref, xs_hbm.at[pl.ds(start, ROW_BLOCK * sub), :], sem)

    @pl.when(pl.program_id(0) == 0)
    def _():
        zero_ref[...] = jnp.zeros_like(zero_ref)

        def issue(e, c):
            @pl.when(padded_ref[e] > 0)
            def _():
                zero_copy(e).start()
            return c
        lax.fori_loop(0, n_exp, issue, 0)

        def drain(e, c):
            @pl.when(padded_ref[e] > 0)
            def _():
                zero_copy(e).wait()
            return c
        lax.fori_loop(0, n_exp, drain, 0)

    def row_copy(k, r):
        return pltpu.make_async_copy(x_ref.at[_token_rows(r, sub), :],
                                     xs_hbm.at[_token_rows(dest_ref[r * TOP_K + k], sub), :], sem)

    def issue(g, c):
        for u in range(DMA_UNROLL):
            for k in range(TOP_K):
                row_copy(k, g * DMA_UNROLL + u).start(priority=k % 2)
        return c
    lax.fori_loop(0, rows // DMA_UNROLL, issue, 0)

    def drain(g, c):
        for u in range(DMA_UNROLL):
            for k in range(TOP_K):
                row_copy(k, g * DMA_UNROLL + u).wait()
        return c
    lax.fori_loop(0, rows // DMA_UNROLL, drain, 0)


def _dispatch(x1p, dest, pad_ends, padded, n_rows, sub):
    tokens = x1p.shape[0] // sub
    rows = DISPATCH_ROWS
    grid_spec = pltpu.PrefetchScalarGridSpec(
        num_scalar_prefetch=2,
        grid=(tokens // rows,),
        in_specs=[pl.BlockSpec((TOP_K * rows,), lambda i, pe, pd: (i,), memory_space=pltpu.SMEM),
                  pl.BlockSpec((rows * sub, V7X_LANES), lambda i, pe, pd: (i, 0))],
        out_specs=pl.BlockSpec(memory_space=pl.ANY),
        scratch_shapes=[pltpu.VMEM((ROW_BLOCK * sub, V7X_LANES), x1p.dtype), pltpu.SemaphoreType.DMA(())],
    )
    return pl.pallas_call(
        functools.partial(_dispatch_kernel, sub=sub),
        grid_spec=grid_spec,
        out_shape=jax.ShapeDtypeStruct((n_rows * sub, V7X_LANES), x1p.dtype),
        compiler_params=pltpu.CompilerParams(dimension_semantics=("arbitrary",), has_side_effects=True),
        name="dispatch",
    )(pad_ends, padded, dest, x1p)


def _experts_kernel(first_ref, nblk_ref, total_ref, xs_hbm, wg_ref, wu_ref, wd_ref, y_hbm,
                    xbuf_ref, ybuf_ref, wgu_b, wd_b, in_sem, out_sem, *, sub):
    e = pl.program_id(0)
    n = nblk_ref[e]
    first = first_ref[e]
    total = total_ref[0]
    ahead = EXPERT_AHEAD
    block = ROW_BLOCK * sub

    def block_rows(g):
        return pl.ds(pl.multiple_of(g * block, block), block)

    def in_copy(g):
        slot = lax.rem(g, EXPERT_SLOTS)
        return pltpu.make_async_copy(xs_hbm.at[block_rows(g), :], xbuf_ref.at[slot], in_sem.at[slot])

    def out_copy(g):
        slot = lax.rem(g, EXPERT_SLOTS)
        return pltpu.make_async_copy(ybuf_ref.at[slot], y_hbm.at[block_rows(g), :], out_sem.at[slot])

    @pl.when(e == 0)
    def _():
        for g in range(ahead):
            @pl.when(g < total)
            def _():
                in_copy(g).start()

    @pl.when(n > 0)
    def _():
        d_exp = wg_ref.shape[1]
        wgu_b[:, :d_exp] = wg_ref[...].astype(BF16)
        wgu_b[:, d_exp:] = wu_ref[...].astype(BF16)
        wd_b[...] = wd_ref[...].astype(BF16)

        def stage(g):
            slot = lax.rem(g, EXPERT_SLOTS)
            in_copy(g).wait()

            @pl.when(g + ahead < total)
            def _():
                in_copy(g + ahead).start()

            @pl.when(g >= EXPERT_SLOTS)
            def _():
                out_copy(g - EXPERT_SLOTS).wait()

            halves = [_unpack_rows(p) for p in _load_token_rows(xbuf_ref, ROW_BLOCK, (slot,))]
            return jnp.concatenate([h for h, _ in halves] + [l for _, l in halves], axis=1).astype(BF16)

        def swiglu(xb):
            gate_up = _dot(xb, wgu_b[...])
            hidden = _silu(gate_up[:, :d_exp]) * gate_up[:, d_exp:]
            return _pack_rows(_dot(hidden.astype(BF16), wd_b[...]))

        def finish(g, packed):
            _store_token_rows(ybuf_ref, packed, (lax.rem(g, EXPERT_SLOTS),))
            out_copy(g).start()

        def group(g, count):
            xbs = [stage(g + i) for i in range(count)]
            y = swiglu(xbs[0] if count == 1 else jnp.concatenate(xbs, axis=0))
            for i in range(count):
                finish(g + i, y[i * ROW_BLOCK:(i + 1) * ROW_BLOCK])

        def full_groups(q, c):
            group(first + EXPERT_GROUP * q, EXPERT_GROUP)
            return c
        lax.fori_loop(0, n // EXPERT_GROUP, full_groups, 0)

        done = n - lax.rem(n, EXPERT_GROUP)
        size = EXPERT_GROUP // 2
        while size >= 1:
            take = lax.rem(n, 2 * size) >= size

            @pl.when(take)
            def _(done=done, size=size):
                group(first + done, size)
            done = done + jnp.where(take, size, 0)
            size //= 2

    @pl.when(e == pl.num_programs(0) - 1)
    def _():
        for d in range(EXPERT_SLOTS):
            @pl.when(total - 1 - d >= 0)
            def _():
                out_copy(total - 1 - d).wait()


def _experts(xs, first_block, n_block, w_gate, w_up, w_down, sub):
    n_exp, d_model, d_exp = w_gate.shape
    assert sub * V7X_LANES * 2 == d_model and xs.shape[0] % (ROW_BLOCK * sub) == 0
    total = jnp.sum(n_block).reshape(1).astype(jnp.int32)
    grid_spec = pltpu.PrefetchScalarGridSpec(
        num_scalar_prefetch=3,
        grid=(n_exp,),
        in_specs=[pl.BlockSpec(memory_space=pl.ANY),
                  pl.BlockSpec((None, d_model, d_exp), lambda e, fb, nb, tt: (e, 0, 0)),
                  pl.BlockSpec((None, d_model, d_exp), lambda e, fb, nb, tt: (e, 0, 0)),
                  pl.BlockSpec((None, d_exp, d_model), lambda e, fb, nb, tt: (e, 0, 0))],
        out_specs=pl.BlockSpec(memory_space=pl.ANY),
        scratch_shapes=[pltpu.VMEM((EXPERT_SLOTS, ROW_BLOCK * sub, V7X_LANES), xs.dtype),
                        pltpu.VMEM((EXPERT_SLOTS, ROW_BLOCK * sub, V7X_LANES), xs.dtype),
                        pltpu.VMEM((d_model, 2 * d_exp), BF16),
                        pltpu.VMEM((d_exp, d_model), BF16),
                        pltpu.SemaphoreType.DMA((EXPERT_SLOTS,)),
                        pltpu.SemaphoreType.DMA((EXPERT_SLOTS,))],
    )
    return pl.pallas_call(
        functools.partial(_experts_kernel, sub=sub),
        grid_spec=grid_spec,
        out_shape=jax.ShapeDtypeStruct(xs.shape, xs.dtype),
        compiler_params=pltpu.CompilerParams(dimension_semantics=("arbitrary",), vmem_limit_bytes=V7X_VMEM_LIMIT,
                                             has_side_effects=True),
        name="experts",
    )(first_block, n_block, total, xs, w_gate, w_up, w_down)


def _combine_kernel(dest_ref, next_ref, wt_ref, x_ref, y_hbm, sg_ref, su_ref, sd_ref, g_ref, b_ref,
                    out_ref, ybuf_a, ybuf_b, sem_a, sem_b, *, alpha, sub):
    rows = x_ref.shape[0] // 2
    step = pl.program_id(0)

    def row_copy(idx_ref, tok, k, r, buf, sem):
        return pltpu.make_async_copy(y_hbm.at[_token_rows(idx_ref[tok * TOP_K + k], sub), :],
                                     buf.at[k, _token_rows(r, sub), :], sem)

    def issue_rolled(idx_ref, first_tok, buf, sem):
        def body(g, c):
            for u in range(DMA_UNROLL):
                for k in range(TOP_K):
                    r = g * DMA_UNROLL + u
                    row_copy(idx_ref, first_tok + r, k, r, buf, sem).start(priority=k % 2)
            return c
        lax.fori_loop(0, rows // DMA_UNROLL, body, 0)

    def issue_inline(idx_ref, first_tok, buf, sem):
        for r in range(rows):
            for k in range(TOP_K):
                row_copy(idx_ref, first_tok + r, k, r, buf, sem).start(priority=k % 2)

    def drain(buf, sem):
        def body(g, c):
            for u in range(DMA_UNROLL):
                for k in range(TOP_K):
                    r = g * DMA_UNROLL + u
                    pltpu.make_async_copy(y_hbm.at[_token_rows(0, sub), :],
                                          buf.at[k, _token_rows(r, sub), :], sem).wait()
            return c
        lax.fori_loop(0, rows // DMA_UNROLL, body, 0)

    def reduce_block(first_tok, buf):
        rs = slice(first_tok, first_tok + rows)
        x = x_ref[rs, :]
        xb = x.astype(BF16)
        hidden = _silu(_dot(xb, sg_ref[...])) * _dot(xb, su_ref[...])
        acc = alpha * x + _dot(hidden.astype(BF16), sd_ref[...])
        wt = wt_ref[rs, :]
        chunks = [acc[:, c * V7X_LANES:(c + 1) * V7X_LANES] for c in range(2 * sub)]
        for k in range(TOP_K):
            wk = wt[:, k:k + 1]
            for c, p in enumerate(_load_token_rows(buf, rows, (k,))):
                hi, lo = _unpack_rows(p)
                chunks[c] = chunks[c] + hi * wk
                chunks[sub + c] = chunks[sub + c] + lo * wk
        out_ref[rs, :] = _layernorm_rows(jnp.concatenate(chunks, axis=1), g_ref[...], b_ref[...])

    @pl.when(step == 0)
    def _():
        issue_rolled(dest_ref, 0, ybuf_a, sem_a)

    drain(ybuf_a, sem_a)
    issue_inline(dest_ref, rows, ybuf_b, sem_b)
    reduce_block(0, ybuf_a)
    drain(ybuf_b, sem_b)
    issue_inline(next_ref, 0, ybuf_a, sem_a)
    reduce_block(rows, ybuf_b)

    @pl.when(step == pl.num_programs(0) - 1)
    def _():
        drain(ybuf_a, sem_a)


def _combine(x1, y, dest, w_tok, sw_gate, sw_up, sw_down, ln2_g, ln2_b, alpha, sub):
    tokens, d_model = x1.shape
    d_shared = sw_gate.shape[1]
    rows = 2 * COMBINE_ROWS
    assert tokens % rows == 0
    steps = tokens // rows
    row2 = lambda a: a.reshape(1, -1).astype(F32)
    return pl.pallas_call(
        functools.partial(_combine_kernel, alpha=alpha, sub=sub),
        grid=(steps,),
        in_specs=[pl.BlockSpec((TOP_K * rows,), lambda i: (i,), memory_space=pltpu.SMEM),
                  pl.BlockSpec((TOP_K * rows,), lambda i: (jnp.minimum(i + 1, steps - 1),), memory_space=pltpu.SMEM),
                  pl.BlockSpec((rows, TOP_K), lambda i: (i, 0)),
                  pl.BlockSpec((rows, d_model), lambda i: (i, 0)),
                  pl.BlockSpec(memory_space=pl.ANY),
                  _const_spec((d_model, d_shared)),
                  _const_spec((d_model, d_shared)),
                  _const_spec((d_shared, d_model)),
                  _const_spec((1, d_model)),
                  _const_spec((1, d_model))],
        out_specs=pl.BlockSpec((rows, d_model), lambda i: (i, 0)),
        out_shape=jax.ShapeDtypeStruct((tokens, d_model), F32),
        scratch_shapes=[pltpu.VMEM((TOP_K, COMBINE_ROWS * sub, V7X_LANES), y.dtype),
                        pltpu.VMEM((TOP_K, COMBINE_ROWS * sub, V7X_LANES), y.dtype),
                        pltpu.SemaphoreType.DMA(()), pltpu.SemaphoreType.DMA(())],
        compiler_params=pltpu.CompilerParams(dimension_semantics=("arbitrary",), vmem_limit_bytes=V7X_VMEM_LIMIT),
        name="combine",
    )(dest, dest, w_tok, x1, y, sw_gate.astype(BF16), sw_up.astype(BF16), sw_down.astype(BF16),
      row2(ln2_g), row2(ln2_b))


def _block_layout(counts, tokens):
    n_exp = counts.shape[0]
    padded = (counts + ROW_BLOCK - 1) // ROW_BLOCK * ROW_BLOCK
    pad_ends = jnp.cumsum(padded)
    pad_starts = pad_ends - padded
    n_blocks = -(-(tokens * TOP_K + n_exp * (ROW_BLOCK - 1)) // ROW_BLOCK)
    return padded, pad_ends, pad_starts, pad_starts // ROW_BLOCK, padded // ROW_BLOCK, n_blocks * ROW_BLOCK


def kernel(x, w_in, ret_norm_g, gla_gate_w2, gla_gate_b, gla_norm_g, w_ret_out, w_gla_out, w_o, ln1_g, ln1_b, router_w, router_bias, exp_w_gate, exp_w_up, exp_w_down, shared_w_gate, shared_w_up, shared_w_down, ln2_g, ln2_b):
    batch, seq, d_model = x.shape
    depth = w_in.shape[0]
    alpha = (2.0 * depth) ** 0.25
    for l in range(depth):
        x1, x1p = _mix(x, w_in[l], ret_norm_g[l], gla_gate_w2[l], gla_gate_b[l], gla_norm_g[l],
                       w_ret_out[l], w_gla_out[l], w_o[l], ln1_g[l], ln1_b[l], alpha)
        x1 = x1.reshape(batch * seq, d_model)
        sub = x1p.shape[1] // seq
        x1p = x1p.reshape(batch * seq * sub, V7X_LANES)
        idx, w_sel, rank, counts = _route(x1, router_w[l], router_bias[l])
        padded, pad_ends, pad_starts, first_block, n_block, n_rows = _block_layout(
            counts[:, 0].astype(jnp.int32), batch * seq)
        dest = _plan(idx, rank, pad_starts).T.reshape(-1)
        xs = _dispatch(x1p, dest, pad_ends, padded, n_rows, sub)
        y = _experts(xs, first_block, n_block, exp_w_gate[l], exp_w_up[l], exp_w_down[l], sub)
        out = _combine(x1, y, dest, w_sel.T, shared_w_gate[l], shared_w_up[l], shared_w_down[l],
                       ln2_g[l], ln2_b[l], alpha, sub)
        x = out.reshape(batch, seq, d_model)
    return x
```

```python
import functools

import jax
import jax.numpy as jnp
import numpy as np
from jax import lax
from jax.experimental import pallas as pl
from jax.experimental.pallas import tpu as pltpu

CHUNK = 64
RET_HEADS = 4
RET_DK = 128
RET_DV = 256
GLA_HEADS = 4
GLA_DK = 128
GLA_DV = 256
GLA_GATE_RANK = 16
GLA_GATE_TAU = 16.0
ROPE_THETA = 10000.0
N_EXPERTS = 256
TOP_K = 8
N_GROUPS = 8
TOPK_GROUPS = 4
ROUTED_SCALE = 2.5
LN_EPS = 1e-5
NORM_EPS = 1e-6

V7X_LANES = 128
V7X_VMEM_LIMIT = 60 * 1024 * 1024

MIX_ROWS = 256
MIX_SEQS = 2
ROUTE_COLS = 1024
ROW_BLOCK = 128
DISPATCH_ROWS = 512
DMA_UNROLL = 2
EXPERT_GROUP = 8
EXPERT_AHEAD = 16
EXPERT_SLOTS = EXPERT_GROUP + EXPERT_AHEAD
COMBINE_ROWS = 256

F32 = jnp.float32
BF16 = jnp.bfloat16


def _dot(a, b):
    return jnp.dot(a, b, preferred_element_type=F32)


def _dot_nt(a, b):
    return lax.dot_general(a, b, (((1,), (1,)), ((), ())), preferred_element_type=F32)


def _dot_tn(a, b):
    return lax.dot_general(a, b, (((0,), (0,)), ((), ())), preferred_element_type=F32)


def _sigmoid(v):
    return 1.0 / (1.0 + jnp.exp(-v))


def _silu(v):
    return v * _sigmoid(v)


def _pack_rows(v):
    half = v.shape[1] // 2
    hi = lax.bitcast_convert_type(v[:, :half].astype(BF16).astype(F32), jnp.uint32)
    lo = lax.bitcast_convert_type(v[:, half:].astype(BF16).astype(F32), jnp.uint32)
    return hi | (lo >> 16)


def _unpack_rows(p):
    hi = lax.bitcast_convert_type(p & jnp.uint32(0xFFFF0000), F32)
    lo = lax.bitcast_convert_type(p << 16, F32)
    return hi, lo


def _store_token_rows(ref, packed, lead=()):
    m, width = packed.shape
    sub = width // V7X_LANES
    for c in range(sub):
        ref[lead + (pl.ds(c, m, stride=sub), slice(None))] = packed[:, c * V7X_LANES:(c + 1) * V7X_LANES]


def _load_token_rows(ref, m, lead=()):
    sub = ref.shape[-2] // m
    return [ref[lead + (pl.ds(c, m, stride=sub), slice(None))] for c in range(sub)]


def _layernorm_rows(v, g, b):
    mu = jnp.mean(v, axis=-1, keepdims=True)
    vc = v - mu
    var = jnp.mean(vc * vc, axis=-1, keepdims=True)
    return vc * lax.rsqrt(var + LN_EPS) * g + b


def _mix_kernel(x_ref, wrq_ref, wrk_ref, wrv_ref, wrg_ref, wgq_ref, wgk_ref, wgv_ref, wgg_ref, wga_ref, wmg_ref,
                cos_ref, sin_ref, dmask_ref, qdec_ref, kdec_ref, tri_ref, w2_ref, gb_ref, retg_ref, glag_ref,
                wro_ref, wgo_ref, wo_ref, ln1g_ref, ln1b_ref,
                out_ref, packed_ref, rstate_ref, gstate_ref, yret_ref, ygla_ref, *, block_decay, alpha):
    n_seq, rows, d_model = x_ref.shape

    @pl.when(pl.program_id(1) == 0)
    def _():
        rstate_ref[...] = jnp.zeros_like(rstate_ref)
        gstate_ref[...] = jnp.zeros_like(gstate_ref)

    x = x_ref[...].reshape(n_seq * rows, d_model)
    xb = x.astype(BF16)

    def proj(w_ref):
        return _dot(xb, w_ref[...])

    cos = cos_ref[...]
    sin = sin_ref[...]

    def rope(t):
        return t * cos + pltpu.roll(t, RET_DK // 2, 1) * sin

    rq = proj(wrq_ref)
    rk = proj(wrk_ref)
    rv = proj(wrv_ref)
    rg = proj(wrg_ref)
    seq_heads = [(s, h) for s in range(n_seq) for h in range(RET_HEADS)]
    ret_states = [rstate_ref[s * RET_HEADS + h] for s, h in seq_heads]
    for i, (s, h) in enumerate(seq_heads):
        sr = slice(s * rows, (s + 1) * rows)
        qk = slice(h * RET_DK, (h + 1) * RET_DK)
        vv = slice(h * RET_DV, (h + 1) * RET_DV)
        q = rope(rq[sr, qk])
        k = rope(rk[sr, qk])
        v = rv[sr, vv].astype(BF16)
        scores = _dot_nt(q.astype(BF16), k.astype(BF16)) * dmask_ref[h]
        o = _dot(scores.astype(BF16), v)
        state = ret_states[i]
        o = o + _dot((q * qdec_ref[h]).astype(BF16), state.astype(BF16))
        ret_states[i] = state * block_decay[h] + _dot_tn((k * kdec_ref[h]).astype(BF16), v)
        mu = jnp.mean(o, axis=-1, keepdims=True)
        oc = o - mu
        var = jnp.mean(oc * oc, axis=-1, keepdims=True)
        y = oc * lax.rsqrt(var + LN_EPS) * retg_ref[:, vv] * _silu(rg[sr, vv])
        yret_ref[sr, vv] = y.astype(BF16)
    for i, (s, h) in enumerate(seq_heads):
        rstate_ref[s * RET_HEADS + h] = ret_states[i]

    gq = proj(wgq_ref) * (GLA_DK ** -0.5)
    gk = proj(wgk_ref)
    gv = proj(wgv_ref)
    gg = proj(wgg_ref)
    ga = proj(wga_ref)
    z = _dot(ga.astype(BF16), w2_ref[...]) + gb_ref[...]
    log_a = (jnp.minimum(z, 0.0) - jnp.log1p(jnp.exp(-jnp.abs(z)))) * (1.0 / GLA_GATE_TAU)
    la_hi = log_a.astype(BF16)
    la_lo = (log_a - la_hi.astype(F32)).astype(BF16)
    tri = tri_ref[...]
    bcum = jnp.concatenate([_dot(tri, la_hi[s * rows:(s + 1) * rows]) + _dot(tri, la_lo[s * rows:(s + 1) * rows])
                            for s in range(n_seq)], axis=0)
    n_chunks = rows // CHUNK
    seq_heads = [(s, h) for s in range(n_seq) for h in range(GLA_HEADS)]
    gla_states = [gstate_ref[s * GLA_HEADS + h] for s, h in seq_heads]
    decays = []
    updates = []
    for c in range(n_seq * n_chunks):
        rs = slice(c * CHUNK, (c + 1) * CHUNK)
        b_end = bcum[(c + 1) * CHUNK - 1:(c + 1) * CHUNK, :]
        kt = (gk[rs, :] * jnp.exp(b_end - bcum[rs, :])).astype(BF16)
        decays.append(jnp.exp(b_end))
        updates.append([_dot_tn(gv[rs, h * GLA_DV:(h + 1) * GLA_DV].astype(BF16), kt[:, h * GLA_DK:(h + 1) * GLA_DK])
                        for h in range(GLA_HEADS)])
    for c in range(n_seq * n_chunks):
        rs = slice(c * CHUNK, (c + 1) * CHUNK)
        for h in range(GLA_HEADS):
            i = (c // n_chunks) * GLA_HEADS + h
            qk = slice(h * GLA_DK, (h + 1) * GLA_DK)
            vv = slice(h * GLA_DV, (h + 1) * GLA_DV)
            gla_states[i] = gla_states[i] * decays[c][:, qk] + updates[c][h]
            o = _dot_nt(gq[rs, qk].astype(BF16), gla_states[i].astype(BF16))
            ms = jnp.mean(o * o, axis=-1, keepdims=True)
            y = o * lax.rsqrt(ms + NORM_EPS) * glag_ref[:, vv] * _silu(gg[rs, vv])
            ygla_ref[rs, vv] = y.astype(BF16)
    for i, (s, h) in enumerate(seq_heads):
        gstate_ref[s * GLA_HEADS + h] = gla_states[i]

    u_ret = _dot(yret_ref[...], wro_ref[...])
    u_gla = _dot(ygla_ref[...], wgo_ref[...])
    gate = _sigmoid(proj(wmg_ref))
    merged = gate[:, :d_model] * u_ret + gate[:, d_model:] * u_gla
    mix = _dot(merged.astype(BF16), wo_ref[...])
    out = _layernorm_rows(alpha * x + mix, ln1g_ref[...], ln1b_ref[...])
    out_ref[...] = out.reshape(n_seq, rows, d_model)
    packed = _pack_rows(out)
    for s in range(n_seq):
        _store_token_rows(packed_ref, packed[s * rows:(s + 1) * rows], (s,))


def _mix_tables(seq, rows):
    half = RET_DK // 2
    inv = ROPE_THETA ** (-np.arange(half, dtype=np.float64) / half)
    ang = np.arange(seq, dtype=np.float64)[:, None] * inv[None, :]
    cos2 = np.concatenate([np.cos(ang), np.cos(ang)], axis=1)
    sin2 = np.concatenate([-np.sin(ang), np.sin(ang)], axis=1)
    log_g = np.log1p(-np.exp2(-5.0 - np.arange(RET_HEADS, dtype=np.float64)))
    j = np.arange(rows, dtype=np.float64)
    same_or_earlier_chunk = (j[None, :] // CHUNK) <= (j[:, None] // CHUNK)
    k_scale = RET_DK ** -0.5
    dmask = np.exp(log_g[:, None, None] * np.abs(j[:, None] - j[None, :])) * same_or_earlier_chunk[None] * k_scale
    qdec = np.exp(log_g[:, None] * (j[None, :] + 1.0))
    kdec = np.exp(log_g[:, None] * (rows - 1.0 - j[None, :])) * k_scale
    qdec = np.broadcast_to(qdec[:, :, None], (RET_HEADS, rows, RET_DK))
    kdec = np.broadcast_to(kdec[:, :, None], (RET_HEADS, rows, RET_DK))
    block_decay = tuple(float(v) for v in np.exp(log_g * rows))
    tri = ((j[None, :] <= j[:, None]) & ((j[None, :] // CHUNK) == (j[:, None] // CHUNK)))
    to = lambda a, dt: jnp.asarray(np.ascontiguousarray(a), dtype=dt)
    return (to(cos2, F32), to(sin2, F32), to(dmask, F32), to(qdec, F32), to(kdec, F32), to(tri, BF16)), block_decay


def _const_spec(shape):
    nd = len(shape)
    return pl.BlockSpec(shape, lambda *_: (0,) * nd, pipeline_mode=pl.Buffered(1))


def _mix(x, w_in, ret_norm_g, gla_gate_w2, gla_gate_b, gla_norm_g, w_ret_out, w_gla_out, w_o, ln1_g, ln1_b, alpha):
    batch, seq, d_model = x.shape
    rows = MIX_ROWS
    assert seq % rows == 0 and rows % CHUNK == 0
    ret_qk, ret_v = RET_HEADS * RET_DK, RET_HEADS * RET_DV
    gla_qk, gla_v = GLA_HEADS * GLA_DK, GLA_HEADS * GLA_DV
    splits = (ret_qk, ret_qk, ret_v, ret_v, gla_qk, gla_qk, gla_v, gla_v, GLA_GATE_RANK, 2 * d_model)
    assert w_in.shape == (d_model, sum(splits))
    offs = np.cumsum((0,) + splits)
    parts = [w_in[:, offs[i]:offs[i + 1]].astype(BF16) for i in range(len(splits))]
    parts[8] = jnp.pad(parts[8], ((0, 0), (0, V7X_LANES - GLA_GATE_RANK)))
    w2 = jnp.pad(gla_gate_w2.astype(BF16), ((0, V7X_LANES - GLA_GATE_RANK), (0, 0)))
    (cos2, sin2, dmask, qdec, kdec, tri), block_decay = _mix_tables(seq, rows)
    row2 = lambda a: a.reshape(1, -1).astype(F32)
    consts = [dmask, qdec, kdec, tri, w2, row2(gla_gate_b), row2(ret_norm_g), row2(gla_norm_g),
              w_ret_out.astype(BF16), w_gla_out.astype(BF16), w_o.astype(BF16), row2(ln1_g), row2(ln1_b)]
    sub = d_model // 2 // V7X_LANES
    pos_spec = pl.BlockSpec((rows, RET_DK), lambda b, s: (s, 0))
    n_seq = MIX_SEQS
    assert batch % n_seq == 0
    in_specs = ([pl.BlockSpec((n_seq, rows, d_model), lambda b, s: (b, s, 0))]
                + [_const_spec(p.shape) for p in parts]
                + [pos_spec, pos_spec]
                + [_const_spec(c.shape) for c in consts])
    return pl.pallas_call(
        functools.partial(_mix_kernel, block_decay=block_decay, alpha=alpha),
        grid=(batch // n_seq, seq // rows),
        in_specs=in_specs,
        out_specs=[pl.BlockSpec((n_seq, rows, d_model), lambda b, s: (b, s, 0)),
                   pl.BlockSpec((n_seq, rows * sub, V7X_LANES), lambda b, s: (b, s, 0))],
        out_shape=[jax.ShapeDtypeStruct((batch, seq, d_model), F32),
                   jax.ShapeDtypeStruct((batch, seq * sub, V7X_LANES), jnp.uint32)],
        scratch_shapes=[pltpu.VMEM((n_seq * RET_HEADS, RET_DK, RET_DV), F32),
                        pltpu.VMEM((n_seq * GLA_HEADS, GLA_DV, GLA_DK), F32),
                        pltpu.VMEM((n_seq * rows, ret_v), BF16),
                        pltpu.VMEM((n_seq * rows, gla_v), BF16)],
        compiler_params=pltpu.CompilerParams(dimension_semantics=("arbitrary", "arbitrary"),
                                             vmem_limit_bytes=V7X_VMEM_LIMIT),
        name="mix",
    )(x, *parts, cos2, sin2, *consts)


def _route_kernel(x_ref, rwt_ref, bias_ref, triu_ref, ones_ref,
                  idx_ref, w_ref, rank_ref, counts_ref, carry_ref):
    cols = x_ref.shape[0]
    n_exp = rwt_ref.shape[0]
    per_group = n_exp // N_GROUPS
    neg_inf = -jnp.inf

    @pl.when(pl.program_id(0) == 0)
    def _():
        carry_ref[...] = jnp.zeros_like(carry_ref)

    logits = _dot_nt(rwt_ref[...], x_ref[...].astype(BF16))
    scores = _sigmoid(logits)
    biased = scores + bias_ref[...]

    sub = lax.broadcasted_iota(jnp.int32, (per_group, cols), 0)
    gscore = []
    for g in range(N_GROUPS):
        blk = biased[g * per_group:(g + 1) * per_group, :]
        m1 = jnp.max(blk, axis=0, keepdims=True)
        i1 = jnp.min(jnp.where(blk == m1, sub, per_group), axis=0, keepdims=True)
        m2 = jnp.max(jnp.where(sub == i1, neg_inf, blk), axis=0, keepdims=True)
        gscore.append(m1 + m2)
    masked = []
    for g in range(N_GROUPS):
        ahead = jnp.zeros((1, cols), jnp.int32)
        for o in range(N_GROUPS):
            if o == g:
                continue
            before = (gscore[o] >= gscore[g]) if o < g else (gscore[o] > gscore[g])
            ahead = ahead + before.astype(jnp.int32)
        keep = ahead < TOPK_GROUPS
        blk = biased[g * per_group:(g + 1) * per_group, :]
        masked.append(jnp.where(keep, blk, neg_inf))
    candidates = jnp.concatenate(masked, axis=0)
    cur = candidates

    rowid = lax.broadcasted_iota(jnp.int32, (n_exp, cols), 0)
    picked = []
    weights = []
    for _ in range(TOP_K):
        m = jnp.max(cur, axis=0, keepdims=True)
        ik = jnp.min(jnp.where(cur == m, rowid, n_exp), axis=0, keepdims=True)
        sel = rowid == ik
        weights.append(jnp.sum(jnp.where(sel, scores, 0.0), axis=0, keepdims=True))
        cur = jnp.where(sel, neg_inf, cur)
        picked.append(ik)
    wsum = weights[0]
    for wk in weights[1:]:
        wsum = wsum + wk

    chosen = (cur == neg_inf) & (candidates != neg_inf)
    chosen_b = jnp.where(chosen, 1.0, 0.0).astype(BF16)
    carry = carry_ref[...]
    before = _dot(chosen_b, triu_ref[...]) + jnp.concatenate([carry] * (cols // V7X_LANES), axis=1)
    for k in range(TOP_K):
        sel = rowid == picked[k]
        rank_ref[k:k + 1, :] = jnp.sum(jnp.where(sel, before, 0.0), axis=0, keepdims=True).astype(jnp.int32)
        idx_ref[k:k + 1, :] = picked[k]
        w_ref[k:k + 1, :] = weights[k] / wsum * ROUTED_SCALE
    carry = carry + _dot(chosen_b, ones_ref[...])
    carry_ref[...] = carry
    counts_ref[...] = carry


def _route(x1, router_w, router_bias):
    tokens, d_model = x1.shape
    n_exp = router_w.shape[1]
    cols = ROUTE_COLS
    assert tokens % cols == 0 and n_exp % N_GROUPS == 0
    j = np.arange(cols)
    triu = jnp.asarray((j[:, None] < j[None, :]), dtype=BF16)
    ones = jnp.ones((cols, V7X_LANES), BF16)
    out_row = lambda dt: jax.ShapeDtypeStruct((TOP_K, tokens), dt)
    row_spec = pl.BlockSpec((TOP_K, cols), lambda i: (0, i))
    return pl.pallas_call(
        _route_kernel,
        grid=(tokens // cols,),
        in_specs=[pl.BlockSpec((cols, d_model), lambda i: (i, 0)),
                  _const_spec((n_exp, d_model)),
                  _const_spec((n_exp, 1)),
                  _const_spec((cols, cols)),
                  _const_spec((cols, V7X_LANES))],
        out_specs=[row_spec, row_spec, row_spec, pl.BlockSpec((n_exp, V7X_LANES), lambda i: (0, 0))],
        out_shape=[out_row(jnp.int32), out_row(F32), out_row(jnp.int32),
                   jax.ShapeDtypeStruct((n_exp, V7X_LANES), F32)],
        scratch_shapes=[pltpu.VMEM((n_exp, V7X_LANES), F32)],
        compiler_params=pltpu.CompilerParams(dimension_semantics=("arbitrary",), vmem_limit_bytes=V7X_VMEM_LIMIT),
        name="route",
    )(x1, router_w.T.astype(BF16), router_bias.reshape(n_exp, 1).astype(F32), triu, ones)


def _plan_kernel(idx_ref, rank_ref, pstart_ref, dest_ref):
    n_exp = pstart_ref.shape[0]
    cols = idx_ref.shape[1]
    rowid = lax.broadcasted_iota(jnp.int32, (n_exp, cols), 0)
    pstart = pstart_ref[...]
    for k in range(TOP_K):
        base = jnp.sum(jnp.where(rowid == idx_ref[k:k + 1, :], pstart, 0.0), axis=0, keepdims=True)
        dest_ref[k:k + 1, :] = base.astype(jnp.int32) + rank_ref[k:k + 1, :]


def _plan(idx, rank, pad_starts):
    tokens = idx.shape[1]
    n_exp = pad_starts.shape[0]
    cols = ROUTE_COLS
    row_spec = pl.BlockSpec((TOP_K, cols), lambda i: (0, i))
    return pl.pallas_call(
        _plan_kernel,
        grid=(tokens // cols,),
        in_specs=[row_spec, row_spec, _const_spec((n_exp, 1))],
        out_specs=row_spec,
        out_shape=jax.ShapeDtypeStruct((TOP_K, tokens), jnp.int32),
        compiler_params=pltpu.CompilerParams(dimension_semantics=("arbitrary",)),
        name="plan",
    )(idx, rank, pad_starts.reshape(n_exp, 1).astype(F32))


def _token_rows(tok, sub):
    return pl.ds(pl.multiple_of(tok * sub, sub), sub)


def _dispatch_kernel(pad_end_ref, padded_ref, dest_ref, x_ref, xs_hbm, zero_ref, sem, *, sub):
    rows = x_ref.shape[0] // sub
    n_exp = pad_end_ref.shape[0]

    def zero_copy(e):
        start = pl.multiple_of((pad_end_ref[e] - ROW_BLOCK) * sub, ROW_BLOCK * sub)
        return pltpu.make_async_copy(zero_ref, xs_hbm.at[pl.ds(start, ROW_BLOCK * sub), :], sem)

    @pl.when(pl.program_id(0) == 0)
    def _():
        zero_ref[...] = jnp.zeros_like(zero_ref)

        def issue(e, c):
            @pl.when(padded_ref[e] > 0)
            def _():
                zero_copy(e).start()
            return c
        lax.fori_loop(0, n_exp, issue, 0)

        def drain(e, c):
            @pl.when(padded_ref[e] > 0)
            def _():
                zero_copy(e).wait()
            return c
        lax.fori_loop(0, n_exp, drain, 0)

    def row_copy(k, r):
        return pltpu.make_async_copy(x_ref.at[_token_rows(r, sub), :],
                                     xs_hbm.at[_token_rows(dest_ref[r * TOP_K + k], sub), :], sem)

    def issue(g, c):
        for u in range(DMA_UNROLL):
            for k in range(TOP_K):
                row_copy(k, g * DMA_UNROLL + u).start(priority=k % 2)
        return c
    lax.fori_loop(0, rows // DMA_UNROLL, issue, 0)

    def drain(g, c):
        for u in range(DMA_UNROLL):
            for k in range(TOP_K):
                row_copy(k, g * DMA_UNROLL + u).wait()
        return c
    lax.fori_loop(0, rows // DMA_UNROLL, drain, 0)


def _dispatch(x1p, dest, pad_ends, padded, n_rows, sub):
    tokens = x1p.shape[0] // sub
    rows = DISPATCH_ROWS
    grid_spec = pltpu.PrefetchScalarGridSpec(
        num_scalar_prefetch=2,
        grid=(tokens // rows,),
        in_specs=[pl.BlockSpec((TOP_K * rows,), lambda i, pe, pd: (i,), memory_space=pltpu.SMEM),
                  pl.BlockSpec((rows * sub, V7X_LANES), lambda i, pe, pd: (i, 0))],
        out_specs=pl.BlockSpec(memory_space=pl.ANY),
        scratch_shapes=[pltpu.VMEM((ROW_BLOCK * sub, V7X_LANES), x1p.dtype), pltpu.SemaphoreType.DMA(())],
    )
    return pl.pallas_call(
        functools.partial(_dispatch_kernel, sub=sub),
        grid_spec=grid_spec,
        out_shape=jax.ShapeDtypeStruct((n_rows * sub, V7X_LANES), x1p.dtype),
        compiler_params=pltpu.CompilerParams(dimension_semantics=("arbitrary",), has_side_effects=True),
        name="dispatch",
    )(pad_ends, padded, dest, x1p)


def _experts_kernel(first_ref, nblk_ref, total_ref, xs_hbm, wg_ref, wu_ref, wd_ref, y_hbm,
                    xbuf_ref, ybuf_ref, wgu_b, wd_b, in_sem, out_sem, *, sub):
    e = pl.program_id(0)
    n = nblk_ref[e]
    first = first_ref[e]
    total = total_ref[0]
    ahead = EXPERT_AHEAD
    block = ROW_BLOCK * sub

    def block_rows(g):
        return pl.ds(pl.multiple_of(g * block, block), block)

    def in_copy(g):
        slot = lax.rem(g, EXPERT_SLOTS)
        return pltpu.make_async_copy(xs_hbm.at[block_rows(g), :], xbuf_ref.at[slot], in_sem.at[slot])

    def out_copy(g):
        slot = lax.rem(g, EXPERT_SLOTS)
        return pltpu.make_async_copy(ybuf_ref.at[slot], y_hbm.at[block_rows(g), :], out_sem.at[slot])

    @pl.when(e == 0)
    def _():
        for g in range(ahead):
            @pl.when(g < total)
            def _():
                in_copy(g).start()

    @pl.when(n > 0)
    def _():
        d_exp = wg_ref.shape[1]
        wgu_b[:, :d_exp] = wg_ref[...].astype(BF16)
        wgu_b[:, d_exp:] = wu_ref[...].astype(BF16)
        wd_b[...] = wd_ref[...].astype(BF16)

        def stage(g):
            slot = lax.rem(g, EXPERT_SLOTS)
            in_copy(g).wait()

            @pl.when(g + ahead < total)
            def _():
                in_copy(g + ahead).start()

            @pl.when(g >= EXPERT_SLOTS)
            def _():
                out_copy(g - EXPERT_SLOTS).wait()

            halves = [_unpack_rows(p) for p in _load_token_rows(xbuf_ref, ROW_BLOCK, (slot,))]
            return jnp.concatenate([h for h, _ in halves] + [l for _, l in halves], axis=1).astype(BF16)

        def swiglu(xb):
            gate_up = _dot(xb, wgu_b[...])
            hidden = _silu(gate_up[:, :d_exp]) * gate_up[:, d_exp:]
            return _pack_rows(_dot(hidden.astype(BF16), wd_b[...]))

        def finish(g, packed):
            _store_token_rows(ybuf_ref, packed, (lax.rem(g, EXPERT_SLOTS),))
            out_copy(g).start()

        def group(g, count):
            xbs = [stage(g + i) for i in range(count)]
            y = swiglu(xbs[0] if count == 1 else jnp.concatenate(xbs, axis=0))
            for i in range(count):
                finish(g + i, y[i * ROW_BLOCK:(i + 1) * ROW_BLOCK])

        def full_groups(q, c):
            group(first + EXPERT_GROUP * q, EXPERT_GROUP)
            return c
        lax.fori_loop(0, n // EXPERT_GROUP, full_groups, 0)

        done = n - lax.rem(n, EXPERT_GROUP)
        size = EXPERT_GROUP // 2
        while size >= 1:
            take = lax.rem(n, 2 * size) >= size

            @pl.when(take)
            def _(done=done, size=size):
                group(first + done, size)
            done = done + jnp.where(take, size, 0)
            size //= 2

    @pl.when(e == pl.num_programs(0) - 1)
    def _():
        for d in range(EXPERT_SLOTS):
            @pl.when(total - 1 - d >= 0)
            def _():
                out_copy(total - 1 - d).wait()


def _experts(xs, first_block, n_block, w_gate, w_up, w_down, sub):
    n_exp, d_model, d_exp = w_gate.shape
    assert sub * V7X_LANES * 2 == d_model and xs.shape[0] % (ROW_BLOCK * sub) == 0
    total = jnp.sum(n_block).reshape(1).astype(jnp.int32)
    grid_spec = pltpu.PrefetchScalarGridSpec(
        num_scalar_prefetch=3,
        grid=(n_exp,),
        in_specs=[pl.BlockSpec(memory_space=pl.ANY),
                  pl.BlockSpec((None, d_model, d_exp), lambda e, fb, nb, tt: (e, 0, 0)),
                  pl.BlockSpec((None, d_model, d_exp), lambda e, fb, nb, tt: (e, 0, 0)),
                  pl.BlockSpec((None, d_exp, d_model), lambda e, fb, nb, tt: (e, 0, 0))],
        out_specs=pl.BlockSpec(memory_space=pl.ANY),
        scratch_shapes=[pltpu.VMEM((EXPERT_SLOTS, ROW_BLOCK * sub, V7X_LANES), xs.dtype),
                        pltpu.VMEM((EXPERT_SLOTS, ROW_BLOCK * sub, V7X_LANES), xs.dtype),
                        pltpu.VMEM((d_model, 2 * d_exp), BF16),
                        pltpu.VMEM((d_exp, d_model), BF16),
                        pltpu.SemaphoreType.DMA((EXPERT_SLOTS,)),
                        pltpu.SemaphoreType.DMA((EXPERT_SLOTS,))],
    )
    return pl.pallas_call(
        functools.partial(_experts_kernel, sub=sub),
        grid_spec=grid_spec,
        out_shape=jax.ShapeDtypeStruct(xs.shape, xs.dtype),
        compiler_params=pltpu.CompilerParams(dimension_semantics=("arbitrary",), vmem_limit_bytes=V7X_VMEM_LIMIT,
                                             has_side_effects=True),
        name="experts",
    )(first_block, n_block, total, xs, w_gate, w_up, w_down)


def _combine_kernel(dest_ref, next_ref, wt_ref, x_ref, y_hbm, sg_ref, su_ref, sd_ref, g_ref, b_ref,
                    out_ref, ybuf_a, ybuf_b, sem_a, sem_b, *, alpha, sub):
    rows = x_ref.shape[0] // 2
    step = pl.program_id(0)

    def row_copy(idx_ref, tok, k, r, buf, sem):
        return pltpu.make_async_copy(y_hbm.at[_token_rows(idx_ref[tok * TOP_K + k], sub), :],
                                     buf.at[k, _token_rows(r, sub), :], sem)

    def issue_rolled(idx_ref, first_tok, buf, sem):
        def body(g, c):
            for u in range(DMA_UNROLL):
                for k in range(TOP_K):
                    r = g * DMA_UNROLL + u
                    row_copy(idx_ref, first_tok + r, k, r, buf, sem).start(priority=k % 2)
            return c
        lax.fori_loop(0, rows // DMA_UNROLL, body, 0)

    def issue_inline(idx_ref, first_tok, buf, sem):
        for r in range(rows):
            for k in range(TOP_K):
                row_copy(idx_ref, first_tok + r, k, r, buf, sem).start(priority=k % 2)

    def drain(buf, sem):
        def body(g, c):
            for u in range(DMA_UNROLL):
                for k in range(TOP_K):
                    r = g * DMA_UNROLL + u
                    pltpu.make_async_copy(y_hbm.at[_token_rows(0, sub), :],
                                          buf.at[k, _token_rows(r, sub), :], sem).wait()
            return c
        lax.fori_loop(0, rows // DMA_UNROLL, body, 0)

    def reduce_block(first_tok, buf):
        rs = slice(first_tok, first_tok + rows)
        x = x_ref[rs, :]
        xb = x.astype(BF16)
        hidden = _silu(_dot(xb, sg_ref[...])) * _dot(xb, su_ref[...])
        acc = alpha * x + _dot(hidden.astype(BF16), sd_ref[...])
        wt = wt_ref[rs, :]
        chunks = [acc[:, c * V7X_LANES:(c + 1) * V7X_LANES] for c in range(2 * sub)]
        for k in range(TOP_K):
            wk = wt[:, k:k + 1]
            for c, p in enumerate(_load_token_rows(buf, rows, (k,))):
                hi, lo = _unpack_rows(p)
                chunks[c] = chunks[c] + hi * wk
                chunks[sub + c] = chunks[sub + c] + lo * wk
        out_ref[rs, :] = _layernorm_rows(jnp.concatenate(chunks, axis=1), g_ref[...], b_ref[...])

    @pl.when(step == 0)
    def _():
        issue_rolled(dest_ref, 0, ybuf_a, sem_a)

    drain(ybuf_a, sem_a)
    issue_inline(dest_ref, rows, ybuf_b, sem_b)
    reduce_block(0, ybuf_a)
    drain(ybuf_b, sem_b)
    issue_inline(next_ref, 0, ybuf_a, sem_a)
    reduce_block(rows, ybuf_b)

    @pl.when(step == pl.num_programs(0) - 1)
    def _():
        drain(ybuf_a, sem_a)


def _combine(x1, y, dest, w_tok, sw_gate, sw_up, sw_down, ln2_g, ln2_b, alpha, sub):
    tokens, d_model = x1.shape
    d_shared = sw_gate.shape[1]
    rows = 2 * COMBINE_ROWS
    assert tokens % rows == 0
    steps = tokens // rows
    row2 = lambda a: a.reshape(1, -1).astype(F32)
    return pl.pallas_call(
        functools.partial(_combine_kernel, alpha=alpha, sub=sub),
        grid=(steps,),
        in_specs=[pl.BlockSpec((TOP_K * rows,), lambda i: (i,), memory_space=pltpu.SMEM),
                  pl.BlockSpec((TOP_K * rows,), lambda i: (jnp.minimum(i + 1, steps - 1),), memory_space=pltpu.SMEM),
                  pl.BlockSpec((rows, TOP_K), lambda i: (i, 0)),
                  pl.BlockSpec((rows, d_model), lambda i: (i, 0)),
                  pl.BlockSpec(memory_space=pl.ANY),
                  _const_spec((d_model, d_shared)),
                  _const_spec((d_model, d_shared)),
                  _const_spec((d_shared, d_model)),
                  _const_spec((1, d_model)),
                  _const_spec((1, d_model))],
        out_specs=pl.BlockSpec((rows, d_model), lambda i: (i, 0)),
        out_shape=jax.ShapeDtypeStruct((tokens, d_model), F32),
        scratch_shapes=[pltpu.VMEM((TOP_K, COMBINE_ROWS * sub, V7X_LANES), y.dtype),
                        pltpu.VMEM((TOP_K, COMBINE_ROWS * sub, V7X_LANES), y.dtype),
                        pltpu.SemaphoreType.DMA(()), pltpu.SemaphoreType.DMA(())],
        compiler_params=pltpu.CompilerParams(dimension_semantics=("arbitrary",), vmem_limit_bytes=V7X_VMEM_LIMIT),
        name="combine",
    )(dest, dest, w_tok, x1, y, sw_gate.astype(BF16), sw_up.astype(BF16), sw_down.astype(BF16),
      row2(ln2_g), row2(ln2_b))


def _block_layout(counts, tokens):
    n_exp = counts.shape[0]
    padded = (counts + ROW_BLOCK - 1) // ROW_BLOCK * ROW_BLOCK
    pad_ends = jnp.cumsum(padded)
    pad_starts = pad_ends - padded
    n_blocks = -(-(tokens * TOP_K + n_exp * (ROW_BLOCK - 1)) // ROW_BLOCK)
    return padded, pad_ends, pad_starts, pad_starts // ROW_BLOCK, padded // ROW_BLOCK, n_blocks * ROW_BLOCK


def kernel(x, w_in, ret_norm_g, gla_gate_w2, gla_gate_b, gla_norm_g, w_ret_out, w_gla_out, w_o, ln1_g, ln1_b, router_w, router_bias, exp_w_gate, exp_w_up, exp_w_down, shared_w_gate, shared_w_up, shared_w_down, ln2_g, ln2_b):
    batch, seq, d_model = x.shape
    depth = w_in.shape[0]
    alpha = (2.0 * depth) ** 0.25
    for l in range(depth):
        x1, x1p = _mix(x, w_in[l], ret_norm_g[l], gla_gate_w2[l], gla_gate_b[l], gla_norm_g[l],
                       w_ret_out[l], w_gla_out[l], w_o[l], ln1_g[l], ln1_b[l], alpha)
        x1 = x1.reshape(batch * seq, d_model)
        sub = x1p.shape[1] // seq
        x1p = x1p.reshape(batch * seq * sub, V7X_LANES)
        idx, w_sel, rank, counts = _route(x1, router_w[l], router_bias[l])
        padded, pad_ends, pad_starts, first_block, n_block, n_rows = _block_layout(
            counts[:, 0].astype(jnp.int32), batch * seq)
        dest = _plan(idx, rank, pad_starts).T.reshape(-1)
        xs = _dispatch(x1p, dest, pad_ends, padded, n_rows, sub)
        y = _experts(xs, first_block, n_block, exp_w_gate[l], exp_w_up[l], exp_w_down[l], sub)
        out = _combine(x1, y, dest, w_sel.T, shared_w_gate[l], shared_w_up[l], shared_w_down[l],
                       ln2_g[l], ln2_b[l], alpha, sub)
        x = out.reshape(batch, seq, d_model)
    return x
```
